```python
import jax, jax.numpy as jnp
from jax import lax
import numpy as np

D_MODEL = 2048
BATCH = 2
SEQ = 4096
DEPTH = 2

N_BRANCHES = 4
BRANCH_WIDTH = D_MODEL // N_BRANCHES
HEAD_DIM = 128
N_SB_HEADS = BRANCH_WIDTH // HEAD_DIM
N_FOX_HEADS = BRANCH_WIDTH // HEAD_DIM
N_MOBA_HEADS = BRANCH_WIDTH // HEAD_DIM
N_GLA_HEADS = 4
GLA_VAL_DIM = BRANCH_WIDTH
GLA_KEY_DIM = BRANCH_WIDTH // 2
GLA_HEAD_K = GLA_KEY_DIM // N_GLA_HEADS
GLA_HEAD_V = GLA_VAL_DIM // N_GLA_HEADS
GLA_GATE_RANK = 16
GLA_GATE_NORMALIZER = 16.0
GLA_CHUNK = 64
QUERY_BLOCK = 128
MOBA_BLOCK = 256
MOBA_TOPK = 3
MOBA_QUERY_BLOCK = 64
ROPE_THETA = 500000.0
ROPE_DIM = HEAD_DIM // 4
FORGET_BIAS_INIT = 4.0
IN_SPLITS = (3 * N_SB_HEADS * HEAD_DIM,
             3 * N_FOX_HEADS * HEAD_DIM,
             N_FOX_HEADS,
             3 * N_MOBA_HEADS * HEAD_DIM,
             GLA_KEY_DIM, GLA_KEY_DIM,
             GLA_VAL_DIM, GLA_VAL_DIM,
             GLA_GATE_RANK,
             N_BRANCHES * D_MODEL)
IN_WIDTH = sum(IN_SPLITS)
N_EXPERTS = 64
N_EXPERT_GROUPS = 8
TOPK_GROUPS = 4
TOP_K = 8
EXPERT_HIDDEN = D_MODEL // 4
SHARED_HIDDEN = D_MODEL // 4
ROUTED_SCALE = 2.5
EXPERT_ROW_BLOCK = 128
RMS_EPS = 1e-6

kernel_name = "hybrid_sb_fox_moba_gla_moe_block"


def rms_norm(x, g):
    xf = x.astype(jnp.float32)
    y = xf * lax.rsqrt(jnp.mean(xf * xf, axis=-1, keepdims=True) + RMS_EPS)
    return (y * g.astype(jnp.float32)).astype(x.dtype)


def split_heads(t, n):
    B, S, _ = t.shape
    return t.reshape(B, S, n, -1).transpose(0, 2, 1, 3)


def merge_heads(t):
    B, n, S, d = t.shape
    return t.transpose(0, 2, 1, 3).reshape(B, S, n * d)


def partial_rope(x, positions):
    half = ROPE_DIM // 2
    inv_freq = jnp.power(ROPE_THETA, -jnp.arange(half, dtype=jnp.float32) * 2.0 / ROPE_DIM)
    ang = positions.astype(jnp.float32)[:, None, :, None] * inv_freq
    cos, sin = jnp.cos(ang), jnp.sin(ang)
    x1 = x[..., :half].astype(jnp.float32)
    x2 = x[..., half:ROPE_DIM].astype(jnp.float32)
    rot = jnp.concatenate([x1 * cos - x2 * sin, x2 * cos + x1 * sin], axis=-1)
    return jnp.concatenate([rot.astype(x.dtype), x[..., ROPE_DIM:]], axis=-1)


def stick_breaking_attention(q, k, v):
    B, H, S, dh = q.shape
    nq = S // QUERY_BLOCK
    scale = dh ** -0.5
    kpos = jnp.arange(S)
    qb = q.reshape(B, H, nq, QUERY_BLOCK, dh).transpose(2, 0, 1, 3, 4)

    def block(args):
        qi, i = args
        qpos = i * QUERY_BLOCK + jnp.arange(QUERY_BLOCK)
        z = jnp.einsum('bhqd,bhkd->bhqk', qi, k, preferred_element_type=jnp.float32) * scale
        past = kpos[None, :] < qpos[:, None]
        log_stay = jnp.where(past, jax.nn.log_sigmoid(-z), 0.0)
        between = lax.cumsum(log_stay, axis=3, reverse=True) - log_stay
        w = jnp.where(past, jnp.exp(jax.nn.log_sigmoid(z) + between), 0.0)
        return jnp.einsum('bhqk,bhkd->bhqd', w.astype(v.dtype), v)

    o = lax.map(block, (qb, jnp.arange(nq)))
    return o.transpose(1, 2, 0, 3, 4).reshape(B, H, S, dh)


def forgetting_attention(q, k, v, log_f):
    B, H, S, dh = q.shape
    nq = S // QUERY_BLOCK
    scale = dh ** -0.5
    F = jnp.cumsum(log_f, axis=-1)
    kpos = jnp.arange(S)
    qb = q.reshape(B, H, nq, QUERY_BLOCK, dh).transpose(2, 0, 1, 3, 4)
    Fb = F.reshape(B, H, nq, QUERY_BLOCK).transpose(2, 0, 1, 3)

    def block(args):
        qi, Fi, i = args
        qpos = i * QUERY_BLOCK + jnp.arange(QUERY_BLOCK)
        s = jnp.einsum('bhqd,bhkd->bhqk', qi, k, preferred_element_type=jnp.float32) * scale
        s = s + Fi[..., None] - F[:, :, None, :]
        s = jnp.where(kpos[None, :] <= qpos[:, None], s, -jnp.inf)
        p = jax.nn.softmax(s, axis=-1)
        return jnp.einsum('bhqk,bhkd->bhqd', p.astype(v.dtype), v)

    o = lax.map(block, (qb, Fb, jnp.arange(nq)))
    return o.transpose(1, 2, 0, 3, 4).reshape(B, H, S, dh)


def moba_attention(q, k, v):
    B, H, S, dh = q.shape
    scale = dh ** -0.5
    nbp = -(-S // MOBA_BLOCK)
    Sp = nbp * MOBA_BLOCK
    pad = ((0, 0), (0, 0), (0, Sp - S), (0, 0))
    kp, vp = jnp.pad(k, pad), jnp.pad(v, pad)
    kb = kp.reshape(B, H, nbp, MOBA_BLOCK, dh)
    vb = vp.reshape(B, H, nbp, MOBA_BLOCK, dh)
    kmean = jnp.mean(kb.astype(jnp.float32), axis=3)
    gate = jnp.einsum('bhsd,bhnd->bhsn', q.astype(jnp.float32), kmean)
    qblk = jnp.arange(S) // MOBA_BLOCK
    fully_past = jnp.arange(nbp)[None, :] < qblk[:, None]
    gate = jnp.where(fully_past, gate, -jnp.inf)
    n_sel = min(MOBA_TOPK, nbp)
    top_score, top_idx = lax.top_k(gate, n_sel)
    top_valid = jnp.isfinite(top_score)

    Q = MOBA_QUERY_BLOCK
    nq = S // Q
    qb = q.reshape(B, H, nq, Q, dh).transpose(2, 0, 1, 3, 4)
    idxb = top_idx.reshape(B, H, nq, Q, n_sel).transpose(2, 0, 1, 3, 4)
    valb = top_valid.reshape(B, H, nq, Q, n_sel).transpose(2, 0, 1, 3, 4)
    bi = jnp.arange(B)[:, None, None, None]
    hi = jnp.arange(H)[None, :, None, None]

    def block(args):
        qi, idx_i, val_i, i = args
        qpos = i * Q + jnp.arange(Q)
        k_sel = kb[bi, hi, idx_i]
        v_sel = vb[bi, hi, idx_i]
        s_sel = jnp.einsum('bhqd,bhqnkd->bhqnk', qi, k_sel,
                           preferred_element_type=jnp.float32) * scale
        s_sel = jnp.where(val_i[..., None], s_sel, -jnp.inf).reshape(B, H, Q, n_sel * MOBA_BLOCK)
        own = (i * Q) // MOBA_BLOCK
        k_own = lax.dynamic_slice_in_dim(kp, own * MOBA_BLOCK, MOBA_BLOCK, axis=2)
        v_own = lax.dynamic_slice_in_dim(vp, own * MOBA_BLOCK, MOBA_BLOCK, axis=2)
        s_own = jnp.einsum('bhqd,bhkd->bhqk', qi, k_own,
                           preferred_element_type=jnp.float32) * scale
        own_pos = own * MOBA_BLOCK + jnp.arange(MOBA_BLOCK)
        s_own = jnp.where(own_pos[None, :] <= qpos[:, None], s_own, -jnp.inf)
        p = jax.nn.softmax(jnp.concatenate([s_sel, s_own], axis=-1), axis=-1).astype(v.dtype)
        p_sel = p[..., :n_sel * MOBA_BLOCK].reshape(B, H, Q, n_sel, MOBA_BLOCK)
        p_own = p[..., n_sel * MOBA_BLOCK:]
        return (jnp.einsum('bhqnk,bhqnkd->bhqd', p_sel, v_sel)
                + jnp.einsum('bhqk,bhkd->bhqd', p_own, v_own))

    o = lax.map(block, (qb, idxb, valb, jnp.arange(nq)))
    return o.transpose(1, 2, 0, 3, 4).reshape(B, H, S, dh)


def gla_chunked(q, k, v, log_a):
    B, H, S, dk = q.shape
    dv = v.shape[-1]
    C = GLA_CHUNK
    n = S // C
    f32 = jnp.float32
    qc = (q.astype(f32) * dk ** -0.5).reshape(B, H, n, C, dk)
    kc = k.astype(f32).reshape(B, H, n, C, dk)
    vc = v.astype(f32).reshape(B, H, n, C, dv)
    b = jnp.cumsum(log_a.reshape(B, H, n, C, dk), axis=3)
    b_last = b[:, :, :, -1:, :]
    q_dec = qc * jnp.exp(b)
    k_inv = kc * jnp.exp(-b)
    k_end = kc * jnp.exp(b_last - b)
    causal = jnp.tril(jnp.ones((C, C), dtype=bool))
    attn = jnp.where(causal, jnp.einsum('bhncd,bhnsd->bhncs', q_dec, k_inv), 0.0)
    o_intra = jnp.einsum('bhncs,bhnse->bhnce', attn, vc)
    kv = jnp.einsum('bhncd,bhnce->bhnde', k_end, vc)
    decay = jnp.exp(b_last[:, :, :, 0, :])

    def step(state, inp):
        kv_n, dec_n = inp
        return dec_n[..., None] * state + kv_n, state

    _, s_prev = lax.scan(step, jnp.zeros((B, H, dk, dv), f32),
                         (kv.transpose(2, 0, 1, 3, 4), decay.transpose(2, 0, 1, 3)))
    s_prev = s_prev.transpose(1, 2, 0, 3, 4)
    o_inter = jnp.einsum('bhncd,bhnde->bhnce', q_dec, s_prev)
    return (o_intra + o_inter).reshape(B, H, S, dv).astype(v.dtype)


def hybrid_mixer(h, positions, w_in, fox_bias, gla_w_gate, gla_b_gate, gla_norm, w_branch, w_out):
    B, S, _ = h.shape
    points, acc = [], 0
    for w in IN_SPLITS[:-1]:
        acc += w
        points.append(acc)
    proj = h @ w_in
    sb_qkv, fox_qkv, fox_f, moba_qkv, gq, gk, gv, gr, glr, gates = jnp.split(proj, points, axis=-1)

    qa, ka, va = [split_heads(t, N_SB_HEADS) for t in jnp.split(sb_qkv, 3, axis=-1)]
    o_a = merge_heads(stick_breaking_attention(qa, ka, va))

    qb, kb, vb = [split_heads(t, N_FOX_HEADS) for t in jnp.split(fox_qkv, 3, axis=-1)]
    log_f = jax.nn.log_sigmoid((fox_f + fox_bias).astype(jnp.float32)).transpose(0, 2, 1)
    o_b = merge_heads(forgetting_attention(qb, kb, vb, log_f))

    qc, kc, vc = [split_heads(t, N_MOBA_HEADS) for t in jnp.split(moba_qkv, 3, axis=-1)]
    o_c = merge_heads(moba_attention(partial_rope(qc, positions), partial_rope(kc, positions), vc))

    log_a = jax.nn.log_sigmoid((glr @ gla_w_gate + gla_b_gate).astype(jnp.float32)) / GLA_GATE_NORMALIZER
    o_d = gla_chunked(split_heads(gq, N_GLA_HEADS), split_heads(gk, N_GLA_HEADS),
                      split_heads(gv, N_GLA_HEADS), split_heads(log_a, N_GLA_HEADS))
    o_d = rms_norm(o_d.transpose(0, 2, 1, 3), gla_norm).reshape(B, S, GLA_VAL_DIM) * jax.nn.silu(gr)

    branches = jnp.stack([o_a, o_b, o_c, o_d], axis=2)
    up = jnp.einsum('bsnw,nwd->bsnd', branches, w_branch)
    g = jax.nn.sigmoid(gates.reshape(B, S, N_BRANCHES, D_MODEL))
    return jnp.sum(g * up, axis=2) @ w_out


def route(h, w_router, router_bias):
    T = h.shape[0]
    scores = jax.nn.sigmoid((h @ w_router).astype(jnp.float32))
    biased = scores + router_bias.astype(jnp.float32)
    grp = biased.reshape(T, N_EXPERT_GROUPS, N_EXPERTS // N_EXPERT_GROUPS)
    grp_score = jnp.sum(lax.top_k(grp, 2)[0], axis=-1)
    _, top_groups = lax.top_k(grp_score, TOPK_GROUPS)
    group_mask = jnp.any(top_groups[:, :, None] == jnp.arange(N_EXPERT_GROUPS)[None, None, :], axis=1)
    expert_mask = jnp.repeat(group_mask, N_EXPERTS // N_EXPERT_GROUPS, axis=1)
    _, top_idx = lax.top_k(jnp.where(expert_mask, biased, -jnp.inf), TOP_K)
    top_w = jnp.take_along_axis(scores, top_idx, axis=-1)
    top_w = top_w / jnp.sum(top_w, axis=-1, keepdims=True) * ROUTED_SCALE
    return top_idx, top_w


def routed_experts(xf, top_idx, top_w, w_gate, w_up, w_down):
    T, D = xf.shape
    E = w_gate.shape[0]
    M = EXPERT_ROW_BLOCK
    A = T * TOP_K
    flat_e = top_idx.reshape(A)
    flat_tok = jnp.arange(A, dtype=jnp.int32) // TOP_K
    flat_w = top_w.reshape(A)
    order = jnp.argsort(flat_e)
    sorted_e = flat_e[order]
    counts = jnp.bincount(flat_e, length=E)
    padded = (counts + M - 1) // M * M
    pad_end = jnp.cumsum(padded)
    pad_start = pad_end - padded
    start = jnp.cumsum(counts) - counts
    dest = pad_start[sorted_e] + jnp.arange(A) - start[sorted_e]
    n_blocks = (A + E * (M - 1) + M - 1) // M
    P = n_blocks * M
    row_tok = jnp.zeros((P,), jnp.int32).at[dest].set(flat_tok[order])
    row_w = jnp.zeros((P,), jnp.float32).at[dest].set(flat_w[order])
    blk_e = jnp.minimum(jnp.searchsorted(pad_end, jnp.arange(n_blocks) * M, side='right'), E - 1)

    def block(args):
        tok, e = args
        xb = xf[tok]
        hb = jax.nn.silu(xb @ w_gate[e]) * (xb @ w_up[e])
        return hb @ w_down[e]

    y = lax.map(block, (row_tok.reshape(n_blocks, M), blk_e)).reshape(P, D)
    y = y * row_w[:, None].astype(y.dtype)
    return jnp.zeros((T, D), y.dtype).at[row_tok].add(y)


def setup_inputs(seed: int = 0) -> dict:
    key = jax.random.key(seed)
    ks = jax.random.split(key, 24)
    f32 = jnp.float32
    L, D, E, Hx = DEPTH, D_MODEL, N_EXPERTS, EXPERT_HIDDEN

    def nrm(k, shape, scale):
        return jax.random.normal(k, shape, f32) * scale

    positions = (jax.random.randint(ks[2], (BATCH, 1), 0, 1024, dtype=jnp.int32)
                 + jnp.arange(SEQ, dtype=jnp.int32)[None, :])
    return {
        "x": nrm(ks[0], (BATCH, SEQ, D), 1.0),
        "c": nrm(ks[1], (BATCH, D), 1.0),
        "positions": positions,
        "attn_norm": 1.0 + nrm(ks[3], (L, D), 0.02),
        "w_ada": nrm(ks[4], (L, D, 6 * D), 0.5 * D ** -0.5),
        "b_ada": nrm(ks[5], (L, 6 * D), 0.02),
        "w_in": nrm(ks[6], (L, D, IN_WIDTH), D ** -0.5),
        "fox_bias": FORGET_BIAS_INIT + nrm(ks[7], (L, N_FOX_HEADS), 0.1),
        "gla_w_gate": nrm(ks[8], (L, GLA_GATE_RANK, GLA_KEY_DIM), GLA_GATE_RANK ** -0.5),
        "gla_b_gate": nrm(ks[9], (L, GLA_KEY_DIM), 0.1),
        "gla_norm": 1.0 + nrm(ks[10], (L, GLA_HEAD_V), 0.02),
        "w_branch": nrm(ks[11], (L, N_BRANCHES, BRANCH_WIDTH, D), BRANCH_WIDTH ** -0.5),
        "w_out": nrm(ks[12], (L, D, D), D ** -0.5),
        "ffn_norm": 1.0 + nrm(ks[13], (L, D), 0.02),
        "w_router": nrm(ks[14], (L, D, E), D ** -0.5),
        "router_bias": nrm(ks[15], (L, E), 0.01),
        "w_exp_gate": nrm(ks[16], (L, E, D, Hx), D ** -0.5),
        "w_exp_up": nrm(ks[17], (L, E, D, Hx), D ** -0.5),
        "w_exp_down": nrm(ks[18], (L, E, Hx, D), Hx ** -0.5),
        "w_sh_gate": nrm(ks[19], (L, D, SHARED_HIDDEN), D ** -0.5),
        "w_sh_up": nrm(ks[20], (L, D, SHARED_HIDDEN), D ** -0.5),
        "w_sh_down": nrm(ks[21], (L, SHARED_HIDDEN, D), SHARED_HIDDEN ** -0.5),
        "final_norm": 1.0 + nrm(ks[22], (D,), 0.02),
    }


def reference(x, c, positions, attn_norm, w_ada, b_ada, w_in, fox_bias, gla_w_gate, gla_b_gate,
              gla_norm, w_branch, w_out, ffn_norm, w_router, router_bias, w_exp_gate, w_exp_up,
              w_exp_down, w_sh_gate, w_sh_up, w_sh_down, final_norm):
    B, S, D = x.shape
    c_act = jax.nn.silu(c)
    for l in range(DEPTH):
        mod = (c_act @ w_ada[l] + b_ada[l])[:, None, :]
        shift1, scale1, gate1, shift2, scale2, gate2 = jnp.split(mod, 6, axis=-1)
        h = rms_norm(x, attn_norm[l]) * (1.0 + scale1) + shift1
        x = x + gate1 * hybrid_mixer(h, positions, w_in[l], fox_bias[l], gla_w_gate[l],
                                     gla_b_gate[l], gla_norm[l], w_branch[l], w_out[l])
        h = rms_norm(x, ffn_norm[l]) * (1.0 + scale2) + shift2
        hf = h.reshape(B * S, D)
        top_idx, top_w = route(hf, w_router[l], router_bias[l])
        shared = (jax.nn.silu(hf @ w_sh_gate[l]) * (hf @ w_sh_up[l])) @ w_sh_down[l]
        y = routed_experts(hf, top_idx, top_w, w_exp_gate[l], w_exp_up[l], w_exp_down[l]) + shared
        x = x + gate2 * y.reshape(B, S, D)
    return rms_norm(x, final_norm)
```

```python
import functools

import jax
import jax.numpy as jnp
from jax import lax
from jax.experimental import pallas as pl
from jax.experimental.pallas import tpu as pltpu

F32 = jnp.float32
BF16 = jnp.bfloat16
I32 = jnp.int32
U32 = jnp.uint32

HEAD_DIM = 128
N_HEADS = 4
BRANCH_WIDTH = N_HEADS * HEAD_DIM
GLA_HEAD_K = 64
GLA_CHUNK = 64
GLA_GATE_RANK = 16
GLA_GATE_NORMALIZER = 16.0
MOBA_BLOCK = 256
MOBA_TOPK = 3
ROPE_THETA = 500000.0
ROPE_DIM = HEAD_DIM // 4
N_EXPERTS = 64
GROUP_SIZE = 8
TOPK_GROUPS = 4
TOP_K = 8
ROUTED_SCALE = 2.5
RMS_EPS = 1e-6

LANES = 128
SUBLANES = 8
MIB = 1024 * 1024
NEG_BIG = -1e30

ATT_BLOCK = 256
EXPERT_ROWS = 256
TOKEN_TILE = 128

COL_SB, COL_FOX, COL_MOBA = 0, 12, 24
COL_GQ, COL_GK, COL_GV, COL_GR, COL_GATES = 36, 38, 40, 44, 48
N_MAIN_COLS = 112 * LANES
MISC_FOX_F, MISC_GLR = 0, 4


def _cparams(n_axes, vmem_mib):
    return pltpu.CompilerParams(dimension_semantics=("arbitrary",) * n_axes,
                                vmem_limit_bytes=vmem_mib * MIB)


def _softplus(z):
    return jnp.maximum(z, 0.0) + jnp.log(1.0 + jnp.exp(-jnp.abs(z)))


def _log_sigmoid(z):
    return -_softplus(-z)


def _sigmoid(z):
    return 1.0 / (1.0 + jnp.exp(-z))


def _dot(a, b, precision=None):
    return jnp.dot(a, b, preferred_element_type=F32, precision=precision)


def _dot_nt(a, b, precision=None):
    return lax.dot_general(a, b, (((1,), (1,)), ((), ())), preferred_element_type=F32,
                           precision=precision)


def _dot_tn(a, b, precision=None):
    return lax.dot_general(a, b, (((0,), (0,)), ((), ())), preferred_element_type=F32,
                           precision=precision)


HIGHEST = lax.Precision.HIGHEST


def _split3(x):
    hi = x.astype(BF16)
    r1 = x - hi.astype(F32)
    mid = r1.astype(BF16)
    lo = (r1 - mid.astype(F32)).astype(BF16)
    return hi, mid, lo


def _ada_kernel(c_ref, w_ref, b_ref, o_ref):
    c = c_ref[...]
    c_act = (c * _sigmoid(c)).astype(BF16)
    o_ref[...] = _dot(c_act, w_ref[...].astype(BF16)) + b_ref[...]


def _ada_mod(c_pad, w_ada, b_ada):
    L, D, N = w_ada.shape
    tn = 1536
    return pl.pallas_call(
        _ada_kernel,
        out_shape=jax.ShapeDtypeStruct((L, SUBLANES, N), F32),
        grid=(L, N // tn),
        in_specs=[pl.BlockSpec((SUBLANES, D), lambda l, j: (0, 0)),
                  pl.BlockSpec((None, D, tn), lambda l, j: (l, 0, j)),
                  pl.BlockSpec((None, 1, tn), lambda l, j: (l, 0, j))],
        out_specs=pl.BlockSpec((None, SUBLANES, tn), lambda l, j: (l, 0, j)),
        compiler_params=_cparams(2, 40),
        name="ada_mod",
    )(c_pad, w_ada, b_ada.reshape(L, 1, N))


def _normed(x, g, scale, shift):
    y = x * lax.rsqrt(jnp.mean(x * x, axis=-1, keepdims=True) + RMS_EPS)
    return (y * g) * (1.0 + scale) + shift


def _norm_mod_kernel(x_ref, g_ref, sc_ref, sh_ref, h_ref):
    h_ref[...] = _normed(x_ref[...], g_ref[...], sc_ref[...], sh_ref[...]).astype(BF16)


def _pack_pairs(h):
    half = h.shape[1] // 2
    hi = lax.bitcast_convert_type(h[:, :half].astype(BF16).astype(F32), U32)
    lo = lax.bitcast_convert_type(h[:, half:].astype(BF16).astype(F32), U32)
    return hi | (lo >> 16)


def _unpack_pairs(w):
    a = lax.bitcast_convert_type(w & jnp.uint32(0xFFFF0000), F32).astype(BF16)
    b = lax.bitcast_convert_type(w << 16, F32).astype(BF16)
    return jnp.concatenate([a, b], axis=1)


def _norm_route_kernel(x_ref, g_ref, sc_ref, sh_ref, wr_ref, hp_ref, lg_ref):
    h = _normed(x_ref[...], g_ref[...], sc_ref[...], sh_ref[...])
    hp_ref[...] = _pack_pairs(h)
    lg_ref[...] = _dot(h, wr_ref[...], precision=HIGHEST)


def _norm_mod(x, g, scale, shift, S, w_router_pad=None):
    T, D = x.shape
    tm = 512
    nb = S // tm
    row = lambda i: (i, 0)
    per_batch = pl.BlockSpec((None, 1, D), lambda i: (i // nb, 0, 0))
    in_specs = [pl.BlockSpec((tm, D), row), pl.BlockSpec((1, D), lambda i: (0, 0)), per_batch, per_batch]
    if w_router_pad is None:
        return pl.pallas_call(
            _norm_mod_kernel, out_shape=jax.ShapeDtypeStruct((T, D), BF16), grid=(T // tm,),
            in_specs=in_specs, out_specs=pl.BlockSpec((tm, D), row),
            compiler_params=_cparams(1, 32), name="norm_mod",
        )(x, g, scale, shift)
    return pl.pallas_call(
        _norm_route_kernel,
        out_shape=(jax.ShapeDtypeStruct((T, D // 2), U32), jax.ShapeDtypeStruct((T, LANES), F32)),
        grid=(T // tm,),
        in_specs=in_specs + [pl.BlockSpec((D, LANES), lambda i: (0, 0))],
        out_specs=(pl.BlockSpec((tm, D // 2), row), pl.BlockSpec((tm, LANES), row)),
        compiler_params=_cparams(1, 32), name="norm_route",
    )(x, g, scale, shift, w_router_pad)


def _mm_kernel(*refs, cast_w, residual):
    if residual:
        a_ref, w_ref, x_ref, g_ref, o_ref = refs[:5]
        scratch = refs[5:]
    else:
        a_ref, w_ref, o_ref = refs[:3]
        scratch = refs[3:]
    if cast_w:
        wbf_ref, = scratch

        @pl.when(pl.program_id(1) == 0)
        def _():
            wbf_ref[...] = w_ref[...].astype(BF16)

        w = wbf_ref[...]
    else:
        w = w_ref[...]
    acc = _dot(a_ref[...], w)
    if residual:
        acc = x_ref[...] + g_ref[...] * acc
    o_ref[...] = acc.astype(o_ref.dtype)


def _matmul(a, w, tm, tn, out_dtype=F32, residual=None, S=None, vmem_mib=48):
    M, K = a.shape
    N = w.shape[1]
    cast_w = w.dtype != BF16
    in_specs = [pl.BlockSpec((tm, K), lambda j, i: (i, 0)), pl.BlockSpec((K, tn), lambda j, i: (0, j))]
    args = [a, w]
    if residual is not None:
        x, gate = residual
        nb = S // tm
        in_specs += [pl.BlockSpec((tm, tn), lambda j, i: (i, j)),
                     pl.BlockSpec((None, 1, tn), lambda j, i: (i // nb, 0, j))]
        args += [x, gate]
    return pl.pallas_call(
        functools.partial(_mm_kernel, cast_w=cast_w, residual=residual is not None),
        out_shape=jax.ShapeDtypeStruct((M, N), out_dtype),
        grid=(N // tn, M // tm),
        in_specs=in_specs,
        out_specs=pl.BlockSpec((tm, tn), lambda j, i: (i, j)),
        scratch_shapes=[pltpu.VMEM((K, tn), BF16)] if cast_w else [],
        compiler_params=_cparams(2, vmem_mib),
        name="matmul_res" if residual is not None else "matmul",
    )(*args)


def _sb_kernel(q_ref, k_ref, v_ref, o_ref, *, blk, scale):
    i = pl.program_id(2)
    q = (q_ref[...] * scale).astype(BF16)
    row = lax.broadcasted_iota(I32, (blk, blk), 0)
    col = lax.broadcasted_iota(I32, (blk, blk), 1)
    later = (row > col).astype(BF16)

    def step(jj, carry):
        c, acc = carry
        j = i - jj
        start = pl.multiple_of(j * blk, blk)
        k = k_ref[pl.ds(start, blk), :].astype(BF16)
        v = v_ref[pl.ds(start, blk), :].astype(BF16)
        z = _dot_nt(q, k)
        past = jnp.logical_or(j < i, col < row)
        ls = jnp.where(past, -_softplus(z), 0.0)
        hi = ls.astype(BF16)
        lo = (ls - hi.astype(F32)).astype(BF16)
        between = _dot(hi, later) + _dot(lo, later)
        w = jnp.where(past, jnp.exp(z + ls + between + c), 0.0)
        acc = acc + _dot(w.astype(BF16), v)
        c = c + jnp.sum(ls, axis=1, keepdims=True)
        return c, acc

    _, acc = lax.fori_loop(0, i + 1, step,
                           (jnp.zeros((blk, 1), F32), jnp.zeros((blk, HEAD_DIM), F32)))
    o_ref[...] = acc


def _sb_attention(proj, B, S, col0):
    T = proj.shape[0]
    blk = ATT_BLOCK
    nq = S // blk
    return pl.pallas_call(
        functools.partial(_sb_kernel, blk=blk, scale=HEAD_DIM ** -0.5),
        out_shape=jax.ShapeDtypeStruct((T, BRANCH_WIDTH), F32),
        grid=(B, N_HEADS, nq),
        in_specs=[pl.BlockSpec((blk, HEAD_DIM), lambda b, h, i: (b * nq + i, col0 + h)),
                  pl.BlockSpec((S, HEAD_DIM), lambda b, h, i: (b, col0 + N_HEADS + h)),
                  pl.BlockSpec((S, HEAD_DIM), lambda b, h, i: (b, col0 + 2 * N_HEADS + h))],
        out_specs=pl.BlockSpec((blk, HEAD_DIM), lambda b, h, i: (b * nq + i, h)),
        compiler_params=_cparams(3, 32),
        name="sb_attention",
    )(proj, proj, proj)


def _flash_kernel(qa_ref, ka_ref, v_ref, o_ref, *, blk):
    i = pl.program_id(2)
    qa = qa_ref[...]
    row = lax.broadcasted_iota(I32, (blk, blk), 0)
    col = lax.broadcasted_iota(I32, (blk, blk), 1)

    def step(j, carry):
        m, l, acc = carry
        start = pl.multiple_of(j * blk, blk)
        s = _dot_nt(qa, ka_ref[pl.ds(start, blk), :])
        s = jnp.where(jnp.logical_or(j < i, col <= row), s, NEG_BIG)
        m_new = jnp.maximum(m, jnp.max(s, axis=1, keepdims=True))
        alpha = jnp.exp(m - m_new)
        p = jnp.exp(s - m_new)
        l = alpha * l + jnp.sum(p, axis=1, keepdims=True)
        acc = alpha * acc + _dot(p.astype(BF16), v_ref[pl.ds(start, blk), :].astype(BF16))
        return m_new, l, acc

    init = (jnp.full((blk, 1), NEG_BIG, F32), jnp.zeros((blk, 1), F32), jnp.zeros((blk, HEAD_DIM), F32))
    _, l, acc = lax.fori_loop(0, i + 1, step, init)
    o_ref[...] = acc / l


def _flash_attention(qa, ka, proj, B, S, vcol0):
    T = qa.shape[0]
    blk = ATT_BLOCK
    nq = S // blk
    return pl.pallas_call(
        functools.partial(_flash_kernel, blk=blk),
        out_shape=jax.ShapeDtypeStruct((T, BRANCH_WIDTH), F32),
        grid=(B, N_HEADS, nq),
        in_specs=[pl.BlockSpec((blk, 2 * HEAD_DIM), lambda b, h, i: (b * nq + i, h)),
                  pl.BlockSpec((S, 2 * HEAD_DIM), lambda b, h, i: (b, h)),
                  pl.BlockSpec((S, HEAD_DIM), lambda b, h, i: (b, vcol0 + h))],
        out_specs=pl.BlockSpec((blk, HEAD_DIM), lambda b, h, i: (b * nq + i, h)),
        compiler_params=_cparams(3, 32),
        name="flash_attention",
    )(qa, ka, proj)


def _fox_prep_kernel(q_ref, k_ref, misc_ref, bias_ref, qa_ref, ka_ref, carry_ref, *, tb, scale):
    @pl.when(pl.program_id(1) == 0)
    def _():
        carry_ref[...] = jnp.zeros_like(carry_ref)

    lane = lax.broadcasted_iota(I32, (tb, LANES), 1)
    lf = jnp.where(lane < N_HEADS, _log_sigmoid(misc_ref[...] + bias_ref[...]), 0.0)
    row = lax.broadcasted_iota(I32, (tb, tb), 0)
    col = lax.broadcasted_iota(I32, (tb, tb), 1)
    incl = (col <= row).astype(F32)
    F = _dot(incl, lf, precision=HIGHEST) + carry_ref[0:1, :]
    carry_ref[0:1, :] = F[tb - 1:tb, :]
    for h in range(N_HEADS):
        Fh = jnp.broadcast_to(F[:, h:h + 1], (tb, LANES))
        hi, mid, lo = (p.astype(F32) for p in _split3(Fh))
        ones = jnp.ones((tb, LANES), F32)
        zeros = jnp.zeros((tb, LANES), F32)
        eq = jnp.where(lane == 0, hi, jnp.where(lane == 1, mid, jnp.where(lane == 2, lo,
             jnp.where(lane < 6, ones, zeros))))
        ek = jnp.where(lane < 3, ones, jnp.where(lane == 3, -hi, jnp.where(lane == 4, -mid,
             jnp.where(lane == 5, -lo, zeros))))
        hs = slice(h * HEAD_DIM, (h + 1) * HEAD_DIM)
        qa_ref[:, 2 * h * HEAD_DIM:(2 * h + 1) * HEAD_DIM] = (q_ref[:, hs] * scale).astype(BF16)
        qa_ref[:, (2 * h + 1) * HEAD_DIM:(2 * h + 2) * HEAD_DIM] = eq.astype(BF16)
        ka_ref[:, 2 * h * HEAD_DIM:(2 * h + 1) * HEAD_DIM] = k_ref[:, hs].astype(BF16)
        ka_ref[:, (2 * h + 1) * HEAD_DIM:(2 * h + 2) * HEAD_DIM] = ek.astype(BF16)


def _fox_prep(proj, misc, fox_bias_pad, B, S):
    T = proj.shape[0]
    tb = 512
    nb = S // tb
    qblk = COL_FOX * LANES // BRANCH_WIDTH
    out = jax.ShapeDtypeStruct((T, 2 * BRANCH_WIDTH), BF16)
    return pl.pallas_call(
        functools.partial(_fox_prep_kernel, tb=tb, scale=HEAD_DIM ** -0.5),
        out_shape=(out, out),
        grid=(B, nb),
        in_specs=[pl.BlockSpec((tb, BRANCH_WIDTH), lambda b, i: (b * nb + i, qblk)),
                  pl.BlockSpec((tb, BRANCH_WIDTH), lambda b, i: (b * nb + i, qblk + 1)),
                  pl.BlockSpec((tb, LANES), lambda b, i: (b * nb + i, 0)),
                  pl.BlockSpec((1, LANES), lambda b, i: (0, 0))],
        out_specs=(pl.BlockSpec((tb, 2 * BRANCH_WIDTH), lambda b, i: (b * nb + i, 0)),
                   pl.BlockSpec((tb, 2 * BRANCH_WIDTH), lambda b, i: (b * nb + i, 0))),
        scratch_shapes=[pltpu.VMEM((SUBLANES, LANES), F32)],
        compiler_params=_cparams(2, 32),
        name="fox_prep",
    )(proj, proj, misc, fox_bias_pad)


def _rope(x, cos, sin, lane):
    half = ROPE_DIM // 2
    up = pltpu.roll(x, half, 1)
    down = pltpu.roll(x, LANES - half, 1)
    rot = jnp.where(lane < half, -down * sin, jnp.where(lane < ROPE_DIM, up * sin, 0.0))
    return x * jnp.where(lane < ROPE_DIM, cos, 1.0) + rot


def _angles(pos_ref, freq_ref):
    ang = pos_ref[...].astype(F32) * freq_ref[...]
    return jnp.cos(ang), jnp.sin(ang)


def _moba_k_kernel(k_ref, pos_ref, freq_ref, ka_ref, kmean_ref, *, tb):
    i = pl.program_id(1)

    @pl.when(i == 0)
    def _():
        kmean_ref[...] = jnp.zeros_like(kmean_ref)

    lane = lax.broadcasted_iota(I32, (tb, LANES), 1)
    cos, sin = _angles(pos_ref, freq_ref)
    onehot = jnp.where(lane == i, 1.0, 0.0).astype(BF16)
    this_row = lax.broadcasted_iota(I32, (LANES, HEAD_DIM), 0) == i
    for h in range(N_HEADS):
        hs = slice(h * HEAD_DIM, (h + 1) * HEAD_DIM)
        kr = _rope(k_ref[:, hs], cos, sin, lane)
        mean = jnp.sum(kr, axis=0, keepdims=True) * (1.0 / tb)
        kmean_ref[:, hs] = jnp.where(this_row, mean, kmean_ref[:, hs])
        ka_ref[:, 2 * h * HEAD_DIM:(2 * h + 1) * HEAD_DIM] = kr.astype(BF16)
        ka_ref[:, (2 * h + 1) * HEAD_DIM:(2 * h + 2) * HEAD_DIM] = onehot


def _moba_q_kernel(q_ref, pos_ref, freq_ref, kmean_ref, qa_ref, *, tb, scale):
    i = pl.program_id(1)
    lane = lax.broadcasted_iota(I32, (tb, LANES), 1)
    cos, sin = _angles(pos_ref, freq_ref)
    neg_inf = jnp.float32(-jnp.inf)
    for h in range(N_HEADS):
        hs = slice(h * HEAD_DIM, (h + 1) * HEAD_DIM)
        qr = _rope(q_ref[:, hs], cos, sin, lane)
        gate = _dot_nt(qr, kmean_ref[:, hs], precision=HIGHEST)
        cur = jnp.where(lane < i, gate, neg_inf)
        chosen = lane == i
        for _ in range(MOBA_TOPK):
            m = jnp.max(cur, axis=1, keepdims=True)
            first = jnp.min(jnp.where(jnp.logical_and(cur == m, m > neg_inf), lane, LANES),
                            axis=1, keepdims=True)
            pick = lane == first
            chosen = jnp.logical_or(chosen, pick)
            cur = jnp.where(pick, neg_inf, cur)
        bias = jnp.where(jnp.logical_or(chosen, lane >= LANES // 2), 0.0, NEG_BIG)
        qa_ref[:, 2 * h * HEAD_DIM:(2 * h + 1) * HEAD_DIM] = (qr * scale).astype(BF16)
        qa_ref[:, (2 * h + 1) * HEAD_DIM:(2 * h + 2) * HEAD_DIM] = bias.astype(BF16)


def _moba_prep(proj, pos_col, freq_lanes, B, S):
    T = proj.shape[0]
    tb = MOBA_BLOCK
    nb = S // tb
    qblk = COL_MOBA * LANES // BRANCH_WIDTH
    aug = jax.ShapeDtypeStruct((T, 2 * BRANCH_WIDTH), BF16)
    row_spec = lambda c: pl.BlockSpec((tb, BRANCH_WIDTH), lambda b, i: (b * nb + i, c))
    pos_spec = pl.BlockSpec((tb, 1), lambda b, i: (b * nb + i, 0))
    freq_spec = pl.BlockSpec((1, LANES), lambda b, i: (0, 0))
    aug_spec = pl.BlockSpec((tb, 2 * BRANCH_WIDTH), lambda b, i: (b * nb + i, 0))
    kmean_spec = pl.BlockSpec((None, LANES, BRANCH_WIDTH), lambda b, i: (b, 0, 0))
    ka, kmean = pl.pallas_call(
        functools.partial(_moba_k_kernel, tb=tb),
        out_shape=(aug, jax.ShapeDtypeStruct((B, LANES, BRANCH_WIDTH), F32)),
        grid=(B, nb),
        in_specs=[row_spec(qblk + 1), pos_spec, freq_spec],
        out_specs=(aug_spec, kmean_spec),
        compiler_params=_cparams(2, 32),
        name="moba_k_prep",
    )(proj, pos_col, freq_lanes)
    qa = pl.pallas_call(
        functools.partial(_moba_q_kernel, tb=tb, scale=HEAD_DIM ** -0.5),
        out_shape=aug,
        grid=(B, nb),
        in_specs=[row_spec(qblk), pos_spec, freq_spec, kmean_spec],
        out_specs=aug_spec,
        compiler_params=_cparams(2, 32),
        name="moba_q_prep",
    )(proj, pos_col, freq_lanes, kmean)
    return qa, ka


def _gla_kernel(q_ref, k_ref, v_ref, gr_ref, misc_ref, wg_ref, bg_ref, gn_ref, o_ref, state_ref, *, tb):
    C = GLA_CHUNK
    @pl.when(pl.program_id(1) == 0)
    def _():
        state_ref[...] = jnp.zeros_like(state_ref)

    la_all = _log_sigmoid(_dot(misc_ref[...], wg_ref[...], precision=HIGHEST) + bg_ref[...]) \
        * (1.0 / GLA_GATE_NORMALIZER)
    rowc = lax.broadcasted_iota(I32, (C, C), 0)
    colc = lax.broadcasted_iota(I32, (C, C), 1)
    incl = (colc <= rowc).astype(F32)
    causal = colc <= rowc
    lane = lax.broadcasted_iota(I32, (C, LANES), 1)
    row2 = lax.broadcasted_iota(I32, (2 * GLA_HEAD_K, LANES), 0)
    ones_cv = jnp.ones((C, LANES), F32)
    qscale = GLA_HEAD_K ** -0.5
    gn = gn_ref[...]
    for c in range(tb // C):
        rs = slice(c * C, (c + 1) * C)
        for p in range(2):
            ps = slice(p * LANES, (p + 1) * LANES)
            la = la_all[rs, ps]
            b = _dot(incl, la, precision=HIGHEST)
            b_last = b[C - 1:C, :]
            eb = jnp.exp(b)
            q_dec = q_ref[rs, ps] * qscale * eb
            k = k_ref[rs, ps]
            k_inv = (k * jnp.exp(-b)).astype(BF16)
            k_end = (k * jnp.exp(b_last - b)).astype(BF16)
            decay = jnp.exp(_dot_tn(la, ones_cv, precision=HIGHEST))
            st = state_ref[p * LANES:(p + 1) * LANES, :]
            new_rows = []
            for e in range(2):
                h = 2 * p + e
                own = jnp.logical_and(lane >= e * GLA_HEAD_K, lane < (e + 1) * GLA_HEAD_K)
                qh = jnp.where(own, q_dec, 0.0).astype(BF16)
                v = v_ref[rs, h * HEAD_DIM:(h + 1) * HEAD_DIM].astype(BF16)
                attn = jnp.where(causal, _dot_nt(qh, k_inv), 0.0)
                st_h = jnp.where(jnp.logical_and(row2 >= e * GLA_HEAD_K, row2 < (e + 1) * GLA_HEAD_K), st, 0.0)
                o = _dot(attn.astype(BF16), v) + _dot(qh, st_h.astype(BF16))
                y = o * lax.rsqrt(jnp.mean(o * o, axis=-1, keepdims=True) + RMS_EPS) * gn
                g = gr_ref[rs, h * HEAD_DIM:(h + 1) * HEAD_DIM]
                o_ref[rs, h * HEAD_DIM:(h + 1) * HEAD_DIM] = y * (g * _sigmoid(g))
                new_rows.append(_dot_tn(k_end, v))
            kv = jnp.where(row2 < GLA_HEAD_K, new_rows[0], new_rows[1])
            state_ref[p * LANES:(p + 1) * LANES, :] = decay * st + kv


def _gla(proj, misc, w_gate_pad, b_gate, gla_norm, B, S):
    T = proj.shape[0]
    tb = 512
    nb = S // tb
    kd = 2 * LANES
    rows = lambda w, c: pl.BlockSpec((tb, w), lambda b, i: (b * nb + i, c))
    const = lambda shape: pl.BlockSpec(shape, lambda b, i: (0, 0))
    return pl.pallas_call(
        functools.partial(_gla_kernel, tb=tb),
        out_shape=jax.ShapeDtypeStruct((T, BRANCH_WIDTH), F32),
        grid=(B, nb),
        in_specs=[rows(kd, COL_GQ * LANES // kd), rows(kd, COL_GK * LANES // kd),
                  rows(BRANCH_WIDTH, COL_GV * LANES // BRANCH_WIDTH),
                  rows(BRANCH_WIDTH, COL_GR * LANES // BRANCH_WIDTH),
                  rows(LANES, 0), const((LANES, kd)), const((1, kd)), const((1, HEAD_DIM))],
        out_specs=rows(BRANCH_WIDTH, 0),
        scratch_shapes=[pltpu.VMEM((kd, HEAD_DIM), F32)],
        compiler_params=_cparams(2, 32),
        name="gla",
    )(proj, proj, proj, proj, misc, w_gate_pad, b_gate, gla_norm)


def _merge_kernel(oa_ref, ob_ref, oc_ref, od_ref, g0_ref, g1_ref, g2_ref, g3_ref, w_ref, o_ref, wbf_ref):
    @pl.when(pl.program_id(1) == 0)
    def _():
        wbf_ref[...] = w_ref[...].astype(BF16)

    acc = None
    for n, (b_ref, g_ref) in enumerate(((oa_ref, g0_ref), (ob_ref, g1_ref), (oc_ref, g2_ref), (od_ref, g3_ref))):
        term = _sigmoid(g_ref[...]) * _dot(b_ref[...].astype(BF16), wbf_ref[n])
        acc = term if acc is None else acc + term
    o_ref[...] = acc.astype(BF16)


def _merge(branches, proj, w_branch):
    T = proj.shape[0]
    nbr, W, D = w_branch.shape
    tm, tn = 512, 512
    g0 = COL_GATES * LANES // tn
    per = D // tn
    br_spec = pl.BlockSpec((tm, W), lambda j, i: (i, 0))
    gate_spec = lambda n: pl.BlockSpec((tm, tn), lambda j, i: (i, g0 + n * per + j))
    return pl.pallas_call(
        _merge_kernel,
        out_shape=jax.ShapeDtypeStruct((T, D), BF16),
        grid=(D // tn, T // tm),
        in_specs=[br_spec] * 4 + [gate_spec(n) for n in range(4)]
                 + [pl.BlockSpec((nbr, W, tn), lambda j, i: (0, 0, j))],
        out_specs=pl.BlockSpec((tm, tn), lambda j, i: (i, j)),
        scratch_shapes=[pltpu.VMEM((nbr, W, tn), BF16)],
        compiler_params=_cparams(2, 40),
        name="branch_merge",
    )(*branches, proj, proj, proj, proj, w_branch)


def _butterfly(x, lane, op):
    for s in (1, 2, 4):
        up = pltpu.roll(x, s, 1)
        down = pltpu.roll(x, LANES - s, 1)
        x = op(x, jnp.where((lane & s) != 0, up, down))
    return x


def _route_kernel(lg_ref, bias_ref, e8_ref, w8_ref, p8_ref, cnt_ref, carry_ref, *, tm):
    @pl.when(pl.program_id(0) == 0)
    def _():
        carry_ref[...] = jnp.zeros_like(carry_ref)

    neg_inf = jnp.float32(-jnp.inf)
    lane = lax.broadcasted_iota(I32, (tm, LANES), 1)
    valid = lane < N_EXPERTS
    scores = _sigmoid(lg_ref[...])
    biased = jnp.where(valid, scores + bias_ref[...], neg_inf)
    g1 = _butterfly(biased, lane, jnp.maximum)
    first = _butterfly(jnp.where(biased == g1, lane, LANES), lane, jnp.minimum)
    g2 = _butterfly(jnp.where(lane == first, neg_inf, biased), lane, jnp.maximum)
    gs = g1 + g2
    gs = jnp.where(valid, gs, pltpu.roll(gs, N_EXPERTS, 1))
    gidx = lane >> 3
    beaten = jnp.zeros((tm, LANES), I32)
    for r in range(1, N_EXPERTS // GROUP_SIZE):
        other = pltpu.roll(gs, GROUP_SIZE * r, 1)
        og = (gidx - r) & (N_EXPERTS // GROUP_SIZE - 1)
        wins = jnp.logical_or(other > gs, jnp.logical_and(other == gs, og < gidx))
        beaten = beaten + wins.astype(I32)
    cur = jnp.where(jnp.logical_and(beaten < TOPK_GROUPS, valid), biased, neg_inf)
    sel = jnp.zeros((tm, LANES), jnp.bool_)
    for _ in range(TOP_K):
        m = jnp.max(cur, axis=1, keepdims=True)
        pick = lane == jnp.min(jnp.where(cur == m, lane, LANES), axis=1, keepdims=True)
        sel = jnp.logical_or(sel, pick)
        cur = jnp.where(pick, neg_inf, cur)
    wsel = jnp.where(sel, scores, 0.0)
    wd = wsel / jnp.sum(wsel, axis=1, keepdims=True) * ROUTED_SCALE
    selb = jnp.where(sel, 1.0, 0.0).astype(BF16)
    row = lax.broadcasted_iota(I32, (tm, tm), 0)
    col = lax.broadcasted_iota(I32, (tm, tm), 1)
    pos = _dot((col < row).astype(BF16), selb) + carry_ref[0:1, :]
    total = carry_ref[0:1, :] + jnp.sum(selb.astype(F32), axis=0, keepdims=True)
    carry_ref[0:1, :] = total
    cnt_ref[...] = jnp.broadcast_to(total, cnt_ref.shape)
    r2 = lax.broadcasted_iota(I32, (LANES, LANES), 0)
    c2 = lax.broadcasted_iota(I32, (LANES, LANES), 1)
    slot = _dot(selb, (r2 < c2).astype(BF16))
    lane_f = lane.astype(F32)
    e8 = jnp.zeros((tm, LANES), F32)
    w8 = jnp.zeros((tm, LANES), F32)
    p8 = jnp.zeros((tm, LANES), F32)
    for k in range(TOP_K):
        mk = jnp.logical_and(sel, slot == k)
        put = lane == k
        e8 = jnp.where(put, jnp.sum(jnp.where(mk, lane_f, 0.0), axis=1, keepdims=True), e8)
        w8 = jnp.where(put, jnp.sum(jnp.where(mk, wd, 0.0), axis=1, keepdims=True), w8)
        p8 = jnp.where(put, jnp.sum(jnp.where(mk, pos, 0.0), axis=1, keepdims=True), p8)
    e8_ref[...] = e8.astype(I32)
    w8_ref[...] = w8
    p8_ref[...] = p8.astype(I32)


def _route(logits, router_bias_pad):
    T = logits.shape[0]
    tm = 512
    row = pl.BlockSpec((tm, LANES), lambda i: (i, 0))
    return pl.pallas_call(
        functools.partial(_route_kernel, tm=tm),
        out_shape=(jax.ShapeDtypeStruct((T, LANES), I32), jax.ShapeDtypeStruct((T, LANES), F32),
                   jax.ShapeDtypeStruct((T, LANES), I32), jax.ShapeDtypeStruct((SUBLANES, LANES), F32)),
        grid=(T // tm,),
        in_specs=[row, pl.BlockSpec((1, LANES), lambda i: (0, 0))],
        out_specs=(row, row, row, pl.BlockSpec((SUBLANES, LANES), lambda i: (0, 0))),
        scratch_shapes=[pltpu.VMEM((SUBLANES, LANES), F32)],
        compiler_params=_cparams(1, 32),
        name="route",
    )(logits, router_bias_pad)


def _row_copy(src, src_row, dst, dst_row, sem):
    return pltpu.make_async_copy(src.at[pl.ds(src_row, 1), :], dst.at[pl.ds(dst_row, 1), :], sem)


def _dispatch_kernel(pad_end_ref, padded_ref, dest_ref, hp_ref, xs_ref, zero_ref, sem_ref, *, tm, rows, n_blocks):
    @pl.when(pl.program_id(0) == 0)
    def _():
        zero_ref[...] = jnp.zeros_like(zero_ref)

        def fill(e, do_wait):
            @pl.when(padded_ref[e] > 0)
            def _():
                start = pl.multiple_of(pad_end_ref[e] - rows, rows)
                cp = pltpu.make_async_copy(zero_ref, xs_ref.at[pl.ds(start, rows), :], sem_ref)
                if do_wait:
                    cp.wait()
                else:
                    cp.start()

        lax.fori_loop(0, N_EXPERTS, lambda e, c: (fill(e, False), c)[1], 0)
        lax.fori_loop(0, N_EXPERTS, lambda e, c: (fill(e, True), c)[1], 0)

        def tail(b):
            return pltpu.make_async_copy(zero_ref, xs_ref.at[pl.ds(pl.multiple_of(b * rows, rows), rows), :], sem_ref)

        n_used = pad_end_ref[N_EXPERTS - 1] // rows
        lax.fori_loop(n_used, n_blocks, lambda b, c: (tail(b).start(), c)[1], 0)
        lax.fori_loop(n_used, n_blocks, lambda b, c: (tail(b).wait(), c)[1], 0)

    def issue(t, c):
        for k in range(TOP_K):
            _row_copy(hp_ref, t, xs_ref, dest_ref[t * TOP_K + k], sem_ref).start()
        return c

    def drain(t, c):
        for k in range(TOP_K):
            _row_copy(hp_ref, t, xs_ref, dest_ref[t * TOP_K + k], sem_ref).wait()
        return c

    lax.fori_loop(0, tm, issue, 0)
    lax.fori_loop(0, tm, drain, 0)


def _dispatch(hp, dest_flat, pad_end, padded, n_rows):
    T, W = hp.shape
    tm = TOKEN_TILE
    grid_spec = pltpu.PrefetchScalarGridSpec(
        num_scalar_prefetch=2,
        grid=(T // tm,),
        in_specs=[pl.BlockSpec((tm * TOP_K,), lambda i, pe, pd: (i,), memory_space=pltpu.SMEM),
                  pl.BlockSpec((tm, W), lambda i, pe, pd: (i, 0))],
        out_specs=pl.BlockSpec(memory_space=pl.ANY),
        scratch_shapes=[pltpu.VMEM((EXPERT_ROWS, W), U32), pltpu.SemaphoreType.DMA(())],
    )
    return pl.pallas_call(
        functools.partial(_dispatch_kernel, tm=tm, rows=EXPERT_ROWS, n_blocks=n_rows // EXPERT_ROWS),
        out_shape=jax.ShapeDtypeStruct((n_rows, W), U32),
        grid_spec=grid_spec,
        compiler_params=_cparams(1, 32),
        name="dispatch",
    )(pad_end, padded, dest_flat, hp)


def _expert_kernel(blk_e_ref, n_used_ref, x_ref, wg_ref, wu_ref, wd_ref, y_ref, wg_bf, wu_bf, wd_bf):
    i = pl.program_id(0)
    prev = blk_e_ref[jnp.maximum(i - 1, 0)]
    active = i < n_used_ref[0]

    @pl.when(jnp.logical_and(active, jnp.logical_or(i == 0, blk_e_ref[i] != prev)))
    def _():
        wg_bf[...] = wg_ref[...].astype(BF16)
        wu_bf[...] = wu_ref[...].astype(BF16)
        wd_bf[...] = wd_ref[...].astype(BF16)

    @pl.when(active)
    def _():
        x = _unpack_pairs(x_ref[...])
        g = _dot(x, wg_bf[...])
        u = _dot(x, wu_bf[...])
        hb = (g * _sigmoid(g)) * u
        y_ref[...] = _dot(hb.astype(BF16), wd_bf[...])

    @pl.when(jnp.logical_not(active))
    def _():
        y_ref[...] = jnp.zeros_like(y_ref)


def _experts(xs, blk_e, n_used, w_gate, w_up, w_down):
    P, W = xs.shape
    E, D, Hx = w_gate.shape
    M = EXPERT_ROWS
    nblk = P // M
    xrow = lambda i, be, nu: (jnp.minimum(i, nu[0] - 1), 0)
    wsel = lambda i, be, nu: (be[i], 0, 0)
    grid_spec = pltpu.PrefetchScalarGridSpec(
        num_scalar_prefetch=2,
        grid=(nblk,),
        in_specs=[pl.BlockSpec((M, W), xrow),
                  pl.BlockSpec((None, D, Hx), wsel), pl.BlockSpec((None, D, Hx), wsel),
                  pl.BlockSpec((None, Hx, D), wsel)],
        out_specs=pl.BlockSpec((M, D), lambda i, be, nu: (i, 0)),
        scratch_shapes=[pltpu.VMEM((D, Hx), BF16), pltpu.VMEM((D, Hx), BF16), pltpu.VMEM((Hx, D), BF16)],
    )
    return pl.pallas_call(
        _expert_kernel,
        out_shape=jax.ShapeDtypeStruct((P, D), F32),
        grid_spec=grid_spec,
        compiler_params=_cparams(1, 52),
        name="experts",
    )(blk_e, n_used, xs, w_gate, w_up, w_down)


def _combine_kernel(dest_ref, x_ref, ysh_ref, w8_ref, g2_ref, fn_ref, ys_ref, o_ref, buf_ref, sem_ref,
                    *, tm, final_norm):
    def issue(t, c):
        for k in range(TOP_K):
            _row_copy(ys_ref, dest_ref[t * TOP_K + k], buf_ref, k * tm + t, sem_ref).start()
        return c

    def drain(t, c):
        for k in range(TOP_K):
            _row_copy(ys_ref, dest_ref[t * TOP_K + k], buf_ref, k * tm + t, sem_ref).wait()
        return c

    lax.fori_loop(0, tm, issue, 0)
    lax.fori_loop(0, tm, drain, 0)
    w8 = w8_ref[...]
    y = ysh_ref[...]
    for k in range(TOP_K):
        y = y + w8[:, k:k + 1] * buf_ref[k * tm:(k + 1) * tm, :]
    out = x_ref[...] + g2_ref[...] * y
    if final_norm:
        out = out * lax.rsqrt(jnp.mean(out * out, axis=-1, keepdims=True) + RMS_EPS) * fn_ref[...]
    o_ref[...] = out


def _combine(x, ysh, ys, dest_flat, w8, gate2, final_g, S, final_norm):
    T, D = x.shape
    tm = TOKEN_TILE
    nb = S // tm
    row = lambda i: (i, 0)
    return pl.pallas_call(
        functools.partial(_combine_kernel, tm=tm, final_norm=final_norm),
        out_shape=jax.ShapeDtypeStruct((T, D), F32),
        grid=(T // tm,),
        in_specs=[pl.BlockSpec((tm * TOP_K,), lambda i: (i,), memory_space=pltpu.SMEM),
                  pl.BlockSpec((tm, D), row), pl.BlockSpec((tm, D), row),
                  pl.BlockSpec((tm, LANES), row),
                  pl.BlockSpec((None, 1, D), lambda i: (i // nb, 0, 0)),
                  pl.BlockSpec((1, D), lambda i: (0, 0)),
                  pl.BlockSpec(memory_space=pl.ANY)],
        out_specs=pl.BlockSpec((tm, D), row),
        scratch_shapes=[pltpu.VMEM((TOP_K * tm, D), F32), pltpu.SemaphoreType.DMA(())],
        compiler_params=_cparams(1, 32),
        name="combine",
    )(dest_flat, x, ysh, w8, gate2, final_g, ys)


def _reorder_w_in(w, D):
    W = BRANCH_WIDTH
    kd = N_HEADS * GLA_HEAD_K
    sizes = (3 * W, 3 * W, N_HEADS, 3 * W, kd, kd, W, W, GLA_GATE_RANK, 4 * D)
    pts, acc = [], 0
    for s in sizes[:-1]:
        acc += s
        pts.append(acc)
    sb, fox, fox_f, moba, gq, gk, gv, gr, glr, gates = jnp.split(w, pts, axis=1)
    main = jnp.concatenate([sb, fox, moba, gq, gk, gv, gr, gates], axis=1).astype(BF16)
    pad = jnp.zeros((w.shape[0], LANES - N_HEADS - GLA_GATE_RANK), w.dtype)
    misc = jnp.concatenate([fox_f, glr, pad], axis=1).astype(BF16)
    return main, misc


def _pad_lanes(v, offset=0):
    out = jnp.zeros((1, LANES), F32)
    return out.at[0, offset:offset + v.shape[0]].set(v.astype(F32))


def kernel(x, c, positions, attn_norm, w_ada, b_ada, w_in, fox_bias, gla_w_gate, gla_b_gate, gla_norm,
           w_branch, w_out, ffn_norm, w_router, router_bias, w_exp_gate, w_exp_up, w_exp_down,
           w_sh_gate, w_sh_up, w_sh_down, final_norm):
    B, S, D = x.shape
    L = w_ada.shape[0]
    T = B * S
    E = N_EXPERTS
    M = EXPERT_ROWS
    xf = x.reshape(T, D)

    c_pad = jnp.zeros((SUBLANES, D), F32).at[:B].set(c)
    mod = _ada_mod(c_pad, w_ada, b_ada)

    half = ROPE_DIM // 2
    inv_freq = jnp.power(ROPE_THETA, -jnp.arange(half, dtype=F32) * 2.0 / ROPE_DIM)
    freq_lanes = _pad_lanes(jnp.concatenate([inv_freq, inv_freq]))
    pos_col = positions.reshape(T, 1)

    n_blocks = (T * TOP_K + E * (M - 1) + M - 1) // M
    P = n_blocks * M

    for l in range(L):
        m6 = mod[l, :B].reshape(B, 6, 1, D)
        shift1, scale1, gate1, shift2, scale2, gate2 = (m6[:, n] for n in range(6))

        h1 = _norm_mod(xf, attn_norm[l].reshape(1, D), scale1, shift1, S)
        w_main, w_misc = _reorder_w_in(w_in[l], D)
        proj = _matmul(h1, w_main, tm=1024, tn=1024)
        misc = _matmul(h1, w_misc, tm=1024, tn=LANES)

        o_a = _sb_attention(proj, B, S, COL_SB)
        qa, ka = _fox_prep(proj, misc, _pad_lanes(fox_bias[l], MISC_FOX_F), B, S)
        o_b = _flash_attention(qa, ka, proj, B, S, COL_FOX + 2 * N_HEADS)
        qa, ka = _moba_prep(proj, pos_col, freq_lanes, B, S)
        o_c = _flash_attention(qa, ka, proj, B, S, COL_MOBA + 2 * N_HEADS)
        wg_pad = jnp.zeros((LANES, gla_w_gate.shape[2]), F32).at[MISC_GLR:MISC_GLR + GLA_GATE_RANK].set(gla_w_gate[l])
        o_d = _gla(proj, misc, wg_pad, gla_b_gate[l].reshape(1, -1), gla_norm[l].reshape(1, -1), B, S)

        merged = _merge((o_a, o_b, o_c, o_d), proj, w_branch[l])
        xf = _matmul(merged, w_out[l], tm=512, tn=1024, residual=(xf, gate1), S=S)

        wr_pad = jnp.zeros((D, LANES), F32).at[:, :E].set(w_router[l])
        hp, logits = _norm_mod(xf, ffn_norm[l].reshape(1, D), scale2, shift2, S, w_router_pad=wr_pad)
        e8, w8, p8, counts = _route(logits, _pad_lanes(router_bias[l]))
        cnt = counts[0, :E].astype(I32)
        padded = (cnt + M - 1) // M * M
        pad_end = jnp.cumsum(padded)
        pad_start = pad_end - padded
        dest = (pad_start[e8[:, :TOP_K]] + p8[:, :TOP_K]).reshape(T * TOP_K)
        blk_e = jnp.minimum(jnp.searchsorted(pad_end, jnp.arange(n_blocks, dtype=I32) * M, side='right'),
                            E - 1).astype(I32)
        n_used = (pad_end[E - 1:] // M).astype(I32)

        xs = _dispatch(hp, dest, pad_end.astype(I32), padded.astype(I32), P)
        ys = _experts(xs, blk_e, n_used, w_exp_gate[l], w_exp_up[l], w_exp_down[l])
        ysh = _experts(hp, jnp.zeros((T // M,), I32), jnp.full((1,), T // M, I32),
                       w_sh_gate[l][None], w_sh_up[l][None], w_sh_down[l][None])
        last = l == L - 1
        xf = _combine(xf, ysh, ys, dest, w8, gate2, final_norm.reshape(1, D), S, final_norm=last)

    return xf.reshape(B, S, D)
```

```python
import functools

import jax
import jax.numpy as jnp
from jax import lax
from jax.experimental import pallas as pl
from jax.experimental.pallas import tpu as pltpu

F32 = jnp.float32
BF16 = jnp.bfloat16
I32 = jnp.int32
U32 = jnp.uint32

HEAD_DIM = 128
N_HEADS = 4
BRANCH_WIDTH = N_HEADS * HEAD_DIM
GLA_HEAD_K = 64
GLA_CHUNK = 64
GLA_GATE_RANK = 16
GLA_GATE_NORMALIZER = 16.0
MOBA_BLOCK = 256
MOBA_TOPK = 3
ROPE_THETA = 500000.0
ROPE_DIM = HEAD_DIM // 4
N_EXPERTS = 64
GROUP_SIZE = 8
TOPK_GROUPS = 4
TOP_K = 8
ROUTED_SCALE = 2.5
RMS_EPS = 1e-6

LANES = 128
SUBLANES = 8
MIB = 1024 * 1024
NEG_BIG = -1e30

ATT_BLOCK = 256
EXPERT_ROWS = 256
TOKEN_TILE = 128

COL_SB, COL_FOX, COL_MOBA = 0, 12, 24
COL_GQ, COL_GK, COL_GV, COL_GR, COL_GATES = 36, 38, 40, 44, 48
N_MAIN_COLS = 112 * LANES
MISC_FOX_F, MISC_GLR = 0, 4


def _cparams(n_axes, vmem_mib):
    return pltpu.CompilerParams(dimension_semantics=("arbitrary",) * n_axes,
                                vmem_limit_bytes=vmem_mib * MIB)


def _softplus(z):
    return jnp.maximum(z, 0.0) + jnp.log(1.0 + jnp.exp(-jnp.abs(z)))


def _log_sigmoid(z):
    return -_softplus(-z)


def _sigmoid(z):
    return 1.0 / (1.0 + jnp.exp(-z))


def _dot(a, b, precision=None):
    return jnp.dot(a, b, preferred_element_type=F32, precision=precision)


def _dot_nt(a, b, precision=None):
    return lax.dot_general(a, b, (((1,), (1,)), ((), ())), preferred_element_type=F32,
                           precision=precision)


def _dot_tn(a, b, precision=None):
    return lax.dot_general(a, b, (((0,), (0,)), ((), ())), preferred_element_type=F32,
                           precision=precision)


HIGHEST = lax.Precision.HIGHEST


def _split3(x):
    hi = x.astype(BF16)
    r1 = x - hi.astype(F32)
    mid = r1.astype(BF16)
    lo = (r1 - mid.astype(F32)).astype(BF16)
    return hi, mid, lo


def _ada_kernel(c_ref, w_ref, b_ref, o_ref):
    c = c_ref[...]
    c_act = (c * _sigmoid(c)).astype(BF16)
    o_ref[...] = _dot(c_act, w_ref[...].astype(BF16)) + b_ref[...]


def _ada_mod(c_pad, w_ada, b_ada):
    L, D, N = w_ada.shape
    tn = 1536
    return pl.pallas_call(
        _ada_kernel,
        out_shape=jax.ShapeDtypeStruct((L, SUBLANES, N), F32),
        grid=(L, N // tn),
        in_specs=[pl.BlockSpec((SUBLANES, D), lambda l, j: (0, 0)),
                  pl.BlockSpec((None, D, tn), lambda l, j: (l, 0, j)),
                  pl.BlockSpec((None, 1, tn), lambda l, j: (l, 0, j))],
        out_specs=pl.BlockSpec((None, SUBLANES, tn), lambda l, j: (l, 0, j)),
        compiler_params=_cparams(2, 40),
        name="ada_mod",
    )(c_pad, w_ada, b_ada.reshape(L, 1, N))


def _normed(x, g, scale, shift):
    y = x * lax.rsqrt(jnp.mean(x * x, axis=-1, keepdims=True) + RMS_EPS)
    return (y * g) * (1.0 + scale) + shift


def _norm_mod_kernel(x_ref, g_ref, sc_ref, sh_ref, h_ref):
    h_ref[...] = _normed(x_ref[...], g_ref[...], sc_ref[...], sh_ref[...]).astype(BF16)


def _pack_pairs(h):
    half = h.shape[1] // 2
    hi = lax.bitcast_convert_type(h[:, :half].astype(BF16).astype(F32), U32)
    lo = lax.bitcast_convert_type(h[:, half:].astype(BF16).astype(F32), U32)
    return hi | (lo >> 16)


def _unpack_pairs(w):
    a = lax.bitcast_convert_type(w & jnp.uint32(0xFFFF0000), F32).astype(BF16)
    b = lax.bitcast_convert_type(w << 16, F32).astype(BF16)
    return jnp.concatenate([a, b], axis=1)


def _norm_route_kernel(x_ref, g_ref, sc_ref, sh_ref, wr_ref, hp_ref, lg_ref):
    h = _normed(x_ref[...], g_ref[...], sc_ref[...], sh_ref[...])
    hp_ref[...] = _pack_pairs(h)
    lg_ref[...] = _dot(h, wr_ref[...], precision=HIGHEST)


def _norm_mod(x, g, scale, shift, S, w_router_pad=None):
    T, D = x.shape
    tm = 512
    nb = S // tm
    row = lambda i: (i, 0)
    per_batch = pl.BlockSpec((None, 1, D), lambda i: (i // nb, 0, 0))
    in_specs = [pl.BlockSpec((tm, D), row), pl.BlockSpec((1, D), lambda i: (0, 0)), per_batch, per_batch]
    if w_router_pad is None:
        return pl.pallas_call(
            _norm_mod_kernel, out_shape=jax.ShapeDtypeStruct((T, D), BF16), grid=(T // tm,),
            in_specs=in_specs, out_specs=pl.BlockSpec((tm, D), row),
            compiler_params=_cparams(1, 32), name="norm_mod",
        )(x, g, scale, shift)
    return pl.pallas_call(
        _norm_route_kernel,
        out_shape=(jax.ShapeDtypeStruct((T, D // 2), U32), jax.ShapeDtypeStruct((T, LANES), F32)),
        grid=(T // tm,),
        in_specs=in_specs + [pl.BlockSpec((D, LANES), lambda i: (0, 0))],
        out_specs=(pl.BlockSpec((tm, D // 2), row), pl.BlockSpec((tm, LANES), row)),
        compiler_params=_cparams(1, 32), name="norm_route",
    )(x, g, scale, shift, w_router_pad)


def _mm_kernel(*refs, cast_w, residual):
    if residual:
        a_ref, w_ref, x_ref, g_ref, o_ref = refs[:5]
        scratch = refs[5:]
    else:
        a_ref, w_ref, o_ref = refs[:3]
        scratch = refs[3:]
    if cast_w:
        wbf_ref, = scratch

        @pl.when(pl.program_id(1) == 0)
        def _():
            wbf_ref[...] = w_ref[...].astype(BF16)

        w = wbf_ref[...]
    else:
        w = w_ref[...]
    acc = _dot(a_ref[...], w)
    if residual:
        acc = x_ref[...] + g_ref[...] * acc
    o_ref[...] = acc.astype(o_ref.dtype)


def _matmul(a, w, tm, tn, out_dtype=F32, residual=None, S=None, vmem_mib=48, layer=None):
    M, K = a.shape
    N = w.shape[-1]
    cast_w = w.dtype != BF16
    if layer is None:
        w_spec = pl.BlockSpec((K, tn), lambda j, i: (0, j))
    else:
        w_spec = pl.BlockSpec((None, K, tn), lambda j, i: (layer, 0, j))
    in_specs = [pl.BlockSpec((tm, K), lambda j, i: (i, 0)), w_spec]
    args = [a, w]
    if residual is not None:
        x, gate = residual
        nb = S // tm
        in_specs += [pl.BlockSpec((tm, tn), lambda j, i: (i, j)),
                     pl.BlockSpec((None, 1, tn), lambda j, i: (i // nb, 0, j))]
        args += [x, gate]
    return pl.pallas_call(
        functools.partial(_mm_kernel, cast_w=cast_w, residual=residual is not None),
        out_shape=jax.ShapeDtypeStruct((M, N), out_dtype),
        grid=(N // tn, M // tm),
        in_specs=in_specs,
        out_specs=pl.BlockSpec((tm, tn), lambda j, i: (i, j)),
        scratch_shapes=[pltpu.VMEM((K, tn), BF16)] if cast_w else [],
        compiler_params=_cparams(2, vmem_mib),
        name="matmul_res" if residual is not None else "matmul",
    )(*args)


IN_TN = 1024
IN_SEGMENTS = ((0, COL_MOBA * LANES // IN_TN, 0),
               (COL_MOBA * LANES // IN_TN, COL_GATES * LANES // IN_TN, N_HEADS),
               (COL_GATES * LANES // IN_TN, N_MAIN_COLS // IN_TN, N_HEADS + GLA_GATE_RANK))
ROW_CHUNK = 256


def _in_proj_kernel(a_ref, w_ref, wn_ref, o_ref, wbf_ref, *, tn):
    j = pl.program_id(0)

    @pl.when(pl.program_id(1) == 0)
    def _():
        K = w_ref.shape[0]
        for lo, hi, shift in IN_SEGMENTS:
            @pl.when(jnp.logical_and(j >= lo, j < hi))
            def _():
                for r in range(0, K, ROW_CHUNK):
                    rows = slice(r, r + ROW_CHUNK)
                    if shift == 0:
                        wbf_ref[rows, :] = w_ref[rows, :].astype(BF16)
                    else:
                        wide = jnp.concatenate([w_ref[rows, :], wn_ref[rows, :]], axis=1)
                        wbf_ref[rows, :] = wide[:, shift:shift + tn].astype(BF16)

    o_ref[...] = _dot(a_ref[...], wbf_ref[...])


def _in_proj(a, w_in, layer, tm=512):
    M, K = a.shape
    tn = IN_TN
    return pl.pallas_call(
        functools.partial(_in_proj_kernel, tn=tn),
        out_shape=jax.ShapeDtypeStruct((M, N_MAIN_COLS), F32),
        grid=(N_MAIN_COLS // tn, M // tm),
        in_specs=[pl.BlockSpec((tm, K), lambda j, i: (i, 0)),
                  pl.BlockSpec((None, K, tn), lambda j, i: (layer, 0, j)),
                  pl.BlockSpec((None, K, LANES), lambda j, i: (layer, 0, (j + 1) * (tn // LANES)))],
        out_specs=pl.BlockSpec((tm, tn), lambda j, i: (i, j)),
        scratch_shapes=[pltpu.VMEM((K, tn), BF16)],
        compiler_params=_cparams(2, 48),
        name="in_proj",
    )(a, w_in, w_in)


def _sb_kernel(q_ref, k_ref, v_ref, o_ref, *, blk, scale):
    i = pl.program_id(2)
    q = (q_ref[...] * scale).astype(BF16)
    row = lax.broadcasted_iota(I32, (blk, blk), 0)
    col = lax.broadcasted_iota(I32, (blk, blk), 1)
    later = (row > col).astype(BF16)

    def step(jj, carry):
        c, acc = carry
        j = i - jj
        start = pl.multiple_of(j * blk, blk)
        k = k_ref[pl.ds(start, blk), :].astype(BF16)
        v = v_ref[pl.ds(start, blk), :].astype(BF16)
        z = _dot_nt(q, k)
        past = jnp.logical_or(j < i, col < row)
        ls = jnp.where(past, -_softplus(z), 0.0)
        hi = ls.astype(BF16)
        lo = (ls - hi.astype(F32)).astype(BF16)
        between = _dot(hi, later) + _dot(lo, later)
        w = jnp.where(past, jnp.exp(z + ls + between + c), 0.0)
        acc = acc + _dot(w.astype(BF16), v)
        c = c + jnp.sum(ls, axis=1, keepdims=True)
        return c, acc

    _, acc = lax.fori_loop(0, i + 1, step,
                           (jnp.zeros((blk, 1), F32), jnp.zeros((blk, HEAD_DIM), F32)))
    o_ref[...] = acc


def _sb_attention(proj, B, S, col0):
    T = proj.shape[0]
    blk = ATT_BLOCK
    nq = S // blk
    return pl.pallas_call(
        functools.partial(_sb_kernel, blk=blk, scale=HEAD_DIM ** -0.5),
        out_shape=jax.ShapeDtypeStruct((T, BRANCH_WIDTH), F32),
        grid=(B, N_HEADS, nq),
        in_specs=[pl.BlockSpec((blk, HEAD_DIM), lambda b, h, i: (b * nq + i, col0 + h)),
                  pl.BlockSpec((S, HEAD_DIM), lambda b, h, i: (b, col0 + N_HEADS + h)),
                  pl.BlockSpec((S, HEAD_DIM), lambda b, h, i: (b, col0 + 2 * N_HEADS + h))],
        out_specs=pl.BlockSpec((blk, HEAD_DIM), lambda b, h, i: (b * nq + i, h)),
        compiler_params=_cparams(3, 32),
        name="sb_attention",
    )(proj, proj, proj)


def _flash_kernel(qa_ref, ka_ref, v_ref, o_ref, *, blk):
    i = pl.program_id(2)
    qa = qa_ref[...]
    row = lax.broadcasted_iota(I32, (blk, blk), 0)
    col = lax.broadcasted_iota(I32, (blk, blk), 1)

    def step(j, carry):
        m, l, acc = carry
        start = pl.multiple_of(j * blk, blk)
        s = _dot_nt(qa, ka_ref[pl.ds(start, blk), :])
        s = jnp.where(jnp.logical_or(j < i, col <= row), s, NEG_BIG)
        m_new = jnp.maximum(m, jnp.max(s, axis=1, keepdims=True))
        alpha = jnp.exp(m - m_new)
        p = jnp.exp(s - m_new)
        l = alpha * l + jnp.sum(p, axis=1, keepdims=True)
        acc = alpha * acc + _dot(p.astype(BF16), v_ref[pl.ds(start, blk), :].astype(BF16))
        return m_new, l, acc

    init = (jnp.full((blk, 1), NEG_BIG, F32), jnp.zeros((blk, 1), F32), jnp.zeros((blk, HEAD_DIM), F32))
    _, l, acc = lax.fori_loop(0, i + 1, step, init)
    o_ref[...] = acc / l


def _flash_attention(qa, ka, proj, B, S, vcol0):
    T = qa.shape[0]
    blk = ATT_BLOCK
    nq = S // blk
    return pl.pallas_call(
        functools.partial(_flash_kernel, blk=blk),
        out_shape=jax.ShapeDtypeStruct((T, BRANCH_WIDTH), F32),
        grid=(B, N_HEADS, nq),
        in_specs=[pl.BlockSpec((blk, 2 * HEAD_DIM), lambda b, h, i: (b * nq + i, h)),
                  pl.BlockSpec((S, 2 * HEAD_DIM), lambda b, h, i: (b, h)),
                  pl.BlockSpec((S, HEAD_DIM), lambda b, h, i: (b, vcol0 + h))],
        out_specs=pl.BlockSpec((blk, HEAD_DIM), lambda b, h, i: (b * nq + i, h)),
        compiler_params=_cparams(3, 32),
        name="flash_attention",
    )(qa, ka, proj)


def _fox_prep_kernel(q_ref, k_ref, misc_ref, bias_ref, qa_ref, ka_ref, carry_ref, *, tb, scale):
    @pl.when(pl.program_id(1) == 0)
    def _():
        carry_ref[...] = jnp.zeros_like(carry_ref)

    lane = lax.broadcasted_iota(I32, (tb, LANES), 1)
    lf = jnp.where(lane < N_HEADS, _log_sigmoid(misc_ref[...] + bias_ref[...]), 0.0)
    row = lax.broadcasted_iota(I32, (tb, tb), 0)
    col = lax.broadcasted_iota(I32, (tb, tb), 1)
    incl = (col <= row).astype(F32)
    F = _dot(incl, lf, precision=HIGHEST) + carry_ref[0:1, :]
    carry_ref[0:1, :] = F[tb - 1:tb, :]
    for h in range(N_HEADS):
        Fh = jnp.broadcast_to(F[:, h:h + 1], (tb, LANES))
        hi, mid, lo = (p.astype(F32) for p in _split3(Fh))
        ones = jnp.ones((tb, LANES), F32)
        zeros = jnp.zeros((tb, LANES), F32)
        eq = jnp.where(lane == 0, hi, jnp.where(lane == 1, mid, jnp.where(lane == 2, lo,
             jnp.where(lane < 6, ones, zeros))))
        ek = jnp.where(lane < 3, ones, jnp.where(lane == 3, -hi, jnp.where(lane == 4, -mid,
             jnp.where(lane == 5, -lo, zeros))))
        hs = slice(h * HEAD_DIM, (h + 1) * HEAD_DIM)
        qa_ref[:, 2 * h * HEAD_DIM:(2 * h + 1) * HEAD_DIM] = (q_ref[:, hs] * scale).astype(BF16)
        qa_ref[:, (2 * h + 1) * HEAD_DIM:(2 * h + 2) * HEAD_DIM] = eq.astype(BF16)
        ka_ref[:, 2 * h * HEAD_DIM:(2 * h + 1) * HEAD_DIM] = k_ref[:, hs].astype(BF16)
        ka_ref[:, (2 * h + 1) * HEAD_DIM:(2 * h + 2) * HEAD_DIM] = ek.astype(BF16)


def _fox_prep(proj, misc, fox_bias_pad, B, S):
    T = proj.shape[0]
    tb = 512
    nb = S // tb
    qblk = COL_FOX * LANES // BRANCH_WIDTH
    out = jax.ShapeDtypeStruct((T, 2 * BRANCH_WIDTH), BF16)
    return pl.pallas_call(
        functools.partial(_fox_prep_kernel, tb=tb, scale=HEAD_DIM ** -0.5),
        out_shape=(out, out),
        grid=(B, nb),
        in_specs=[pl.BlockSpec((tb, BRANCH_WIDTH), lambda b, i: (b * nb + i, qblk)),
                  pl.BlockSpec((tb, BRANCH_WIDTH), lambda b, i: (b * nb + i, qblk + 1)),
                  pl.BlockSpec((tb, LANES), lambda b, i: (b * nb + i, 0)),
                  pl.BlockSpec((1, LANES), lambda b, i: (0, 0))],
        out_specs=(pl.BlockSpec((tb, 2 * BRANCH_WIDTH), lambda b, i: (b * nb + i, 0)),
                   pl.BlockSpec((tb, 2 * BRANCH_WIDTH), lambda b, i: (b * nb + i, 0))),
        scratch_shapes=[pltpu.VMEM((SUBLANES, LANES), F32)],
        compiler_params=_cparams(2, 32),
        name="fox_prep",
    )(proj, proj, misc, fox_bias_pad)


def _rope(x, cos, sin, lane):
    half = ROPE_DIM // 2
    up = pltpu.roll(x, half, 1)
    down = pltpu.roll(x, LANES - half, 1)
    rot = jnp.where(lane < half, -down * sin, jnp.where(lane < ROPE_DIM, up * sin, 0.0))
    return x * jnp.where(lane < ROPE_DIM, cos, 1.0) + rot


def _angles(pos_ref, freq_ref):
    ang = pos_ref[...].astype(F32) * freq_ref[...]
    return jnp.cos(ang), jnp.sin(ang)


def _moba_k_kernel(k_ref, pos_ref, freq_ref, ka_ref, kmean_ref, *, tb):
    i = pl.program_id(1)

    @pl.when(i == 0)
    def _():
        kmean_ref[...] = jnp.zeros_like(kmean_ref)

    lane = lax.broadcasted_iota(I32, (tb, LANES), 1)
    cos, sin = _angles(pos_ref, freq_ref)
    onehot = jnp.where(lane == i, 1.0, 0.0).astype(BF16)
    this_row = lax.broadcasted_iota(I32, (LANES, HEAD_DIM), 0) == i
    for h in range(N_HEADS):
        hs = slice(h * HEAD_DIM, (h + 1) * HEAD_DIM)
        kr = _rope(k_ref[:, hs], cos, sin, lane)
        mean = jnp.sum(kr, axis=0, keepdims=True) * (1.0 / tb)
        kmean_ref[:, hs] = jnp.where(this_row, mean, kmean_ref[:, hs])
        ka_ref[:, 2 * h * HEAD_DIM:(2 * h + 1) * HEAD_DIM] = kr.astype(BF16)
        ka_ref[:, (2 * h + 1) * HEAD_DIM:(2 * h + 2) * HEAD_DIM] = onehot


def _moba_q_kernel(q_ref, pos_ref, freq_ref, kmean_ref, qa_ref, *, tb, scale):
    i = pl.program_id(1)
    lane = lax.broadcasted_iota(I32, (tb, LANES), 1)
    cos, sin = _angles(pos_ref, freq_ref)
    neg_inf = jnp.float32(-jnp.inf)
    for h in range(N_HEADS):
        hs = slice(h * HEAD_DIM, (h + 1) * HEAD_DIM)
        qr = _rope(q_ref[:, hs], cos, sin, lane)
        gate = _dot_nt(qr, kmean_ref[:, hs], precision=HIGHEST)
        cur = jnp.where(lane < i, gate, neg_inf)
        chosen = lane == i
        for _ in range(MOBA_TOPK):
            m = jnp.max(cur, axis=1, keepdims=True)
            first = jnp.min(jnp.where(jnp.logical_and(cur == m, m > neg_inf), lane, LANES),
                            axis=1, keepdims=True)
            pick = lane == first
            chosen = jnp.logical_or(chosen, pick)
            cur = jnp.where(pick, neg_inf, cur)
        bias = jnp.where(jnp.logical_or(chosen, lane >= LANES // 2), 0.0, NEG_BIG)
        qa_ref[:, 2 * h * HEAD_DIM:(2 * h + 1) * HEAD_DIM] = (qr * scale).astype(BF16)
        qa_ref[:, (2 * h + 1) * HEAD_DIM:(2 * h + 2) * HEAD_DIM] = bias.astype(BF16)


def _moba_prep(proj, pos_col, freq_lanes, B, S):
    T = proj.shape[0]
    tb = MOBA_BLOCK
    nb = S // tb
    qblk = COL_MOBA * LANES // BRANCH_WIDTH
    aug = jax.ShapeDtypeStruct((T, 2 * BRANCH_WIDTH), BF16)
    row_spec = lambda c: pl.BlockSpec((tb, BRANCH_WIDTH), lambda b, i: (b * nb + i, c))
    pos_spec = pl.BlockSpec((tb, 1), lambda b, i: (b * nb + i, 0))
    freq_spec = pl.BlockSpec((1, LANES), lambda b, i: (0, 0))
    aug_spec = pl.BlockSpec((tb, 2 * BRANCH_WIDTH), lambda b, i: (b * nb + i, 0))
    kmean_spec = pl.BlockSpec((None, LANES, BRANCH_WIDTH), lambda b, i: (b, 0, 0))
    ka, kmean = pl.pallas_call(
        functools.partial(_moba_k_kernel, tb=tb),
        out_shape=(aug, jax.ShapeDtypeStruct((B, LANES, BRANCH_WIDTH), F32)),
        grid=(B, nb),
        in_specs=[row_spec(qblk + 1), pos_spec, freq_spec],
        out_specs=(aug_spec, kmean_spec),
        compiler_params=_cparams(2, 32),
        name="moba_k_prep",
    )(proj, pos_col, freq_lanes)
    qa = pl.pallas_call(
        functools.partial(_moba_q_kernel, tb=tb, scale=HEAD_DIM ** -0.5),
        out_shape=aug,
        grid=(B, nb),
        in_specs=[row_spec(qblk), pos_spec, freq_spec, kmean_spec],
        out_specs=aug_spec,
        compiler_params=_cparams(2, 32),
        name="moba_q_prep",
    )(proj, pos_col, freq_lanes, kmean)
    return qa, ka


def _gla_kernel(q_ref, k_ref, v_ref, gr_ref, misc_ref, wg_ref, bg_ref, gn_ref, o_ref, state_ref, *, tb):
    C = GLA_CHUNK
    @pl.when(pl.program_id(1) == 0)
    def _():
        state_ref[...] = jnp.zeros_like(state_ref)

    la_all = _log_sigmoid(_dot(misc_ref[...], wg_ref[...], precision=HIGHEST) + bg_ref[...]) \
        * (1.0 / GLA_GATE_NORMALIZER)
    rowc = lax.broadcasted_iota(I32, (C, C), 0)
    colc = lax.broadcasted_iota(I32, (C, C), 1)
    incl = (colc <= rowc).astype(F32)
    causal = colc <= rowc
    lane = lax.broadcasted_iota(I32, (C, LANES), 1)
    row2 = lax.broadcasted_iota(I32, (2 * GLA_HEAD_K, LANES), 0)
    ones_cv = jnp.ones((C, LANES), F32)
    qscale = GLA_HEAD_K ** -0.5
    gn = gn_ref[...]
    for c in range(tb // C):
        rs = slice(c * C, (c + 1) * C)
        for p in range(2):
            ps = slice(p * LANES, (p + 1) * LANES)
            la = la_all[rs, ps]
            b = _dot(incl, la, precision=HIGHEST)
            b_last = b[C - 1:C, :]
            eb = jnp.exp(b)
            q_dec = q_ref[rs, ps] * qscale * eb
            k = k_ref[rs, ps]
            k_inv = (k * jnp.exp(-b)).astype(BF16)
            k_end = (k * jnp.exp(b_last - b)).astype(BF16)
            decay = jnp.exp(_dot_tn(la, ones_cv, precision=HIGHEST))
            st = state_ref[p * LANES:(p + 1) * LANES, :]
            new_rows = []
            for e in range(2):
                h = 2 * p + e
                own = jnp.logical_and(lane >= e * GLA_HEAD_K, lane < (e + 1) * GLA_HEAD_K)
                qh = jnp.where(own, q_dec, 0.0).astype(BF16)
                v = v_ref[rs, h * HEAD_DIM:(h + 1) * HEAD_DIM].astype(BF16)
                attn = jnp.where(causal, _dot_nt(qh, k_inv), 0.0)
                st_h = jnp.where(jnp.logical_and(row2 >= e * GLA_HEAD_K, row2 < (e + 1) * GLA_HEAD_K), st, 0.0)
                o = _dot(attn.astype(BF16), v) + _dot(qh, st_h.astype(BF16))
                y = o * lax.rsqrt(jnp.mean(o * o, axis=-1, keepdims=True) + RMS_EPS) * gn
                g = gr_ref[rs, h * HEAD_DIM:(h + 1) * HEAD_DIM]
                o_ref[rs, h * HEAD_DIM:(h + 1) * HEAD_DIM] = y * (g * _sigmoid(g))
                new_rows.append(_dot_tn(k_end, v))
            kv = jnp.where(row2 < GLA_HEAD_K, new_rows[0], new_rows[1])
            state_ref[p * LANES:(p + 1) * LANES, :] = decay * st + kv


def _gla(proj, misc, w_gate_pad, b_gate, gla_norm, B, S):
    T = proj.shape[0]
    tb = 512
    nb = S // tb
    kd = 2 * LANES
    rows = lambda w, c: pl.BlockSpec((tb, w), lambda b, i: (b * nb + i, c))
    const = lambda shape: pl.BlockSpec(shape, lambda b, i: (0, 0))
    return pl.pallas_call(
        functools.partial(_gla_kernel, tb=tb),
        out_shape=jax.ShapeDtypeStruct((T, BRANCH_WIDTH), F32),
        grid=(B, nb),
        in_specs=[rows(kd, COL_GQ * LANES // kd), rows(kd, COL_GK * LANES // kd),
                  rows(BRANCH_WIDTH, COL_GV * LANES // BRANCH_WIDTH),
                  rows(BRANCH_WIDTH, COL_GR * LANES // BRANCH_WIDTH),
                  rows(LANES, 0), const((LANES, kd)), const((1, kd)), const((1, HEAD_DIM))],
        out_specs=rows(BRANCH_WIDTH, 0),
        scratch_shapes=[pltpu.VMEM((kd, HEAD_DIM), F32)],
        compiler_params=_cparams(2, 32),
        name="gla",
    )(proj, proj, proj, proj, misc, w_gate_pad, b_gate, gla_norm)


def _merge_kernel(oa_ref, ob_ref, oc_ref, od_ref, g0_ref, g1_ref, g2_ref, g3_ref, w_ref, o_ref, wbf_ref):
    @pl.when(pl.program_id(1) == 0)
    def _():
        wbf_ref[...] = w_ref[...].astype(BF16)

    acc = None
    for n, (b_ref, g_ref) in enumerate(((oa_ref, g0_ref), (ob_ref, g1_ref), (oc_ref, g2_ref), (od_ref, g3_ref))):
        term = _sigmoid(g_ref[...]) * _dot(b_ref[...].astype(BF16), wbf_ref[n])
        acc = term if acc is None else acc + term
    o_ref[...] = acc.astype(BF16)


def _merge(branches, proj, w_branch, layer):
    T = proj.shape[0]
    _, nbr, W, D = w_branch.shape
    tm, tn = 512, 512
    g0 = COL_GATES * LANES // tn
    per = D // tn
    br_spec = pl.BlockSpec((tm, W), lambda j, i: (i, 0))
    gate_spec = lambda n: pl.BlockSpec((tm, tn), lambda j, i: (i, g0 + n * per + j))
    return pl.pallas_call(
        _merge_kernel,
        out_shape=jax.ShapeDtypeStruct((T, D), BF16),
        grid=(D // tn, T // tm),
        in_specs=[br_spec] * 4 + [gate_spec(n) for n in range(4)]
                 + [pl.BlockSpec((None, nbr, W, tn), lambda j, i: (layer, 0, 0, j))],
        out_specs=pl.BlockSpec((tm, tn), lambda j, i: (i, j)),
        scratch_shapes=[pltpu.VMEM((nbr, W, tn), BF16)],
        compiler_params=_cparams(2, 40),
        name="branch_merge",
    )(*branches, proj, proj, proj, proj, w_branch)


def _butterfly(x, lane, op):
    for s in (1, 2, 4):
        up = pltpu.roll(x, s, 1)
        down = pltpu.roll(x, LANES - s, 1)
        x = op(x, jnp.where((lane & s) != 0, up, down))
    return x


def _route_kernel(lg_ref, bias_ref, e8_ref, w8_ref, p8_ref, cnt_ref, carry_ref, *, tm):
    @pl.when(pl.program_id(0) == 0)
    def _():
        carry_ref[...] = jnp.zeros_like(carry_ref)

    neg_inf = jnp.float32(-jnp.inf)
    lane = lax.broadcasted_iota(I32, (tm, LANES), 1)
    valid = lane < N_EXPERTS
    scores = _sigmoid(lg_ref[...])
    biased = jnp.where(valid, scores + bias_ref[...], neg_inf)
    g1 = _butterfly(biased, lane, jnp.maximum)
    first = _butterfly(jnp.where(biased == g1, lane, LANES), lane, jnp.minimum)
    g2 = _butterfly(jnp.where(lane == first, neg_inf, biased), lane, jnp.maximum)
    gs = g1 + g2
    gs = jnp.where(valid, gs, pltpu.roll(gs, N_EXPERTS, 1))
    gidx = lane >> 3
    beaten = jnp.zeros((tm, LANES), I32)
    for r in range(1, N_EXPERTS // GROUP_SIZE):
        other = pltpu.roll(gs, GROUP_SIZE * r, 1)
        og = (gidx - r) & (N_EXPERTS // GROUP_SIZE - 1)
        wins = jnp.logical_or(other > gs, jnp.logical_and(other == gs, og < gidx))
        beaten = beaten + wins.astype(I32)
    cur = jnp.where(jnp.logical_and(beaten < TOPK_GROUPS, valid), biased, neg_inf)
    sel = jnp.zeros((tm, LANES), jnp.bool_)
    for _ in range(TOP_K):
        m = jnp.max(cur, axis=1, keepdims=True)
        pick = lane == jnp.min(jnp.where(cur == m, lane, LANES), axis=1, keepdims=True)
        sel = jnp.logical_or(sel, pick)
        cur = jnp.where(pick, neg_inf, cur)
    wsel = jnp.where(sel, scores, 0.0)
    wd = wsel / jnp.sum(wsel, axis=1, keepdims=True) * ROUTED_SCALE
    selb = jnp.where(sel, 1.0, 0.0).astype(BF16)
    row = lax.broadcasted_iota(I32, (tm, tm), 0)
    col = lax.broadcasted_iota(I32, (tm, tm), 1)
    pos = _dot((col < row).astype(BF16), selb) + carry_ref[0:1, :]
    total = carry_ref[0:1, :] + jnp.sum(selb.astype(F32), axis=0, keepdims=True)
    carry_ref[0:1, :] = total
    cnt_ref[...] = jnp.broadcast_to(total, cnt_ref.shape)
    r2 = lax.broadcasted_iota(I32, (LANES, LANES), 0)
    c2 = lax.broadcasted_iota(I32, (LANES, LANES), 1)
    slot = _dot(selb, (r2 < c2).astype(BF16))
    lane_f = lane.astype(F32)
    e8 = jnp.zeros((tm, LANES), F32)
    w8 = jnp.zeros((tm, LANES), F32)
    p8 = jnp.zeros((tm, LANES), F32)
    for k in range(TOP_K):
        mk = jnp.logical_and(sel, slot == k)
        put = lane == k
        e8 = jnp.where(put, jnp.sum(jnp.where(mk, lane_f, 0.0), axis=1, keepdims=True), e8)
        w8 = jnp.where(put, jnp.sum(jnp.where(mk, wd, 0.0), axis=1, keepdims=True), w8)
        p8 = jnp.where(put, jnp.sum(jnp.where(mk, pos, 0.0), axis=1, keepdims=True), p8)
    e8_ref[...] = e8.astype(I32)
    w8_ref[...] = w8
    p8_ref[...] = p8.astype(I32)


def _route(logits, router_bias_pad):
    T = logits.shape[0]
    tm = 512
    row = pl.BlockSpec((tm, LANES), lambda i: (i, 0))
    return pl.pallas_call(
        functools.partial(_route_kernel, tm=tm),
        out_shape=(jax.ShapeDtypeStruct((T, LANES), I32), jax.ShapeDtypeStruct((T, LANES), F32),
                   jax.ShapeDtypeStruct((T, LANES), I32), jax.ShapeDtypeStruct((SUBLANES, LANES), F32)),
        grid=(T // tm,),
        in_specs=[row, pl.BlockSpec((1, LANES), lambda i: (0, 0))],
        out_specs=(row, row, row, pl.BlockSpec((SUBLANES, LANES), lambda i: (0, 0))),
        scratch_shapes=[pltpu.VMEM((SUBLANES, LANES), F32)],
        compiler_params=_cparams(1, 32),
        name="route",
    )(logits, router_bias_pad)


def _row_copy(src, src_row, dst, dst_row, sem):
    return pltpu.make_async_copy(src.at[pl.ds(src_row, 1), :], dst.at[pl.ds(dst_row, 1), :], sem)


def _dispatch_kernel(pad_end_ref, padded_ref, dest_ref, hp_ref, xs_ref, zero_ref, sem_ref, *, tm, rows, n_blocks):
    @pl.when(pl.program_id(0) == 0)
    def _():
        zero_ref[...] = jnp.zeros_like(zero_ref)

        def fill(e, do_wait):
            @pl.when(padded_ref[e] > 0)
            def _():
                start = pl.multiple_of(pad_end_ref[e] - rows, rows)
                cp = pltpu.make_async_copy(zero_ref, xs_ref.at[pl.ds(start, rows), :], sem_ref)
                if do_wait:
                    cp.wait()
                else:
                    cp.start()

        lax.fori_loop(0, N_EXPERTS, lambda e, c: (fill(e, False), c)[1], 0)
        lax.fori_loop(0, N_EXPERTS, lambda e, c: (fill(e, True), c)[1], 0)

        def tail(b):
            return pltpu.make_async_copy(zero_ref, xs_ref.at[pl.ds(pl.multiple_of(b * rows, rows), rows), :], sem_ref)

        n_used = pad_end_ref[N_EXPERTS - 1] // rows
        lax.fori_loop(n_used, n_blocks, lambda b, c: (tail(b).start(), c)[1], 0)
        lax.fori_loop(n_used, n_blocks, lambda b, c: (tail(b).wait(), c)[1], 0)

    def issue(t, c):
        for k in range(TOP_K):
            _row_copy(hp_ref, t, xs_ref, dest_ref[t * TOP_K + k], sem_ref).start()
        return c

    def drain(t, c):
        for k in range(TOP_K):
            _row_copy(hp_ref, t, xs_ref, dest_ref[t * TOP_K + k], sem_ref).wait()
        return c

    lax.fori_loop(0, tm, issue, 0)
    lax.fori_loop(0, tm, drain, 0)


def _dispatch(hp, dest_flat, pad_end, padded, n_rows):
    T, W = hp.shape
    tm = TOKEN_TILE
    grid_spec = pltpu.PrefetchScalarGridSpec(
        num_scalar_prefetch=2,
        grid=(T // tm,),
        in_specs=[pl.BlockSpec((tm * TOP_K,), lambda i, pe, pd: (i,), memory_space=pltpu.SMEM),
                  pl.BlockSpec((tm, W), lambda i, pe, pd: (i, 0))],
        out_specs=pl.BlockSpec(memory_space=pl.ANY),
        scratch_shapes=[pltpu.VMEM((EXPERT_ROWS, W), U32), pltpu.SemaphoreType.DMA(())],
    )
    return pl.pallas_call(
        functools.partial(_dispatch_kernel, tm=tm, rows=EXPERT_ROWS, n_blocks=n_rows // EXPERT_ROWS),
        out_shape=jax.ShapeDtypeStruct((n_rows, W), U32),
        grid_spec=grid_spec,
        compiler_params=_cparams(1, 32),
        name="dispatch",
    )(pad_end, padded, dest_flat, hp)


def _expert_kernel(blk_e_ref, n_used_ref, x_ref, wg_ref, wu_ref, wd_ref, y_ref, wg_bf, wu_bf, wd_bf):
    i = pl.program_id(0)
    prev = blk_e_ref[jnp.maximum(i - 1, 0)]
    active = i < n_used_ref[0]

    @pl.when(jnp.logical_and(active, jnp.logical_or(i == 0, blk_e_ref[i] != prev)))
    def _():
        wg_bf[...] = wg_ref[...].astype(BF16)
        wu_bf[...] = wu_ref[...].astype(BF16)
        wd_bf[...] = wd_ref[...].astype(BF16)

    @pl.when(active)
    def _():
        x = _unpack_pairs(x_ref[...])
        g = _dot(x, wg_bf[...])
        u = _dot(x, wu_bf[...])
        hb = (g * _sigmoid(g)) * u
        y_ref[...] = _dot(hb.astype(BF16), wd_bf[...])

    @pl.when(jnp.logical_not(active))
    def _():
        y_ref[...] = jnp.zeros_like(y_ref)


def _experts(xs, blk_e, n_used, w_gate, w_up, w_down, layer):
    P, W = xs.shape
    _, E, D, Hx = w_gate.shape
    M = EXPERT_ROWS
    nblk = P // M
    xrow = lambda i, be, nu: (jnp.minimum(i, nu[0] - 1), 0)
    wsel = lambda i, be, nu: (layer, be[i], 0, 0)
    grid_spec = pltpu.PrefetchScalarGridSpec(
        num_scalar_prefetch=2,
        grid=(nblk,),
        in_specs=[pl.BlockSpec((M, W), xrow),
                  pl.BlockSpec((None, None, D, Hx), wsel), pl.BlockSpec((None, None, D, Hx), wsel),
                  pl.BlockSpec((None, None, Hx, D), wsel)],
        out_specs=pl.BlockSpec((M, D), lambda i, be, nu: (i, 0)),
        scratch_shapes=[pltpu.VMEM((D, Hx), BF16), pltpu.VMEM((D, Hx), BF16), pltpu.VMEM((Hx, D), BF16)],
    )
    return pl.pallas_call(
        _expert_kernel,
        out_shape=jax.ShapeDtypeStruct((P, D), F32),
        grid_spec=grid_spec,
        compiler_params=_cparams(1, 52),
        name="experts",
    )(blk_e, n_used, xs, w_gate, w_up, w_down)


def _combine_kernel(dest_ref, x_ref, ysh_ref, w8_ref, g2_ref, fn_ref, ys_ref, o_ref, buf_ref, sem_ref,
                    *, tm, final_norm):
    def issue(t, c):
        for k in range(TOP_K):
            _row_copy(ys_ref, dest_ref[t * TOP_K + k], buf_ref, k * tm + t, sem_ref).start()
        return c

    def drain(t, c):
        for k in range(TOP_K):
            _row_copy(ys_ref, dest_ref[t * TOP_K + k], buf_ref, k * tm + t, sem_ref).wait()
        return c

    lax.fori_loop(0, tm, issue, 0)
    lax.fori_loop(0, tm, drain, 0)
    w8 = w8_ref[...]
    y = ysh_ref[...]
    for k in range(TOP_K):
        y = y + w8[:, k:k + 1] * buf_ref[k * tm:(k + 1) * tm, :]
    out = x_ref[...] + g2_ref[...] * y
    if final_norm:
        out = out * lax.rsqrt(jnp.mean(out * out, axis=-1, keepdims=True) + RMS_EPS) * fn_ref[...]
    o_ref[...] = out


def _combine(x, ysh, ys, dest_flat, w8, gate2, final_g, S, final_norm):
    T, D = x.shape
    tm = TOKEN_TILE
    nb = S // tm
    row = lambda i: (i, 0)
    return pl.pallas_call(
        functools.partial(_combine_kernel, tm=tm, final_norm=final_norm),
        out_shape=jax.ShapeDtypeStruct((T, D), F32),
        grid=(T // tm,),
        in_specs=[pl.BlockSpec((tm * TOP_K,), lambda i: (i,), memory_space=pltpu.SMEM),
                  pl.BlockSpec((tm, D), row), pl.BlockSpec((tm, D), row),
                  pl.BlockSpec((tm, LANES), row),
                  pl.BlockSpec((None, 1, D), lambda i: (i // nb, 0, 0)),
                  pl.BlockSpec((1, D), lambda i: (0, 0)),
                  pl.BlockSpec(memory_space=pl.ANY)],
        out_specs=pl.BlockSpec((tm, D), row),
        scratch_shapes=[pltpu.VMEM((TOP_K * tm, D), F32), pltpu.SemaphoreType.DMA(())],
        compiler_params=_cparams(1, 32),
        name="combine",
    )(dest_flat, x, ysh, w8, gate2, final_g, ys)


def _misc_weights(w_in, layer):
    c_fox_f = COL_MOBA * LANES
    c_glr = COL_GATES * LANES + N_HEADS
    D = w_in.shape[1]
    pad = jnp.zeros((D, LANES - N_HEADS - GLA_GATE_RANK), w_in.dtype)
    return jnp.concatenate([w_in[layer, :, c_fox_f:c_fox_f + N_HEADS],
                            w_in[layer, :, c_glr:c_glr + GLA_GATE_RANK], pad], axis=1).astype(BF16)


def _pad_lanes(v, offset=0):
    out = jnp.zeros((1, LANES), F32)
    return out.at[0, offset:offset + v.shape[0]].set(v.astype(F32))


def kernel(x, c, positions, attn_norm, w_ada, b_ada, w_in, fox_bias, gla_w_gate, gla_b_gate, gla_norm,
           w_branch, w_out, ffn_norm, w_router, router_bias, w_exp_gate, w_exp_up, w_exp_down,
           w_sh_gate, w_sh_up, w_sh_down, final_norm):
    B, S, D = x.shape
    L = w_ada.shape[0]
    T = B * S
    E = N_EXPERTS
    M = EXPERT_ROWS
    xf = x.reshape(T, D)

    c_pad = jnp.zeros((SUBLANES, D), F32).at[:B].set(c)
    mod = _ada_mod(c_pad, w_ada, b_ada)

    half = ROPE_DIM // 2
    inv_freq = jnp.power(ROPE_THETA, -jnp.arange(half, dtype=F32) * 2.0 / ROPE_DIM)
    freq_lanes = _pad_lanes(jnp.concatenate([inv_freq, inv_freq]))
    pos_col = positions.reshape(T, 1)

    n_blocks = (T * TOP_K + E * (M - 1) + M - 1) // M
    P = n_blocks * M

    for l in range(L):
        m6 = mod[l, :B].reshape(B, 6, 1, D)
        shift1, scale1, gate1, shift2, scale2, gate2 = (m6[:, n] for n in range(6))

        h1 = _norm_mod(xf, attn_norm[l].reshape(1, D), scale1, shift1, S)
        proj = _in_proj(h1, w_in, l)
        misc = _matmul(h1, _misc_weights(w_in, l), tm=1024, tn=LANES)

        o_a = _sb_attention(proj, B, S, COL_SB)
        qa, ka = _fox_prep(proj, misc, _pad_lanes(fox_bias[l], MISC_FOX_F), B, S)
        o_b = _flash_attention(qa, ka, proj, B, S, COL_FOX + 2 * N_HEADS)
        qa, ka = _moba_prep(proj, pos_col, freq_lanes, B, S)
        o_c = _flash_attention(qa, ka, proj, B, S, COL_MOBA + 2 * N_HEADS)
        wg_pad = jnp.zeros((LANES, gla_w_gate.shape[2]), F32).at[MISC_GLR:MISC_GLR + GLA_GATE_RANK].set(gla_w_gate[l])
        o_d = _gla(proj, misc, wg_pad, gla_b_gate[l].reshape(1, -1), gla_norm[l].reshape(1, -1), B, S)

        merged = _merge((o_a, o_b, o_c, o_d), proj, w_branch, l)
        xf = _matmul(merged, w_out, tm=512, tn=1024, residual=(xf, gate1), S=S, layer=l)

        wr_pad = jnp.zeros((D, LANES), F32).at[:, :E].set(w_router[l])
        hp, logits = _norm_mod(xf, ffn_norm[l].reshape(1, D), scale2, shift2, S, w_router_pad=wr_pad)
        e8, w8, p8, counts = _route(logits, _pad_lanes(router_bias[l]))
        cnt = counts[0, :E].astype(I32)
        padded = (cnt + M - 1) // M * M
        pad_end = jnp.cumsum(padded)
        pad_start = pad_end - padded
        dest = (pad_start[e8[:, :TOP_K]] + p8[:, :TOP_K]).reshape(T * TOP_K)
        blk_start = jnp.arange(n_blocks, dtype=I32) * M
        blk_e = jnp.minimum(jnp.sum((pad_end[None, :] <= blk_start[:, None]).astype(I32), axis=1), E - 1)
        n_used = (pad_end[E - 1:] // M).astype(I32)

        xs = _dispatch(hp, dest, pad_end.astype(I32), padded.astype(I32), P)
        ys = _experts(xs, blk_e, n_used, w_exp_gate, w_exp_up, w_exp_down, l)
        ysh = _experts(hp, jnp.zeros((T // M,), I32), jnp.full((1,), T // M, I32),
                       w_sh_gate[:, None], w_sh_up[:, None], w_sh_down[:, None], l)
        last = l == L - 1
        xf = _combine(xf, ysh, ys, dest, w8, gate2, final_norm.reshape(1, D), S, final_norm=last)

    return xf.reshape(B, S, D)
```

```python
import functools

import jax
import jax.numpy as jnp
from jax import lax
from jax.experimental import pallas as pl
from jax.experimental.pallas import tpu as pltpu

F32 = jnp.float32
BF16 = jnp.bfloat16
I32 = jnp.int32
U32 = jnp.uint32

HEAD_DIM = 128
N_HEADS = 4
BRANCH_WIDTH = N_HEADS * HEAD_DIM
GLA_HEAD_K = 64
GLA_CHUNK = 64
GLA_GATE_RANK = 16
GLA_GATE_NORMALIZER = 16.0
MOBA_BLOCK = 256
MOBA_TOPK = 3
ROPE_THETA = 500000.0
ROPE_DIM = HEAD_DIM // 4
N_EXPERTS = 64
GROUP_SIZE = 8
TOPK_GROUPS = 4
TOP_K = 8
ROUTED_SCALE = 2.5
RMS_EPS = 1e-6

LANES = 128
SUBLANES = 8
MIB = 1024 * 1024
NEG_BIG = -1e30

ATT_BLOCK = 256
EXPERT_ROWS = 256
TOKEN_TILE = 128

COL_SB, COL_FOX, COL_MOBA = 0, 12, 24
COL_GQ, COL_GK, COL_GV, COL_GR, COL_GATES = 36, 38, 40, 44, 48
N_MAIN_COLS = 112 * LANES
MISC_FOX_F, MISC_GLR = 0, 4


def _cparams(n_axes, vmem_mib):
    return pltpu.CompilerParams(dimension_semantics=("arbitrary",) * n_axes,
                                vmem_limit_bytes=vmem_mib * MIB)


def _softplus(z):
    return jnp.maximum(z, 0.0) + jnp.log(1.0 + jnp.exp(-jnp.abs(z)))


def _log_sigmoid(z):
    return -_softplus(-z)


def _sigmoid(z):
    return 1.0 / (1.0 + jnp.exp(-z))


def _dot(a, b, precision=None):
    return jnp.dot(a, b, preferred_element_type=F32, precision=precision)


def _dot_nt(a, b, precision=None):
    return lax.dot_general(a, b, (((1,), (1,)), ((), ())), preferred_element_type=F32,
                           precision=precision)


def _dot_tn(a, b, precision=None):
    return lax.dot_general(a, b, (((0,), (0,)), ((), ())), preferred_element_type=F32,
                           precision=precision)


HIGHEST = lax.Precision.HIGHEST


def _split3(x):
    hi = x.astype(BF16)
    r1 = x - hi.astype(F32)
    mid = r1.astype(BF16)
    lo = (r1 - mid.astype(F32)).astype(BF16)
    return hi, mid, lo


def _ada_kernel(c_ref, w_ref, b_ref, o_ref):
    c = c_ref[...]
    c_act = (c * _sigmoid(c)).astype(BF16)
    o_ref[...] = _dot(c_act, w_ref[...].astype(BF16)) + b_ref[...]


def _ada_mod(c_pad, w_ada, b_ada):
    L, D, N = w_ada.shape
    tn = 1536
    return pl.pallas_call(
        _ada_kernel,
        out_shape=jax.ShapeDtypeStruct((L, SUBLANES, N), F32),
        grid=(L, N // tn),
        in_specs=[pl.BlockSpec((SUBLANES, D), lambda l, j: (0, 0)),
                  pl.BlockSpec((None, D, tn), lambda l, j: (l, 0, j)),
                  pl.BlockSpec((None, 1, tn), lambda l, j: (l, 0, j))],
        out_specs=pl.BlockSpec((None, SUBLANES, tn), lambda l, j: (l, 0, j)),
        compiler_params=_cparams(2, 40),
        name="ada_mod",
    )(c_pad, w_ada, b_ada.reshape(L, 1, N))


def _normed(x, g, scale, shift):
    y = x * lax.rsqrt(jnp.mean(x * x, axis=-1, keepdims=True) + RMS_EPS)
    return (y * g) * (1.0 + scale) + shift


def _norm_mod_kernel(x_ref, g_ref, sc_ref, sh_ref, h_ref):
    h_ref[...] = _normed(x_ref[...], g_ref[...], sc_ref[...], sh_ref[...]).astype(BF16)


def _pack_pairs(h):
    half = h.shape[1] // 2
    hi = lax.bitcast_convert_type(h[:, :half].astype(BF16).astype(F32), U32)
    lo = lax.bitcast_convert_type(h[:, half:].astype(BF16).astype(F32), U32)
    return hi | (lo >> 16)


def _unpack_pairs(w):
    a = lax.bitcast_convert_type(w & jnp.uint32(0xFFFF0000), F32).astype(BF16)
    b = lax.bitcast_convert_type(w << 16, F32).astype(BF16)
    return jnp.concatenate([a, b], axis=1)


def _norm_route_kernel(x_ref, g_ref, sc_ref, sh_ref, wr_ref, hp_ref, lg_ref):
    h = _normed(x_ref[...], g_ref[...], sc_ref[...], sh_ref[...])
    hp_ref[...] = _pack_pairs(h)
    lg_ref[...] = _dot(h, wr_ref[...], precision=HIGHEST)


def _norm_mod(x, g, scale, shift, S, w_router_pad=None):
    T, D = x.shape
    tm = 512
    nb = S // tm
    row = lambda i: (i, 0)
    per_batch = pl.BlockSpec((None, 1, D), lambda i: (i // nb, 0, 0))
    in_specs = [pl.BlockSpec((tm, D), row), pl.BlockSpec((1, D), lambda i: (0, 0)), per_batch, per_batch]
    if w_router_pad is None:
        return pl.pallas_call(
            _norm_mod_kernel, out_shape=jax.ShapeDtypeStruct((T, D), BF16), grid=(T // tm,),
            in_specs=in_specs, out_specs=pl.BlockSpec((tm, D), row),
            compiler_params=_cparams(1, 32), name="norm_mod",
        )(x, g, scale, shift)
    return pl.pallas_call(
        _norm_route_kernel,
        out_shape=(jax.ShapeDtypeStruct((T, D // 2), U32), jax.ShapeDtypeStruct((T, LANES), F32)),
        grid=(T // tm,),
        in_specs=in_specs + [pl.BlockSpec((D, LANES), lambda i: (0, 0))],
        out_specs=(pl.BlockSpec((tm, D // 2), row), pl.BlockSpec((tm, LANES), row)),
        compiler_params=_cparams(1, 32), name="norm_route",
    )(x, g, scale, shift, w_router_pad)


def _mm_kernel(*refs, cast_w, residual):
    if residual:
        a_ref, w_ref, x_ref, g_ref, o_ref = refs[:5]
        scratch = refs[5:]
    else:
        a_ref, w_ref, o_ref = refs[:3]
        scratch = refs[3:]
    if cast_w:
        wbf_ref, = scratch

        @pl.when(pl.program_id(1) == 0)
        def _():
            wbf_ref[...] = w_ref[...].astype(BF16)

        w = wbf_ref[...]
    else:
        w = w_ref[...]
    acc = _dot(a_ref[...], w)
    if residual:
        acc = x_ref[...] + g_ref[...] * acc
    o_ref[...] = acc.astype(o_ref.dtype)


def _matmul(a, w, tm, tn, out_dtype=F32, residual=None, S=None, vmem_mib=48, layer=None):
    M, K = a.shape
    N = w.shape[-1]
    cast_w = w.dtype != BF16
    if layer is None:
        w_spec = pl.BlockSpec((K, tn), lambda j, i: (0, j))
    else:
        w_spec = pl.BlockSpec((None, K, tn), lambda j, i: (layer, 0, j))
    in_specs = [pl.BlockSpec((tm, K), lambda j, i: (i, 0)), w_spec]
    args = [a, w]
    if residual is not None:
        x, gate = residual
        nb = S // tm
        in_specs += [pl.BlockSpec((tm, tn), lambda j, i: (i, j)),
                     pl.BlockSpec((None, 1, tn), lambda j, i: (i // nb, 0, j))]
        args += [x, gate]
    return pl.pallas_call(
        functools.partial(_mm_kernel, cast_w=cast_w, residual=residual is not None),
        out_shape=jax.ShapeDtypeStruct((M, N), out_dtype),
        grid=(N // tn, M // tm),
        in_specs=in_specs,
        out_specs=pl.BlockSpec((tm, tn), lambda j, i: (i, j)),
        scratch_shapes=[pltpu.VMEM((K, tn), BF16)] if cast_w else [],
        compiler_params=_cparams(2, vmem_mib),
        name="matmul_res" if residual is not None else "matmul",
    )(*args)


IN_TN = 1024
IN_SEGMENTS = ((0, COL_MOBA * LANES // IN_TN, 0),
               (COL_MOBA * LANES // IN_TN, COL_GATES * LANES // IN_TN, N_HEADS),
               (COL_GATES * LANES // IN_TN, N_MAIN_COLS // IN_TN, N_HEADS + GLA_GATE_RANK))
ROW_CHUNK = 256


def _in_proj_kernel(a_ref, w_ref, wn_ref, o_ref, wbf_ref, *, tn):
    j = pl.program_id(0)

    @pl.when(pl.program_id(1) == 0)
    def _():
        K = w_ref.shape[0]
        for lo, hi, shift in IN_SEGMENTS:
            @pl.when(jnp.logical_and(j >= lo, j < hi))
            def _():
                for r in range(0, K, ROW_CHUNK):
                    rows = slice(r, r + ROW_CHUNK)
                    if shift == 0:
                        wbf_ref[rows, :] = w_ref[rows, :].astype(BF16)
                    else:
                        wide = jnp.concatenate([w_ref[rows, :], wn_ref[rows, :]], axis=1)
                        wbf_ref[rows, :] = wide[:, shift:shift + tn].astype(BF16)

    o_ref[...] = _dot(a_ref[...], wbf_ref[...])


def _in_proj(a, w_in, layer, tm=512):
    M, K = a.shape
    tn = IN_TN
    return pl.pallas_call(
        functools.partial(_in_proj_kernel, tn=tn),
        out_shape=jax.ShapeDtypeStruct((M, N_MAIN_COLS), F32),
        grid=(N_MAIN_COLS // tn, M // tm),
        in_specs=[pl.BlockSpec((tm, K), lambda j, i: (i, 0)),
                  pl.BlockSpec((None, K, tn), lambda j, i: (layer, 0, j)),
                  pl.BlockSpec((None, K, LANES), lambda j, i: (layer, 0, (j + 1) * (tn // LANES)))],
        out_specs=pl.BlockSpec((tm, tn), lambda j, i: (i, j)),
        scratch_shapes=[pltpu.VMEM((K, tn), BF16)],
        compiler_params=_cparams(2, 48),
        name="in_proj",
    )(a, w_in, w_in)


def _misc_proj_kernel(a_ref, wf_ref, wg_ref, o_ref, wbf_ref):
    @pl.when(pl.program_id(0) == 0)
    def _():
        lane = lax.broadcasted_iota(I32, wf_ref.shape, 1)
        w = jnp.where(lane < MISC_GLR, wf_ref[...],
                      jnp.where(lane < MISC_GLR + GLA_GATE_RANK, wg_ref[...], 0.0))
        wbf_ref[...] = w.astype(BF16)

    o_ref[...] = _dot(a_ref[...], wbf_ref[...])


def _misc_proj(a, w_in, layer, tm=1024):
    M, K = a.shape
    return pl.pallas_call(
        _misc_proj_kernel,
        out_shape=jax.ShapeDtypeStruct((M, LANES), F32),
        grid=(M // tm,),
        in_specs=[pl.BlockSpec((tm, K), lambda i: (i, 0)),
                  pl.BlockSpec((None, K, LANES), lambda i: (layer, 0, COL_MOBA)),
                  pl.BlockSpec((None, K, LANES), lambda i: (layer, 0, COL_GATES))],
        out_specs=pl.BlockSpec((tm, LANES), lambda i: (i, 0)),
        scratch_shapes=[pltpu.VMEM((K, LANES), BF16)],
        compiler_params=_cparams(1, 32),
        name="misc_proj",
    )(a, w_in, w_in)


SB_LOG_WEIGHT_FLOOR = -110.0


def _sb_kernel(q_ref, k_ref, v_ref, o_ref, *, blk, scale):
    i = pl.program_id(1)
    row = lax.broadcasted_iota(I32, (blk, blk), 0)
    col = lax.broadcasted_iota(I32, (blk, blk), 1)
    later = (row > col).astype(BF16)

    def cond(state):
        jj, alive, _ = state
        return jnp.logical_and(jj <= i, alive)

    def body(state):
        jj, _, heads = state
        j = i - jj
        start = pl.multiple_of(j * blk, blk)
        past = jnp.logical_or(j < i, col < row)
        new, top = [], None
        for h in range(N_HEADS):
            c, acc = heads[h]
            hs = slice(h * HEAD_DIM, (h + 1) * HEAD_DIM)
            q = (q_ref[:, hs] * scale).astype(BF16)
            k = k_ref[pl.ds(start, blk), hs].astype(BF16)
            v = v_ref[pl.ds(start, blk), hs].astype(BF16)
            z = _dot_nt(q, k)
            ls = jnp.where(past, -_softplus(z), 0.0)
            hi = ls.astype(BF16)
            lo = (ls - hi.astype(F32)).astype(BF16)
            between = _dot(hi, later) + _dot(lo, later)
            w = jnp.where(past, jnp.exp(z + ls + between + c), 0.0)
            acc = acc + _dot(w.astype(BF16), v)
            c = c + jnp.sum(ls, axis=1, keepdims=True)
            new.append((c, acc))
            top = c if top is None else jnp.maximum(top, c)
        return jj + 1, jnp.max(top) > SB_LOG_WEIGHT_FLOOR, tuple(new)

    init = tuple((jnp.zeros((blk, 1), F32), jnp.zeros((blk, HEAD_DIM), F32)) for _ in range(N_HEADS))
    _, _, heads = lax.while_loop(cond, body, (jnp.int32(0), jnp.bool_(True), init))
    for h in range(N_HEADS):
        o_ref[:, h * HEAD_DIM:(h + 1) * HEAD_DIM] = heads[h][1]


def _sb_attention(proj, B, S, col0):
    T = proj.shape[0]
    blk = ATT_BLOCK
    nq = S // blk
    W = BRANCH_WIDTH
    cb = col0 * LANES // W
    return pl.pallas_call(
        functools.partial(_sb_kernel, blk=blk, scale=HEAD_DIM ** -0.5),
        out_shape=jax.ShapeDtypeStruct((T, W), F32),
        grid=(B, nq),
        in_specs=[pl.BlockSpec((blk, W), lambda b, i: (b * nq + i, cb)),
                  pl.BlockSpec((S, W), lambda b, i: (b, cb + 1)),
                  pl.BlockSpec((S, W), lambda b, i: (b, cb + 2))],
        out_specs=pl.BlockSpec((blk, W), lambda b, i: (b * nq + i, 0)),
        compiler_params=_cparams(2, 44),
        name="sb_attention",
    )(proj, proj, proj)


def _flash_kernel(qa_ref, ka_ref, v_ref, o_ref, *, blk):
    i = pl.program_id(1)
    row = lax.broadcasted_iota(I32, (blk, blk), 0)
    col = lax.broadcasted_iota(I32, (blk, blk), 1)
    A = 2 * HEAD_DIM

    def step(j, carry):
        start = pl.multiple_of(j * blk, blk)
        keep = jnp.logical_or(j < i, col <= row)
        new = []
        for h in range(N_HEADS):
            m, l, acc = carry[h]
            s = _dot_nt(qa_ref[:, h * A:(h + 1) * A], ka_ref[pl.ds(start, blk), h * A:(h + 1) * A])
            s = jnp.where(keep, s, NEG_BIG)
            m_new = jnp.maximum(m, jnp.max(s, axis=1, keepdims=True))
            alpha = jnp.exp(m - m_new)
            p = jnp.exp(s - m_new)
            l = alpha * l + jnp.sum(p, axis=1, keepdims=True)
            acc = alpha * acc + _dot(p.astype(BF16), v_ref[pl.ds(start, blk), h * HEAD_DIM:(h + 1) * HEAD_DIM])
            new.append((m_new, l, acc))
        return tuple(new)

    init = tuple((jnp.full((blk, 1), NEG_BIG, F32), jnp.zeros((blk, 1), F32),
                  jnp.zeros((blk, HEAD_DIM), F32)) for _ in range(N_HEADS))
    heads = lax.fori_loop(0, i + 1, step, init)
    for h in range(N_HEADS):
        _, l, acc = heads[h]
        o_ref[:, h * HEAD_DIM:(h + 1) * HEAD_DIM] = acc / l


def _flash_attention(qa, ka, va, B, S):
    T = qa.shape[0]
    blk = ATT_BLOCK
    nq = S // blk
    W = BRANCH_WIDTH
    return pl.pallas_call(
        functools.partial(_flash_kernel, blk=blk),
        out_shape=jax.ShapeDtypeStruct((T, W), F32),
        grid=(B, nq),
        in_specs=[pl.BlockSpec((blk, 2 * W), lambda b, i: (b * nq + i, 0)),
                  pl.BlockSpec((S, 2 * W), lambda b, i: (b, 0)),
                  pl.BlockSpec((S, W), lambda b, i: (b, 0))],
        out_specs=pl.BlockSpec((blk, W), lambda b, i: (b * nq + i, 0)),
        compiler_params=_cparams(2, 40),
        name="flash_attention",
    )(qa, ka, va)


def _fox_prep_kernel(q_ref, k_ref, v_ref, misc_ref, bias_ref, qa_ref, ka_ref, va_ref, carry_ref, *, tb, scale):
    va_ref[...] = v_ref[...].astype(BF16)
    @pl.when(pl.program_id(1) == 0)
    def _():
        carry_ref[...] = jnp.zeros_like(carry_ref)

    lane = lax.broadcasted_iota(I32, (tb, LANES), 1)
    lf = jnp.where(lane < N_HEADS, _log_sigmoid(misc_ref[...] + bias_ref[...]), 0.0)
    row = lax.broadcasted_iota(I32, (tb, tb), 0)
    col = lax.broadcasted_iota(I32, (tb, tb), 1)
    incl = (col <= row).astype(F32)
    F = _dot(incl, lf, precision=HIGHEST) + carry_ref[0:1, :]
    carry_ref[0:1, :] = F[tb - 1:tb, :]
    for h in range(N_HEADS):
        Fh = jnp.broadcast_to(F[:, h:h + 1], (tb, LANES))
        hi, mid, lo = (p.astype(F32) for p in _split3(Fh))
        ones = jnp.ones((tb, LANES), F32)
        zeros = jnp.zeros((tb, LANES), F32)
        eq = jnp.where(lane == 0, hi, jnp.where(lane == 1, mid, jnp.where(lane == 2, lo,
             jnp.where(lane < 6, ones, zeros))))
        ek = jnp.where(lane < 3, ones, jnp.where(lane == 3, -hi, jnp.where(lane == 4, -mid,
             jnp.where(lane == 5, -lo, zeros))))
        hs = slice(h * HEAD_DIM, (h + 1) * HEAD_DIM)
        qa_ref[:, 2 * h * HEAD_DIM:(2 * h + 1) * HEAD_DIM] = (q_ref[:, hs] * scale).astype(BF16)
        qa_ref[:, (2 * h + 1) * HEAD_DIM:(2 * h + 2) * HEAD_DIM] = eq.astype(BF16)
        ka_ref[:, 2 * h * HEAD_DIM:(2 * h + 1) * HEAD_DIM] = k_ref[:, hs].astype(BF16)
        ka_ref[:, (2 * h + 1) * HEAD_DIM:(2 * h + 2) * HEAD_DIM] = ek.astype(BF16)


def _fox_prep(proj, misc, fox_bias_pad, B, S):
    T = proj.shape[0]
    tb = 512
    nb = S // tb
    qblk = COL_FOX * LANES // BRANCH_WIDTH
    out = jax.ShapeDtypeStruct((T, 2 * BRANCH_WIDTH), BF16)
    return pl.pallas_call(
        functools.partial(_fox_prep_kernel, tb=tb, scale=HEAD_DIM ** -0.5),
        out_shape=(out, out, jax.ShapeDtypeStruct((T, BRANCH_WIDTH), BF16)),
        grid=(B, nb),
        in_specs=[pl.BlockSpec((tb, BRANCH_WIDTH), lambda b, i: (b * nb + i, qblk)),
                  pl.BlockSpec((tb, BRANCH_WIDTH), lambda b, i: (b * nb + i, qblk + 1)),
                  pl.BlockSpec((tb, BRANCH_WIDTH), lambda b, i: (b * nb + i, qblk + 2)),
                  pl.BlockSpec((tb, LANES), lambda b, i: (b * nb + i, 0)),
                  pl.BlockSpec((1, LANES), lambda b, i: (0, 0))],
        out_specs=(pl.BlockSpec((tb, 2 * BRANCH_WIDTH), lambda b, i: (b * nb + i, 0)),
                   pl.BlockSpec((tb, 2 * BRANCH_WIDTH), lambda b, i: (b * nb + i, 0)),
                   pl.BlockSpec((tb, BRANCH_WIDTH), lambda b, i: (b * nb + i, 0))),
        scratch_shapes=[pltpu.VMEM((SUBLANES, LANES), F32)],
        compiler_params=_cparams(2, 32),
        name="fox_prep",
    )(proj, proj, proj, misc, fox_bias_pad)


def _rope(x, cos, sin, lane):
    half = ROPE_DIM // 2
    up = pltpu.roll(x, half, 1)
    down = pltpu.roll(x, LANES - half, 1)
    rot = jnp.where(lane < half, -down * sin, jnp.where(lane < ROPE_DIM, up * sin, 0.0))
    return x * jnp.where(lane < ROPE_DIM, cos, 1.0) + rot


def _angles(pos_ref, freq_ref):
    ang = pos_ref[...].astype(F32) * freq_ref[...]
    return jnp.cos(ang), jnp.sin(ang)


def _moba_k_kernel(k_ref, v_ref, pos_ref, freq_ref, ka_ref, va_ref, kmean_ref, *, tb):
    i = pl.program_id(1)
    va_ref[...] = v_ref[...].astype(BF16)

    @pl.when(i == 0)
    def _():
        kmean_ref[...] = jnp.zeros_like(kmean_ref)

    lane = lax.broadcasted_iota(I32, (tb, LANES), 1)
    cos, sin = _angles(pos_ref, freq_ref)
    onehot = jnp.where(lane == i, 1.0, 0.0).astype(BF16)
    this_row = lax.broadcasted_iota(I32, (LANES, HEAD_DIM), 0) == i
    for h in range(N_HEADS):
        hs = slice(h * HEAD_DIM, (h + 1) * HEAD_DIM)
        kr = _rope(k_ref[:, hs], cos, sin, lane)
        mean = jnp.sum(kr, axis=0, keepdims=True) * (1.0 / tb)
        kmean_ref[:, hs] = jnp.where(this_row, mean, kmean_ref[:, hs])
        ka_ref[:, 2 * h * HEAD_DIM:(2 * h + 1) * HEAD_DIM] = kr.astype(BF16)
        ka_ref[:, (2 * h + 1) * HEAD_DIM:(2 * h + 2) * HEAD_DIM] = onehot


def _moba_q_kernel(q_ref, pos_ref, freq_ref, kmean_ref, qa_ref, *, tb, scale):
    i = pl.program_id(1)
    lane = lax.broadcasted_iota(I32, (tb, LANES), 1)
    cos, sin = _angles(pos_ref, freq_ref)
    neg_inf = jnp.float32(-jnp.inf)
    for h in range(N_HEADS):
        hs = slice(h * HEAD_DIM, (h + 1) * HEAD_DIM)
        qr = _rope(q_ref[:, hs], cos, sin, lane)
        gate = _dot_nt(qr, kmean_ref[:, hs], precision=HIGHEST)
        cur = jnp.where(lane < i, gate, neg_inf)
        chosen = lane == i
        for _ in range(MOBA_TOPK):
            m = jnp.max(cur, axis=1, keepdims=True)
            first = jnp.min(jnp.where(jnp.logical_and(cur == m, m > neg_inf), lane, LANES),
                            axis=1, keepdims=True)
            pick = lane == first
            chosen = jnp.logical_or(chosen, pick)
            cur = jnp.where(pick, neg_inf, cur)
        bias = jnp.where(jnp.logical_or(chosen, lane >= LANES // 2), 0.0, NEG_BIG)
        qa_ref[:, 2 * h * HEAD_DIM:(2 * h + 1) * HEAD_DIM] = (qr * scale).astype(BF16)
        qa_ref[:, (2 * h + 1) * HEAD_DIM:(2 * h + 2) * HEAD_DIM] = bias.astype(BF16)


def _moba_prep(proj, pos_col, freq_lanes, B, S):
    T = proj.shape[0]
    tb = MOBA_BLOCK
    nb = S // tb
    qblk = COL_MOBA * LANES // BRANCH_WIDTH
    aug = jax.ShapeDtypeStruct((T, 2 * BRANCH_WIDTH), BF16)
    row_spec = lambda c: pl.BlockSpec((tb, BRANCH_WIDTH), lambda b, i: (b * nb + i, c))
    pos_spec = pl.BlockSpec((tb, 1), lambda b, i: (b * nb + i, 0))
    freq_spec = pl.BlockSpec((1, LANES), lambda b, i: (0, 0))
    aug_spec = pl.BlockSpec((tb, 2 * BRANCH_WIDTH), lambda b, i: (b * nb + i, 0))
    kmean_spec = pl.BlockSpec((None, LANES, BRANCH_WIDTH), lambda b, i: (b, 0, 0))
    ka, va, kmean = pl.pallas_call(
        functools.partial(_moba_k_kernel, tb=tb),
        out_shape=(aug, jax.ShapeDtypeStruct((T, BRANCH_WIDTH), BF16),
                   jax.ShapeDtypeStruct((B, LANES, BRANCH_WIDTH), F32)),
        grid=(B, nb),
        in_specs=[row_spec(qblk + 1), row_spec(qblk + 2), pos_spec, freq_spec],
        out_specs=(aug_spec, pl.BlockSpec((tb, BRANCH_WIDTH), lambda b, i: (b * nb + i, 0)), kmean_spec),
        compiler_params=_cparams(2, 32),
        name="moba_k_prep",
    )(proj, proj, pos_col, freq_lanes)
    qa = pl.pallas_call(
        functools.partial(_moba_q_kernel, tb=tb, scale=HEAD_DIM ** -0.5),
        out_shape=aug,
        grid=(B, nb),
        in_specs=[row_spec(qblk), pos_spec, freq_spec, kmean_spec],
        out_specs=aug_spec,
        compiler_params=_cparams(2, 32),
        name="moba_q_prep",
    )(proj, pos_col, freq_lanes, kmean)
    return qa, ka, va


def _gla_kernel(q_ref, k_ref, v_ref, gr_ref, misc_ref, wg_ref, bg_ref, gn_ref, o_ref, state_ref, *, tb):
    C = GLA_CHUNK
    @pl.when(pl.program_id(1) == 0)
    def _():
        state_ref[...] = jnp.zeros_like(state_ref)

    la_all = _log_sigmoid(_dot(misc_ref[...], wg_ref[...], precision=HIGHEST) + bg_ref[...]) \
        * (1.0 / GLA_GATE_NORMALIZER)
    rowc = lax.broadcasted_iota(I32, (C, C), 0)
    colc = lax.broadcasted_iota(I32, (C, C), 1)
    incl = (colc <= rowc).astype(F32)
    causal = colc <= rowc
    lane = lax.broadcasted_iota(I32, (C, LANES), 1)
    row2 = lax.broadcasted_iota(I32, (2 * GLA_HEAD_K, LANES), 0)
    ones_cv = jnp.ones((C, LANES), F32)
    qscale = GLA_HEAD_K ** -0.5
    gn = gn_ref[...]
    for c in range(tb // C):
        rs = slice(c * C, (c + 1) * C)
        for p in range(2):
            ps = slice(p * LANES, (p + 1) * LANES)
            la = la_all[rs, ps]
            b = _dot(incl, la, precision=HIGHEST)
            b_last = b[C - 1:C, :]
            eb = jnp.exp(b)
            q_dec = q_ref[rs, ps] * qscale * eb
            k = k_ref[rs, ps]
            k_inv = (k * jnp.exp(-b)).astype(BF16)
            k_end = (k * jnp.exp(b_last - b)).astype(BF16)
            decay = jnp.exp(_dot_tn(la, ones_cv, precision=HIGHEST))
            st = state_ref[p * LANES:(p + 1) * LANES, :]
            new_rows = []
            for e in range(2):
                h = 2 * p + e
                own = jnp.logical_and(lane >= e * GLA_HEAD_K, lane < (e + 1) * GLA_HEAD_K)
                qh = jnp.where(own, q_dec, 0.0).astype(BF16)
                v = v_ref[rs, h * HEAD_DIM:(h + 1) * HEAD_DIM].astype(BF16)
                attn = jnp.where(causal, _dot_nt(qh, k_inv), 0.0)
                st_h = jnp.where(jnp.logical_and(row2 >= e * GLA_HEAD_K, row2 < (e + 1) * GLA_HEAD_K), st, 0.0)
                o = _dot(attn.astype(BF16), v) + _dot(qh, st_h.astype(BF16))
                y = o * lax.rsqrt(jnp.mean(o * o, axis=-1, keepdims=True) + RMS_EPS) * gn
                g = gr_ref[rs, h * HEAD_DIM:(h + 1) * HEAD_DIM]
                o_ref[rs, h * HEAD_DIM:(h + 1) * HEAD_DIM] = y * (g * _sigmoid(g))
                new_rows.append(_dot_tn(k_end, v))
            kv = jnp.where(row2 < GLA_HEAD_K, new_rows[0], new_rows[1])
            state_ref[p * LANES:(p + 1) * LANES, :] = decay * st + kv


def _gla(proj, misc, w_gate_pad, b_gate, gla_norm, B, S):
    T = proj.shape[0]
    tb = 512
    nb = S // tb
    kd = 2 * LANES
    rows = lambda w, c: pl.BlockSpec((tb, w), lambda b, i: (b * nb + i, c))
    const = lambda shape: pl.BlockSpec(shape, lambda b, i: (0, 0))
    return pl.pallas_call(
        functools.partial(_gla_kernel, tb=tb),
        out_shape=jax.ShapeDtypeStruct((T, BRANCH_WIDTH), F32),
        grid=(B, nb),
        in_specs=[rows(kd, COL_GQ * LANES // kd), rows(kd, COL_GK * LANES // kd),
                  rows(BRANCH_WIDTH, COL_GV * LANES // BRANCH_WIDTH),
                  rows(BRANCH_WIDTH, COL_GR * LANES // BRANCH_WIDTH),
                  rows(LANES, 0), const((LANES, kd)), const((1, kd)), const((1, HEAD_DIM))],
        out_specs=rows(BRANCH_WIDTH, 0),
        scratch_shapes=[pltpu.VMEM((kd, HEAD_DIM), F32)],
        compiler_params=_cparams(2, 32),
        name="gla",
    )(proj, proj, proj, proj, misc, w_gate_pad, b_gate, gla_norm)


def _merge_kernel(oa_ref, ob_ref, oc_ref, od_ref, g0_ref, g1_ref, g2_ref, g3_ref, w_ref, o_ref, wbf_ref):
    @pl.when(pl.program_id(1) == 0)
    def _():
        wbf_ref[...] = w_ref[...].astype(BF16)

    acc = None
    for n, (b_ref, g_ref) in enumerate(((oa_ref, g0_ref), (ob_ref, g1_ref), (oc_ref, g2_ref), (od_ref, g3_ref))):
        term = _sigmoid(g_ref[...]) * _dot(b_ref[...].astype(BF16), wbf_ref[n])
        acc = term if acc is None else acc + term
    o_ref[...] = acc.astype(BF16)


def _merge(branches, proj, w_branch, layer):
    T = proj.shape[0]
    _, nbr, W, D = w_branch.shape
    tm, tn = 512, 512
    g0 = COL_GATES * LANES // tn
    per = D // tn
    br_spec = pl.BlockSpec((tm, W), lambda j, i: (i, 0))
    gate_spec = lambda n: pl.BlockSpec((tm, tn), lambda j, i: (i, g0 + n * per + j))
    return pl.pallas_call(
        _merge_kernel,
        out_shape=jax.ShapeDtypeStruct((T, D), BF16),
        grid=(D // tn, T // tm),
        in_specs=[br_spec] * 4 + [gate_spec(n) for n in range(4)]
                 + [pl.BlockSpec((None, nbr, W, tn), lambda j, i: (layer, 0, 0, j))],
        out_specs=pl.BlockSpec((tm, tn), lambda j, i: (i, j)),
        scratch_shapes=[pltpu.VMEM((nbr, W, tn), BF16)],
        compiler_params=_cparams(2, 40),
        name="branch_merge",
    )(*branches, proj, proj, proj, proj, w_branch)


def _butterfly(x, lane, op):
    for s in (1, 2, 4):
        up = pltpu.roll(x, s, 1)
        down = pltpu.roll(x, LANES - s, 1)
        x = op(x, jnp.where((lane & s) != 0, up, down))
    return x


def _route_kernel(lg_ref, bias_ref, e8_ref, w8_ref, p8_ref, cnt_ref, carry_ref, *, tm):
    @pl.when(pl.program_id(0) == 0)
    def _():
        carry_ref[...] = jnp.zeros_like(carry_ref)

    neg_inf = jnp.float32(-jnp.inf)
    lane = lax.broadcasted_iota(I32, (tm, LANES), 1)
    valid = lane < N_EXPERTS
    scores = _sigmoid(lg_ref[...])
    biased = jnp.where(valid, scores + bias_ref[...], neg_inf)
    g1 = _butterfly(biased, lane, jnp.maximum)
    first = _butterfly(jnp.where(biased == g1, lane, LANES), lane, jnp.minimum)
    g2 = _butterfly(jnp.where(lane == first, neg_inf, biased), lane, jnp.maximum)
    gs = g1 + g2
    gs = jnp.where(valid, gs, pltpu.roll(gs, N_EXPERTS, 1))
    gidx = lane >> 3
    beaten = jnp.zeros((tm, LANES), I32)
    for r in range(1, N_EXPERTS // GROUP_SIZE):
        other = pltpu.roll(gs, GROUP_SIZE * r, 1)
        og = (gidx - r) & (N_EXPERTS // GROUP_SIZE - 1)
        wins = jnp.logical_or(other > gs, jnp.logical_and(other == gs, og < gidx))
        beaten = beaten + wins.astype(I32)
    cur = jnp.where(jnp.logical_and(beaten < TOPK_GROUPS, valid), biased, neg_inf)
    sel = jnp.zeros((tm, LANES), jnp.bool_)
    for _ in range(TOP_K):
        m = jnp.max(cur, axis=1, keepdims=True)
        pick = lane == jnp.min(jnp.where(cur == m, lane, LANES), axis=1, keepdims=True)
        sel = jnp.logical_or(sel, pick)
        cur = jnp.where(pick, neg_inf, cur)
    wsel = jnp.where(sel, scores, 0.0)
    wd = wsel / jnp.sum(wsel, axis=1, keepdims=True) * ROUTED_SCALE
    selb = jnp.where(sel, 1.0, 0.0).astype(BF16)
    row = lax.broadcasted_iota(I32, (tm, tm), 0)
    col = lax.broadcasted_iota(I32, (tm, tm), 1)
    pos = _dot((col < row).astype(BF16), selb) + carry_ref[0:1, :]
    total = carry_ref[0:1, :] + jnp.sum(selb.astype(F32), axis=0, keepdims=True)
    carry_ref[0:1, :] = total
    cnt_ref[...] = jnp.broadcast_to(total, cnt_ref.shape)
    r2 = lax.broadcasted_iota(I32, (LANES, LANES), 0)
    c2 = lax.broadcasted_iota(I32, (LANES, LANES), 1)
    slot = _dot(selb, (r2 < c2).astype(BF16))
    lane_f = lane.astype(F32)
    e8 = jnp.zeros((tm, LANES), F32)
    w8 = jnp.zeros((tm, LANES), F32)
    p8 = jnp.zeros((tm, LANES), F32)
    for k in range(TOP_K):
        mk = jnp.logical_and(sel, slot == k)
        put = lane == k
        e8 = jnp.where(put, jnp.sum(jnp.where(mk, lane_f, 0.0), axis=1, keepdims=True), e8)
        w8 = jnp.where(put, jnp.sum(jnp.where(mk, wd, 0.0), axis=1, keepdims=True), w8)
        p8 = jnp.where(put, jnp.sum(jnp.where(mk, pos, 0.0), axis=1, keepdims=True), p8)
    e8_ref[...] = e8.astype(I32)
    w8_ref[...] = w8
    p8_ref[...] = p8.astype(I32)


def _route(logits, router_bias_pad):
    T = logits.shape[0]
    tm = 512
    row = pl.BlockSpec((tm, LANES), lambda i: (i, 0))
    return pl.pallas_call(
        functools.partial(_route_kernel, tm=tm),
        out_shape=(jax.ShapeDtypeStruct((T, LANES), I32), jax.ShapeDtypeStruct((T, LANES), F32),
                   jax.ShapeDtypeStruct((T, LANES), I32), jax.ShapeDtypeStruct((SUBLANES, LANES), F32)),
        grid=(T // tm,),
        in_specs=[row, pl.BlockSpec((1, LANES), lambda i: (0, 0))],
        out_specs=(row, row, row, pl.BlockSpec((SUBLANES, LANES), lambda i: (0, 0))),
        scratch_shapes=[pltpu.VMEM((SUBLANES, LANES), F32)],
        compiler_params=_cparams(1, 32),
        name="route",
    )(logits, router_bias_pad)


def _row_copy(src, src_row, dst, dst_row, sem):
    return pltpu.make_async_copy(src.at[pl.ds(src_row, 1), :], dst.at[pl.ds(dst_row, 1), :], sem)


def _dispatch_kernel(pad_end_ref, padded_ref, dest_ref, hp_ref, xs_ref, zero_ref, sem_ref, *, tm, rows, n_blocks):
    @pl.when(pl.program_id(0) == 0)
    def _():
        zero_ref[...] = jnp.zeros_like(zero_ref)

        def fill(e, do_wait):
            @pl.when(padded_ref[e] > 0)
            def _():
                start = pl.multiple_of(pad_end_ref[e] - rows, rows)
                cp = pltpu.make_async_copy(zero_ref, xs_ref.at[pl.ds(start, rows), :], sem_ref)
                if do_wait:
                    cp.wait()
                else:
                    cp.start()

        lax.fori_loop(0, N_EXPERTS, lambda e, c: (fill(e, False), c)[1], 0)
        lax.fori_loop(0, N_EXPERTS, lambda e, c: (fill(e, True), c)[1], 0)

        def tail(b):
            return pltpu.make_async_copy(zero_ref, xs_ref.at[pl.ds(pl.multiple_of(b * rows, rows), rows), :], sem_ref)

        n_used = pad_end_ref[N_EXPERTS - 1] // rows
        lax.fori_loop(n_used, n_blocks, lambda b, c: (tail(b).start(), c)[1], 0)
        lax.fori_loop(n_used, n_blocks, lambda b, c: (tail(b).wait(), c)[1], 0)

    def issue(t, c):
        for k in range(TOP_K):
            _row_copy(hp_ref, t, xs_ref, dest_ref[t * TOP_K + k], sem_ref).start()
        return c

    def drain(t, c):
        for k in range(TOP_K):
            _row_copy(hp_ref, t, xs_ref, dest_ref[t * TOP_K + k], sem_ref).wait()
        return c

    lax.fori_loop(0, tm, issue, 0)
    lax.fori_loop(0, tm, drain, 0)


def _dispatch(hp, dest_flat, pad_end, padded, n_rows):
    T, W = hp.shape
    tm = TOKEN_TILE
    grid_spec = pltpu.PrefetchScalarGridSpec(
        num_scalar_prefetch=2,
        grid=(T // tm,),
        in_specs=[pl.BlockSpec((tm * TOP_K,), lambda i, pe, pd: (i,), memory_space=pltpu.SMEM),
                  pl.BlockSpec((tm, W), lambda i, pe, pd: (i, 0))],
        out_specs=pl.BlockSpec(memory_space=pl.ANY),
        scratch_shapes=[pltpu.VMEM((EXPERT_ROWS, W), U32), pltpu.SemaphoreType.DMA(())],
    )
    return pl.pallas_call(
        functools.partial(_dispatch_kernel, tm=tm, rows=EXPERT_ROWS, n_blocks=n_rows // EXPERT_ROWS),
        out_shape=jax.ShapeDtypeStruct((n_rows, W), U32),
        grid_spec=grid_spec,
        compiler_params=_cparams(1, 32),
        name="dispatch",
    )(pad_end, padded, dest_flat, hp)


def _expert_kernel(blk_e_ref, n_used_ref, x_ref, wg_ref, wu_ref, wd_ref, y_ref, wg_bf, wu_bf, wd_bf):
    i = pl.program_id(0)
    prev = blk_e_ref[jnp.maximum(i - 1, 0)]
    active = i < n_used_ref[0]

    @pl.when(jnp.logical_and(active, jnp.logical_or(i == 0, blk_e_ref[i] != prev)))
    def _():
        wg_bf[...] = wg_ref[...].astype(BF16)
        wu_bf[...] = wu_ref[...].astype(BF16)
        wd_bf[...] = wd_ref[...].astype(BF16)

    @pl.when(active)
    def _():
        x = _unpack_pairs(x_ref[...])
        g = _dot(x, wg_bf[...])
        u = _dot(x, wu_bf[...])
        hb = (g * _sigmoid(g)) * u
        y_ref[...] = _dot(hb.astype(BF16), wd_bf[...])

    @pl.when(jnp.logical_not(active))
    def _():
        y_ref[...] = jnp.zeros_like(y_ref)


def _experts(xs, blk_e, n_used, w_gate, w_up, w_down, layer):
    P, W = xs.shape
    _, E, D, Hx = w_gate.shape
    M = EXPERT_ROWS
    nblk = P // M
    xrow = lambda i, be, nu: (jnp.minimum(i, nu[0] - 1), 0)
    wsel = lambda i, be, nu: (layer, be[i], 0, 0)
    grid_spec = pltpu.PrefetchScalarGridSpec(
        num_scalar_prefetch=2,
        grid=(nblk,),
        in_specs=[pl.BlockSpec((M, W), xrow),
                  pl.BlockSpec((None, None, D, Hx), wsel), pl.BlockSpec((None, None, D, Hx), wsel),
                  pl.BlockSpec((None, None, Hx, D), wsel)],
        out_specs=pl.BlockSpec((M, D), lambda i, be, nu: (i, 0)),
        scratch_shapes=[pltpu.VMEM((D, Hx), BF16), pltpu.VMEM((D, Hx), BF16), pltpu.VMEM((Hx, D), BF16)],
    )
    return pl.pallas_call(
        _expert_kernel,
        out_shape=jax.ShapeDtypeStruct((P, D), F32),
        grid_spec=grid_spec,
        compiler_params=_cparams(1, 52),
        name="experts",
    )(blk_e, n_used, xs, w_gate, w_up, w_down)


def _combine_kernel(dest_ref, x_ref, ysh_ref, w8_ref, g2_ref, fn_ref, ys_ref, o_ref, buf_ref, sem_ref,
                    *, tm, final_norm):
    def issue(t, c):
        for k in range(TOP_K):
            _row_copy(ys_ref, dest_ref[t * TOP_K + k], buf_ref, k * tm + t, sem_ref).start()
        return c

    def drain(t, c):
        for k in range(TOP_K):
            _row_copy(ys_ref, dest_ref[t * TOP_K + k], buf_ref, k * tm + t, sem_ref).wait()
        return c

    lax.fori_loop(0, tm, issue, 0)
    lax.fori_loop(0, tm, drain, 0)
    w8 = w8_ref[...]
    y = ysh_ref[...]
    for k in range(TOP_K):
        y = y + w8[:, k:k + 1] * buf_ref[k * tm:(k + 1) * tm, :]
    out = x_ref[...] + g2_ref[...] * y
    if final_norm:
        out = out * lax.rsqrt(jnp.mean(out * out, axis=-1, keepdims=True) + RMS_EPS) * fn_ref[...]
    o_ref[...] = out


def _combine(x, ysh, ys, dest_flat, w8, gate2, final_g, S, final_norm):
    T, D = x.shape
    tm = TOKEN_TILE
    nb = S // tm
    row = lambda i: (i, 0)
    return pl.pallas_call(
        functools.partial(_combine_kernel, tm=tm, final_norm=final_norm),
        out_shape=jax.ShapeDtypeStruct((T, D), F32),
        grid=(T // tm,),
        in_specs=[pl.BlockSpec((tm * TOP_K,), lambda i: (i,), memory_space=pltpu.SMEM),
                  pl.BlockSpec((tm, D), row), pl.BlockSpec((tm, D), row),
                  pl.BlockSpec((tm, LANES), row),
                  pl.BlockSpec((None, 1, D), lambda i: (i // nb, 0, 0)),
                  pl.BlockSpec((1, D), lambda i: (0, 0)),
                  pl.BlockSpec(memory_space=pl.ANY)],
        out_specs=pl.BlockSpec((tm, D), row),
        scratch_shapes=[pltpu.VMEM((TOP_K * tm, D), F32), pltpu.SemaphoreType.DMA(())],
        compiler_params=_cparams(1, 32),
        name="combine",
    )(dest_flat, x, ysh, w8, gate2, final_g, ys)


def _pad_lanes(v, offset=0):
    out = jnp.zeros((1, LANES), F32)
    return out.at[0, offset:offset + v.shape[0]].set(v.astype(F32))


def kernel(x, c, positions, attn_norm, w_ada, b_ada, w_in, fox_bias, gla_w_gate, gla_b_gate, gla_norm,
           w_branch, w_out, ffn_norm, w_router, router_bias, w_exp_gate, w_exp_up, w_exp_down,
           w_sh_gate, w_sh_up, w_sh_down, final_norm):
    B, S, D = x.shape
    L = w_ada.shape[0]
    T = B * S
    E = N_EXPERTS
    M = EXPERT_ROWS
    xf = x.reshape(T, D)

    c_pad = jnp.zeros((SUBLANES, D), F32).at[:B].set(c)
    mod = _ada_mod(c_pad, w_ada, b_ada)

    half = ROPE_DIM // 2
    inv_freq = jnp.power(ROPE_THETA, -jnp.arange(half, dtype=F32) * 2.0 / ROPE_DIM)
    freq_lanes = _pad_lanes(jnp.concatenate([inv_freq, inv_freq]))
    pos_col = positions.reshape(T, 1)

    n_blocks = (T * TOP_K + E * (M - 1) + M - 1) // M
    P = n_blocks * M

    for l in range(L):
        m6 = mod[l, :B].reshape(B, 6, 1, D)
        shift1, scale1, gate1, shift2, scale2, gate2 = (m6[:, n] for n in range(6))

        h1 = _norm_mod(xf, attn_norm[l].reshape(1, D), scale1, shift1, S)
        proj = _in_proj(h1, w_in, l)
        misc = _misc_proj(h1, w_in, l)

        o_a = _sb_attention(proj, B, S, COL_SB)
        qa, ka, va = _fox_prep(proj, misc, _pad_lanes(fox_bias[l], MISC_FOX_F), B, S)
        o_b = _flash_attention(qa, ka, va, B, S)
        qa, ka, va = _moba_prep(proj, pos_col, freq_lanes, B, S)
        o_c = _flash_attention(qa, ka, va, B, S)
        wg_pad = jnp.zeros((LANES, gla_w_gate.shape[2]), F32).at[MISC_GLR:MISC_GLR + GLA_GATE_RANK].set(gla_w_gate[l])
        o_d = _gla(proj, misc, wg_pad, gla_b_gate[l].reshape(1, -1), gla_norm[l].reshape(1, -1), B, S)

        merged = _merge((o_a, o_b, o_c, o_d), proj, w_branch, l)
        xf = _matmul(merged, w_out, tm=512, tn=1024, residual=(xf, gate1), S=S, layer=l)

        wr_pad = jnp.zeros((D, LANES), F32).at[:, :E].set(w_router[l])
        hp, logits = _norm_mod(xf, ffn_norm[l].reshape(1, D), scale2, shift2, S, w_router_pad=wr_pad)
        e8, w8, p8, counts = _route(logits, _pad_lanes(router_bias[l]))
        cnt = counts[0, :E].astype(I32)
        padded = (cnt + M - 1) // M * M
        pad_end = jnp.cumsum(padded)
        pad_start = pad_end - padded
        dest = (pad_start[e8[:, :TOP_K]] + p8[:, :TOP_K]).reshape(T * TOP_K)
        blk_start = jnp.arange(n_blocks, dtype=I32) * M
        blk_e = jnp.minimum(jnp.sum((pad_end[None, :] <= blk_start[:, None]).astype(I32), axis=1), E - 1)
        n_used = (pad_end[E - 1:] // M).astype(I32)

        xs = _dispatch(hp, dest, pad_end.astype(I32), padded.astype(I32), P)
        ys = _experts(xs, blk_e, n_used, w_exp_gate, w_exp_up, w_exp_down, l)
        ysh = _experts(hp, jnp.zeros((T // M,), I32), jnp.full((1,), T // M, I32),
                       w_sh_gate[:, None], w_sh_up[:, None], w_sh_down[:, None], l)
        last = l == L - 1
        xf = _combine(xf, ysh, ys, dest, w8, gate2, final_norm.reshape(1, D), S, final_norm=last)

    return xf.reshape(B, S, D)
```

```python
import functools

import jax
import jax.numpy as jnp
from jax import lax
from jax.experimental import pallas as pl
from jax.experimental.pallas import tpu as pltpu

F32 = jnp.float32
BF16 = jnp.bfloat16
I32 = jnp.int32
U32 = jnp.uint32

HEAD_DIM = 128
N_HEADS = 4
BRANCH_WIDTH = N_HEADS * HEAD_DIM
GLA_HEAD_K = 64
GLA_CHUNK = 64
GLA_GATE_RANK = 16
GLA_GATE_NORMALIZER = 16.0
MOBA_BLOCK = 256
MOBA_TOPK = 3
ROPE_THETA = 500000.0
ROPE_DIM = HEAD_DIM // 4
N_EXPERTS = 64
GROUP_SIZE = 8
TOPK_GROUPS = 4
TOP_K = 8
ROUTED_SCALE = 2.5
RMS_EPS = 1e-6

LANES = 128
SUBLANES = 8
MIB = 1024 * 1024
NEG_BIG = -1e30

ATT_BLOCK = 256
EXPERT_ROWS = 256
TOKEN_TILE = 128

COL_SB, COL_FOX, COL_MOBA = 0, 12, 24
COL_GQ, COL_GK, COL_GV, COL_GR, COL_GATES = 36, 38, 40, 44, 48
N_MAIN_COLS = 112 * LANES
MISC_FOX_F, MISC_GLR = 0, 4


def _cparams(n_axes, vmem_mib):
    return pltpu.CompilerParams(dimension_semantics=("arbitrary",) * n_axes,
                                vmem_limit_bytes=vmem_mib * MIB)


def _softplus(z):
    return jnp.maximum(z, 0.0) + jnp.log(1.0 + jnp.exp(-jnp.abs(z)))


def _log_sigmoid(z):
    return -_softplus(-z)


def _sigmoid(z):
    return 1.0 / (1.0 + jnp.exp(-z))


def _dot(a, b, precision=None):
    return jnp.dot(a, b, preferred_element_type=F32, precision=precision)


def _dot_nt(a, b, precision=None):
    return lax.dot_general(a, b, (((1,), (1,)), ((), ())), preferred_element_type=F32,
                           precision=precision)


def _dot_tn(a, b, precision=None):
    return lax.dot_general(a, b, (((0,), (0,)), ((), ())), preferred_element_type=F32,
                           precision=precision)


HIGHEST = lax.Precision.HIGHEST


def _split3(x):
    hi = x.astype(BF16)
    r1 = x - hi.astype(F32)
    mid = r1.astype(BF16)
    lo = (r1 - mid.astype(F32)).astype(BF16)
    return hi, mid, lo


def _ada_kernel(c_ref, w_ref, b_ref, o_ref):
    c = c_ref[...]
    c_act = (c * _sigmoid(c)).astype(BF16)
    o_ref[...] = _dot(c_act, w_ref[...].astype(BF16)) + b_ref[...]


def _ada_mod(c_pad, w_ada, b_ada):
    L, D, N = w_ada.shape
    tn = 1536
    return pl.pallas_call(
        _ada_kernel,
        out_shape=jax.ShapeDtypeStruct((L, SUBLANES, N), F32),
        grid=(L, N // tn),
        in_specs=[pl.BlockSpec((SUBLANES, D), lambda l, j: (0, 0)),
                  pl.BlockSpec((None, D, tn), lambda l, j: (l, 0, j)),
                  pl.BlockSpec((None, 1, tn), lambda l, j: (l, 0, j))],
        out_specs=pl.BlockSpec((None, SUBLANES, tn), lambda l, j: (l, 0, j)),
        compiler_params=_cparams(2, 40),
        name="ada_mod",
    )(c_pad, w_ada, b_ada.reshape(L, 1, N))


def _normed(x, g, scale, shift):
    y = x * lax.rsqrt(jnp.mean(x * x, axis=-1, keepdims=True) + RMS_EPS)
    return (y * g) * (1.0 + scale) + shift


def _norm_mod_kernel(x_ref, g_ref, sc_ref, sh_ref, h_ref):
    h_ref[...] = _normed(x_ref[...], g_ref[...], sc_ref[...], sh_ref[...]).astype(BF16)


def _pack_pairs(h):
    half = h.shape[1] // 2
    hi = lax.bitcast_convert_type(h[:, :half].astype(BF16).astype(F32), U32)
    lo = lax.bitcast_convert_type(h[:, half:].astype(BF16).astype(F32), U32)
    return hi | (lo >> 16)


def _unpack_pairs(w):
    a = lax.bitcast_convert_type(w & jnp.uint32(0xFFFF0000), F32).astype(BF16)
    b = lax.bitcast_convert_type(w << 16, F32).astype(BF16)
    return jnp.concatenate([a, b], axis=1)


def _norm_route_kernel(x_ref, g_ref, sc_ref, sh_ref, wr_ref, hp_ref, lg_ref):
    h = _normed(x_ref[...], g_ref[...], sc_ref[...], sh_ref[...])
    hp_ref[...] = _pack_pairs(h)
    lg_ref[...] = _dot(h, wr_ref[...], precision=HIGHEST)


def _norm_mod(x, g, scale, shift, S, w_router_pad=None):
    T, D = x.shape
    tm = 512
    nb = S // tm
    row = lambda i: (i, 0)
    per_batch = pl.BlockSpec((None, 1, D), lambda i: (i // nb, 0, 0))
    in_specs = [pl.BlockSpec((tm, D), row), pl.BlockSpec((1, D), lambda i: (0, 0)), per_batch, per_batch]
    if w_router_pad is None:
        return pl.pallas_call(
            _norm_mod_kernel, out_shape=jax.ShapeDtypeStruct((T, D), BF16), grid=(T // tm,),
            in_specs=in_specs, out_specs=pl.BlockSpec((tm, D), row),
            compiler_params=_cparams(1, 32), name="norm_mod",
        )(x, g, scale, shift)
    return pl.pallas_call(
        _norm_route_kernel,
        out_shape=(jax.ShapeDtypeStruct((T, D // 2), U32), jax.ShapeDtypeStruct((T, LANES), F32)),
        grid=(T // tm,),
        in_specs=in_specs + [pl.BlockSpec((D, LANES), lambda i: (0, 0))],
        out_specs=(pl.BlockSpec((tm, D // 2), row), pl.BlockSpec((tm, LANES), row)),
        compiler_params=_cparams(1, 32), name="norm_route",
    )(x, g, scale, shift, w_router_pad)


def _mm_kernel(*refs, cast_w, residual):
    if residual:
        a_ref, w_ref, x_ref, g_ref, o_ref = refs[:5]
        scratch = refs[5:]
    else:
        a_ref, w_ref, o_ref = refs[:3]
        scratch = refs[3:]
    if cast_w:
        wbf_ref, = scratch

        @pl.when(pl.program_id(1) == 0)
        def _():
            wbf_ref[...] = w_ref[...].astype(BF16)

        w = wbf_ref[...]
    else:
        w = w_ref[...]
    acc = _dot(a_ref[...], w)
    if residual:
        acc = x_ref[...] + g_ref[...] * acc
    o_ref[...] = acc.astype(o_ref.dtype)


def _matmul(a, w, tm, tn, out_dtype=F32, residual=None, S=None, vmem_mib=48, layer=None):
    M, K = a.shape
    N = w.shape[-1]
    cast_w = w.dtype != BF16
    if layer is None:
        w_spec = pl.BlockSpec((K, tn), lambda j, i: (0, j))
    else:
        w_spec = pl.BlockSpec((None, K, tn), lambda j, i: (layer, 0, j))
    in_specs = [pl.BlockSpec((tm, K), lambda j, i: (i, 0)), w_spec]
    args = [a, w]
    if residual is not None:
        x, gate = residual
        nb = S // tm
        in_specs += [pl.BlockSpec((tm, tn), lambda j, i: (i, j)),
                     pl.BlockSpec((None, 1, tn), lambda j, i: (i // nb, 0, j))]
        args += [x, gate]
    return pl.pallas_call(
        functools.partial(_mm_kernel, cast_w=cast_w, residual=residual is not None),
        out_shape=jax.ShapeDtypeStruct((M, N), out_dtype),
        grid=(N // tn, M // tm),
        in_specs=in_specs,
        out_specs=pl.BlockSpec((tm, tn), lambda j, i: (i, j)),
        scratch_shapes=[pltpu.VMEM((K, tn), BF16)] if cast_w else [],
        compiler_params=_cparams(2, vmem_mib),
        name="matmul_res" if residual is not None else "matmul",
    )(*args)


IN_TN = 1024
IN_SEGMENTS = ((0, COL_MOBA * LANES // IN_TN, 0),
               (COL_MOBA * LANES // IN_TN, COL_GATES * LANES // IN_TN, N_HEADS),
               (COL_GATES * LANES // IN_TN, N_MAIN_COLS // IN_TN, N_HEADS + GLA_GATE_RANK))
ROW_CHUNK = 256


def _in_proj_kernel(a_ref, w_ref, wn_ref, o_ref, wbf_ref, *, tn):
    j = pl.program_id(0)

    @pl.when(pl.program_id(1) == 0)
    def _():
        K = w_ref.shape[0]
        for lo, hi, shift in IN_SEGMENTS:
            @pl.when(jnp.logical_and(j >= lo, j < hi))
            def _():
                for r in range(0, K, ROW_CHUNK):
                    rows = slice(r, r + ROW_CHUNK)
                    if shift == 0:
                        wbf_ref[rows, :] = w_ref[rows, :].astype(BF16)
                    else:
                        wide = jnp.concatenate([w_ref[rows, :], wn_ref[rows, :]], axis=1)
                        wbf_ref[rows, :] = wide[:, shift:shift + tn].astype(BF16)

    o_ref[...] = _dot(a_ref[...], wbf_ref[...])


def _in_proj(a, w_in, layer, tm=512):
    M, K = a.shape
    tn = IN_TN
    return pl.pallas_call(
        functools.partial(_in_proj_kernel, tn=tn),
        out_shape=jax.ShapeDtypeStruct((M, N_MAIN_COLS), F32),
        grid=(N_MAIN_COLS // tn, M // tm),
        in_specs=[pl.BlockSpec((tm, K), lambda j, i: (i, 0)),
                  pl.BlockSpec((None, K, tn), lambda j, i: (layer, 0, j)),
                  pl.BlockSpec((None, K, LANES), lambda j, i: (layer, 0, (j + 1) * (tn // LANES)))],
        out_specs=pl.BlockSpec((tm, tn), lambda j, i: (i, j)),
        scratch_shapes=[pltpu.VMEM((K, tn), BF16)],
        compiler_params=_cparams(2, 48),
        name="in_proj",
    )(a, w_in, w_in)


def _misc_proj_kernel(a_ref, wf_ref, wg_ref, o_ref, wbf_ref):
    @pl.when(pl.program_id(0) == 0)
    def _():
        lane = lax.broadcasted_iota(I32, wf_ref.shape, 1)
        w = jnp.where(lane < MISC_GLR, wf_ref[...],
                      jnp.where(lane < MISC_GLR + GLA_GATE_RANK, wg_ref[...], 0.0))
        wbf_ref[...] = w.astype(BF16)

    o_ref[...] = _dot(a_ref[...], wbf_ref[...])


def _misc_proj(a, w_in, layer, tm=1024):
    M, K = a.shape
    return pl.pallas_call(
        _misc_proj_kernel,
        out_shape=jax.ShapeDtypeStruct((M, LANES), F32),
        grid=(M // tm,),
        in_specs=[pl.BlockSpec((tm, K), lambda i: (i, 0)),
                  pl.BlockSpec((None, K, LANES), lambda i: (layer, 0, COL_MOBA)),
                  pl.BlockSpec((None, K, LANES), lambda i: (layer, 0, COL_GATES))],
        out_specs=pl.BlockSpec((tm, LANES), lambda i: (i, 0)),
        scratch_shapes=[pltpu.VMEM((K, LANES), BF16)],
        compiler_params=_cparams(1, 32),
        name="misc_proj",
    )(a, w_in, w_in)


SB_LOG_WEIGHT_FLOOR = -110.0


def _sb_kernel(q_ref, k_ref, v_ref, o_ref, *, blk, scale):
    i = pl.program_id(1)
    row = lax.broadcasted_iota(I32, (blk, blk), 0)
    col = lax.broadcasted_iota(I32, (blk, blk), 1)
    later = (row > col).astype(BF16)

    def cond(state):
        jj, alive, _ = state
        return jnp.logical_and(jj <= i, alive)

    def body(state):
        jj, _, heads = state
        j = i - jj
        start = pl.multiple_of(j * blk, blk)
        past = jnp.logical_or(j < i, col < row)
        new, top = [], None
        for h in range(N_HEADS):
            c, acc = heads[h]
            hs = slice(h * HEAD_DIM, (h + 1) * HEAD_DIM)
            q = (q_ref[:, hs] * scale).astype(BF16)
            k = k_ref[pl.ds(start, blk), hs].astype(BF16)
            v = v_ref[pl.ds(start, blk), hs].astype(BF16)
            z = _dot_nt(q, k)
            ls = jnp.where(past, -_softplus(z), 0.0)
            hi = ls.astype(BF16)
            lo = (ls - hi.astype(F32)).astype(BF16)
            between = _dot(hi, later) + _dot(lo, later)
            w = jnp.where(past, jnp.exp(z + ls + between + c), 0.0)
            acc = acc + _dot(w.astype(BF16), v)
            c = c + jnp.sum(ls, axis=1, keepdims=True)
            new.append((c, acc))
            top = c if top is None else jnp.maximum(top, c)
        return jj + 1, jnp.max(top) > SB_LOG_WEIGHT_FLOOR, tuple(new)

    init = tuple((jnp.zeros((blk, 1), F32), jnp.zeros((blk, HEAD_DIM), F32)) for _ in range(N_HEADS))
    _, _, heads = lax.while_loop(cond, body, (jnp.int32(0), jnp.bool_(True), init))
    for h in range(N_HEADS):
        o_ref[:, h * HEAD_DIM:(h + 1) * HEAD_DIM] = heads[h][1]


def _sb_attention(proj, B, S, col0):
    T = proj.shape[0]
    blk = ATT_BLOCK
    nq = S // blk
    W = BRANCH_WIDTH
    cb = col0 * LANES // W
    return pl.pallas_call(
        functools.partial(_sb_kernel, blk=blk, scale=HEAD_DIM ** -0.5),
        out_shape=jax.ShapeDtypeStruct((T, W), F32),
        grid=(B, nq),
        in_specs=[pl.BlockSpec((blk, W), lambda b, i: (b * nq + i, cb)),
                  pl.BlockSpec((S, W), lambda b, i: (b, cb + 1)),
                  pl.BlockSpec((S, W), lambda b, i: (b, cb + 2))],
        out_specs=pl.BlockSpec((blk, W), lambda b, i: (b * nq + i, 0)),
        compiler_params=_cparams(2, 44),
        name="sb_attention",
    )(proj, proj, proj)


def _flash_kernel(qa_ref, ka_ref, v_ref, o_ref, *, blk):
    i = pl.program_id(1)
    row = lax.broadcasted_iota(I32, (blk, blk), 0)
    col = lax.broadcasted_iota(I32, (blk, blk), 1)
    A = 2 * HEAD_DIM

    def step(j, carry):
        start = pl.multiple_of(j * blk, blk)
        keep = jnp.logical_or(j < i, col <= row)
        new = []
        for h in range(N_HEADS):
            m, l, acc = carry[h]
            s = _dot_nt(qa_ref[:, h * A:(h + 1) * A], ka_ref[pl.ds(start, blk), h * A:(h + 1) * A])
            s = jnp.where(keep, s, NEG_BIG)
            m_new = jnp.maximum(m, jnp.max(s, axis=1, keepdims=True))
            alpha = jnp.exp(m - m_new)
            p = jnp.exp(s - m_new)
            l = alpha * l + jnp.sum(p, axis=1, keepdims=True)
            acc = alpha * acc + _dot(p.astype(BF16), v_ref[pl.ds(start, blk), h * HEAD_DIM:(h + 1) * HEAD_DIM])
            new.append((m_new, l, acc))
        return tuple(new)

    init = tuple((jnp.full((blk, 1), NEG_BIG, F32), jnp.zeros((blk, 1), F32),
                  jnp.zeros((blk, HEAD_DIM), F32)) for _ in range(N_HEADS))
    heads = lax.fori_loop(0, i + 1, step, init)
    for h in range(N_HEADS):
        _, l, acc = heads[h]
        o_ref[:, h * HEAD_DIM:(h + 1) * HEAD_DIM] = acc / l


def _flash_attention(qa, ka, va, B, S):
    T = qa.shape[0]
    blk = ATT_BLOCK
    nq = S // blk
    W = BRANCH_WIDTH
    return pl.pallas_call(
        functools.partial(_flash_kernel, blk=blk),
        out_shape=jax.ShapeDtypeStruct((T, W), F32),
        grid=(B, nq),
        in_specs=[pl.BlockSpec((blk, 2 * W), lambda b, i: (b * nq + i, 0)),
                  pl.BlockSpec((S, 2 * W), lambda b, i: (b, 0)),
                  pl.BlockSpec((S, W), lambda b, i: (b, 0))],
        out_specs=pl.BlockSpec((blk, W), lambda b, i: (b * nq + i, 0)),
        compiler_params=_cparams(2, 40),
        name="flash_attention",
    )(qa, ka, va)


def _fox_prep_kernel(q_ref, k_ref, v_ref, misc_ref, bias_ref, qa_ref, ka_ref, va_ref, carry_ref, *, tb, scale):
    va_ref[...] = v_ref[...].astype(BF16)
    @pl.when(pl.program_id(1) == 0)
    def _():
        carry_ref[...] = jnp.zeros_like(carry_ref)

    lane = lax.broadcasted_iota(I32, (tb, LANES), 1)
    lf = jnp.where(lane < N_HEADS, _log_sigmoid(misc_ref[...] + bias_ref[...]), 0.0)
    row = lax.broadcasted_iota(I32, (tb, tb), 0)
    col = lax.broadcasted_iota(I32, (tb, tb), 1)
    incl = (col <= row).astype(F32)
    F = _dot(incl, lf, precision=HIGHEST) + carry_ref[0:1, :]
    carry_ref[0:1, :] = F[tb - 1:tb, :]
    for h in range(N_HEADS):
        Fh = jnp.broadcast_to(F[:, h:h + 1], (tb, LANES))
        hi, mid, lo = (p.astype(F32) for p in _split3(Fh))
        ones = jnp.ones((tb, LANES), F32)
        zeros = jnp.zeros((tb, LANES), F32)
        eq = jnp.where(lane == 0, hi, jnp.where(lane == 1, mid, jnp.where(lane == 2, lo,
             jnp.where(lane < 6, ones, zeros))))
        ek = jnp.where(lane < 3, ones, jnp.where(lane == 3, -hi, jnp.where(lane == 4, -mid,
             jnp.where(lane == 5, -lo, zeros))))
        hs = slice(h * HEAD_DIM, (h + 1) * HEAD_DIM)
        qa_ref[:, 2 * h * HEAD_DIM:(2 * h + 1) * HEAD_DIM] = (q_ref[:, hs] * scale).astype(BF16)
        qa_ref[:, (2 * h + 1) * HEAD_DIM:(2 * h + 2) * HEAD_DIM] = eq.astype(BF16)
        ka_ref[:, 2 * h * HEAD_DIM:(2 * h + 1) * HEAD_DIM] = k_ref[:, hs].astype(BF16)
        ka_ref[:, (2 * h + 1) * HEAD_DIM:(2 * h + 2) * HEAD_DIM] = ek.astype(BF16)


def _fox_prep(proj, misc, fox_bias_pad, B, S):
    T = proj.shape[0]
    tb = 512
    nb = S // tb
    qblk = COL_FOX * LANES // BRANCH_WIDTH
    out = jax.ShapeDtypeStruct((T, 2 * BRANCH_WIDTH), BF16)
    return pl.pallas_call(
        functools.partial(_fox_prep_kernel, tb=tb, scale=HEAD_DIM ** -0.5),
        out_shape=(out, out, jax.ShapeDtypeStruct((T, BRANCH_WIDTH), BF16)),
        grid=(B, nb),
        in_specs=[pl.BlockSpec((tb, BRANCH_WIDTH), lambda b, i: (b * nb + i, qblk)),
                  pl.BlockSpec((tb, BRANCH_WIDTH), lambda b, i: (b * nb + i, qblk + 1)),
                  pl.BlockSpec((tb, BRANCH_WIDTH), lambda b, i: (b * nb + i, qblk + 2)),
                  pl.BlockSpec((tb, LANES), lambda b, i: (b * nb + i, 0)),
                  pl.BlockSpec((1, LANES), lambda b, i: (0, 0))],
        out_specs=(pl.BlockSpec((tb, 2 * BRANCH_WIDTH), lambda b, i: (b * nb + i, 0)),
                   pl.BlockSpec((tb, 2 * BRANCH_WIDTH), lambda b, i: (b * nb + i, 0)),
                   pl.BlockSpec((tb, BRANCH_WIDTH), lambda b, i: (b * nb + i, 0))),
        scratch_shapes=[pltpu.VMEM((SUBLANES, LANES), F32)],
        compiler_params=_cparams(2, 32),
        name="fox_prep",
    )(proj, proj, proj, misc, fox_bias_pad)


def _rope(x, cos, sin, lane):
    half = ROPE_DIM // 2
    up = pltpu.roll(x, half, 1)
    down = pltpu.roll(x, LANES - half, 1)
    rot = jnp.where(lane < half, -down * sin, jnp.where(lane < ROPE_DIM, up * sin, 0.0))
    return x * jnp.where(lane < ROPE_DIM, cos, 1.0) + rot


def _angles(pos_ref, freq_ref):
    ang = pos_ref[...].astype(F32) * freq_ref[...]
    return jnp.cos(ang), jnp.sin(ang)


def _moba_k_kernel(k_ref, v_ref, pos_ref, freq_ref, ka_ref, va_ref, kmean_ref, *, tb):
    i = pl.program_id(1)
    va_ref[...] = v_ref[...].astype(BF16)

    @pl.when(i == 0)
    def _():
        kmean_ref[...] = jnp.zeros_like(kmean_ref)

    lane = lax.broadcasted_iota(I32, (tb, LANES), 1)
    cos, sin = _angles(pos_ref, freq_ref)
    onehot = jnp.where(lane == i, 1.0, 0.0).astype(BF16)
    this_row = lax.broadcasted_iota(I32, (LANES, HEAD_DIM), 0) == i
    for h in range(N_HEADS):
        hs = slice(h * HEAD_DIM, (h + 1) * HEAD_DIM)
        kr = _rope(k_ref[:, hs], cos, sin, lane)
        mean = jnp.sum(kr, axis=0, keepdims=True) * (1.0 / tb)
        kmean_ref[:, hs] = jnp.where(this_row, mean, kmean_ref[:, hs])
        ka_ref[:, 2 * h * HEAD_DIM:(2 * h + 1) * HEAD_DIM] = kr.astype(BF16)
        ka_ref[:, (2 * h + 1) * HEAD_DIM:(2 * h + 2) * HEAD_DIM] = onehot


def _moba_q_kernel(q_ref, pos_ref, freq_ref, kmean_ref, qa_ref, *, tb, scale):
    i = pl.program_id(1)
    lane = lax.broadcasted_iota(I32, (tb, LANES), 1)
    cos, sin = _angles(pos_ref, freq_ref)
    neg_inf = jnp.float32(-jnp.inf)
    for h in range(N_HEADS):
        hs = slice(h * HEAD_DIM, (h + 1) * HEAD_DIM)
        qr = _rope(q_ref[:, hs], cos, sin, lane)
        gate = _dot_nt(qr, kmean_ref[:, hs], precision=HIGHEST)
        cur = jnp.where(lane < i, gate, neg_inf)
        chosen = lane == i
        for _ in range(MOBA_TOPK):
            m = jnp.max(cur, axis=1, keepdims=True)
            first = jnp.min(jnp.where(jnp.logical_and(cur == m, m > neg_inf), lane, LANES),
                            axis=1, keepdims=True)
            pick = lane == first
            chosen = jnp.logical_or(chosen, pick)
            cur = jnp.where(pick, neg_inf, cur)
        bias = jnp.where(jnp.logical_or(chosen, lane >= LANES // 2), 0.0, NEG_BIG)
        qa_ref[:, 2 * h * HEAD_DIM:(2 * h + 1) * HEAD_DIM] = (qr * scale).astype(BF16)
        qa_ref[:, (2 * h + 1) * HEAD_DIM:(2 * h + 2) * HEAD_DIM] = bias.astype(BF16)


def _moba_prep(proj, pos_col, freq_lanes, B, S):
    T = proj.shape[0]
    tb = MOBA_BLOCK
    nb = S // tb
    qblk = COL_MOBA * LANES // BRANCH_WIDTH
    aug = jax.ShapeDtypeStruct((T, 2 * BRANCH_WIDTH), BF16)
    row_spec = lambda c: pl.BlockSpec((tb, BRANCH_WIDTH), lambda b, i: (b * nb + i, c))
    pos_spec = pl.BlockSpec((tb, 1), lambda b, i: (b * nb + i, 0))
    freq_spec = pl.BlockSpec((1, LANES), lambda b, i: (0, 0))
    aug_spec = pl.BlockSpec((tb, 2 * BRANCH_WIDTH), lambda b, i: (b * nb + i, 0))
    kmean_spec = pl.BlockSpec((None, LANES, BRANCH_WIDTH), lambda b, i: (b, 0, 0))
    ka, va, kmean = pl.pallas_call(
        functools.partial(_moba_k_kernel, tb=tb),
        out_shape=(aug, jax.ShapeDtypeStruct((T, BRANCH_WIDTH), BF16),
                   jax.ShapeDtypeStruct((B, LANES, BRANCH_WIDTH), F32)),
        grid=(B, nb),
        in_specs=[row_spec(qblk + 1), row_spec(qblk + 2), pos_spec, freq_spec],
        out_specs=(aug_spec, pl.BlockSpec((tb, BRANCH_WIDTH), lambda b, i: (b * nb + i, 0)), kmean_spec),
        compiler_params=_cparams(2, 32),
        name="moba_k_prep",
    )(proj, proj, pos_col, freq_lanes)
    qa = pl.pallas_call(
        functools.partial(_moba_q_kernel, tb=tb, scale=HEAD_DIM ** -0.5),
        out_shape=aug,
        grid=(B, nb),
        in_specs=[row_spec(qblk), pos_spec, freq_spec, kmean_spec],
        out_specs=aug_spec,
        compiler_params=_cparams(2, 32),
        name="moba_q_prep",
    )(proj, pos_col, freq_lanes, kmean)
    return qa, ka, va


def _gla_kernel(q_ref, k_ref, v_ref, gr_ref, misc_ref, wg_ref, bg_ref, gn_ref, o_ref, state_ref, *, tb):
    C = GLA_CHUNK
    @pl.when(pl.program_id(1) == 0)
    def _():
        state_ref[...] = jnp.zeros_like(state_ref)

    la_all = _log_sigmoid(_dot(misc_ref[...], wg_ref[...], precision=HIGHEST) + bg_ref[...]) \
        * (1.0 / GLA_GATE_NORMALIZER)
    rowc = lax.broadcasted_iota(I32, (C, C), 0)
    colc = lax.broadcasted_iota(I32, (C, C), 1)
    incl = (colc <= rowc).astype(F32)
    causal = colc <= rowc
    lane = lax.broadcasted_iota(I32, (C, LANES), 1)
    row2 = lax.broadcasted_iota(I32, (2 * GLA_HEAD_K, LANES), 0)
    ones_cv = jnp.ones((C, LANES), F32)
    qscale = GLA_HEAD_K ** -0.5
    gn = gn_ref[...]
    for c in range(tb // C):
        rs = slice(c * C, (c + 1) * C)
        for p in range(2):
            ps = slice(p * LANES, (p + 1) * LANES)
            la = la_all[rs, ps]
            b = _dot(incl, la, precision=HIGHEST)
            b_last = b[C - 1:C, :]
            eb = jnp.exp(b)
            q_dec = q_ref[rs, ps] * qscale * eb
            k = k_ref[rs, ps]
            k_inv = (k * jnp.exp(-b)).astype(BF16)
            k_end = (k * jnp.exp(b_last - b)).astype(BF16)
            decay = jnp.exp(_dot_tn(la, ones_cv, precision=HIGHEST))
            st = state_ref[p * LANES:(p + 1) * LANES, :]
            new_rows = []
            for e in range(2):
                h = 2 * p + e
                own = jnp.logical_and(lane >= e * GLA_HEAD_K, lane < (e + 1) * GLA_HEAD_K)
                qh = jnp.where(own, q_dec, 0.0).astype(BF16)
                v = v_ref[rs, h * HEAD_DIM:(h + 1) * HEAD_DIM].astype(BF16)
                attn = jnp.where(causal, _dot_nt(qh, k_inv), 0.0)
                st_h = jnp.where(jnp.logical_and(row2 >= e * GLA_HEAD_K, row2 < (e + 1) * GLA_HEAD_K), st, 0.0)
                o = _dot(attn.astype(BF16), v) + _dot(qh, st_h.astype(BF16))
                y = o * lax.rsqrt(jnp.mean(o * o, axis=-1, keepdims=True) + RMS_EPS) * gn
                g = gr_ref[rs, h * HEAD_DIM:(h + 1) * HEAD_DIM]
                o_ref[rs, h * HEAD_DIM:(h + 1) * HEAD_DIM] = y * (g * _sigmoid(g))
                new_rows.append(_dot_tn(k_end, v))
            kv = jnp.where(row2 < GLA_HEAD_K, new_rows[0], new_rows[1])
            state_ref[p * LANES:(p + 1) * LANES, :] = decay * st + kv


def _gla(proj, misc, w_gate_pad, b_gate, gla_norm, B, S):
    T = proj.shape[0]
    tb = 512
    nb = S // tb
    kd = 2 * LANES
    rows = lambda w, c: pl.BlockSpec((tb, w), lambda b, i: (b * nb + i, c))
    const = lambda shape: pl.BlockSpec(shape, lambda b, i: (0, 0))
    return pl.pallas_call(
        functools.partial(_gla_kernel, tb=tb),
        out_shape=jax.ShapeDtypeStruct((T, BRANCH_WIDTH), F32),
        grid=(B, nb),
        in_specs=[rows(kd, COL_GQ * LANES // kd), rows(kd, COL_GK * LANES // kd),
                  rows(BRANCH_WIDTH, COL_GV * LANES // BRANCH_WIDTH),
                  rows(BRANCH_WIDTH, COL_GR * LANES // BRANCH_WIDTH),
                  rows(LANES, 0), const((LANES, kd)), const((1, kd)), const((1, HEAD_DIM))],
        out_specs=rows(BRANCH_WIDTH, 0),
        scratch_shapes=[pltpu.VMEM((kd, HEAD_DIM), F32)],
        compiler_params=_cparams(2, 32),
        name="gla",
    )(proj, proj, proj, proj, misc, w_gate_pad, b_gate, gla_norm)


def _merge_kernel(oa_ref, ob_ref, oc_ref, od_ref, g0_ref, g1_ref, g2_ref, g3_ref, w_ref, o_ref, wbf_ref):
    @pl.when(pl.program_id(1) == 0)
    def _():
        wbf_ref[...] = w_ref[...].astype(BF16)

    acc = None
    for n, (b_ref, g_ref) in enumerate(((oa_ref, g0_ref), (ob_ref, g1_ref), (oc_ref, g2_ref), (od_ref, g3_ref))):
        term = _sigmoid(g_ref[...]) * _dot(b_ref[...].astype(BF16), wbf_ref[n])
        acc = term if acc is None else acc + term
    o_ref[...] = acc.astype(BF16)


def _merge(branches, proj, w_branch, layer):
    T = proj.shape[0]
    _, nbr, W, D = w_branch.shape
    tm, tn = 512, 512
    g0 = COL_GATES * LANES // tn
    per = D // tn
    br_spec = pl.BlockSpec((tm, W), lambda j, i: (i, 0))
    gate_spec = lambda n: pl.BlockSpec((tm, tn), lambda j, i: (i, g0 + n * per + j))
    return pl.pallas_call(
        _merge_kernel,
        out_shape=jax.ShapeDtypeStruct((T, D), BF16),
        grid=(D // tn, T // tm),
        in_specs=[br_spec] * 4 + [gate_spec(n) for n in range(4)]
                 + [pl.BlockSpec((None, nbr, W, tn), lambda j, i: (layer, 0, 0, j))],
        out_specs=pl.BlockSpec((tm, tn), lambda j, i: (i, j)),
        scratch_shapes=[pltpu.VMEM((nbr, W, tn), BF16)],
        compiler_params=_cparams(2, 40),
        name="branch_merge",
    )(*branches, proj, proj, proj, proj, w_branch)


def _butterfly(x, lane, op):
    for s in (1, 2, 4):
        up = pltpu.roll(x, s, 1)
        down = pltpu.roll(x, LANES - s, 1)
        x = op(x, jnp.where((lane & s) != 0, up, down))
    return x


def _route_kernel(lg_ref, bias_ref, e8_ref, w8_ref, p8_ref, cnt_ref, carry_ref, *, tm):
    @pl.when(pl.program_id(0) == 0)
    def _():
        carry_ref[...] = jnp.zeros_like(carry_ref)

    neg_inf = jnp.float32(-jnp.inf)
    lane = lax.broadcasted_iota(I32, (tm, LANES), 1)
    valid = lane < N_EXPERTS
    scores = _sigmoid(lg_ref[...])
    biased = jnp.where(valid, scores + bias_ref[...], neg_inf)
    g1 = _butterfly(biased, lane, jnp.maximum)
    first = _butterfly(jnp.where(biased == g1, lane, LANES), lane, jnp.minimum)
    g2 = _butterfly(jnp.where(lane == first, neg_inf, biased), lane, jnp.maximum)
    gs = g1 + g2
    gs = jnp.where(valid, gs, pltpu.roll(gs, N_EXPERTS, 1))
    gidx = lane >> 3
    beaten = jnp.zeros((tm, LANES), I32)
    for r in range(1, N_EXPERTS // GROUP_SIZE):
        other = pltpu.roll(gs, GROUP_SIZE * r, 1)
        og = (gidx - r) & (N_EXPERTS // GROUP_SIZE - 1)
        wins = jnp.logical_or(other > gs, jnp.logical_and(other == gs, og < gidx))
        beaten = beaten + wins.astype(I32)
    cur = jnp.where(jnp.logical_and(beaten < TOPK_GROUPS, valid), biased, neg_inf)
    sel = jnp.zeros((tm, LANES), jnp.bool_)
    for _ in range(TOP_K):
        m = jnp.max(cur, axis=1, keepdims=True)
        pick = lane == jnp.min(jnp.where(cur == m, lane, LANES), axis=1, keepdims=True)
        sel = jnp.logical_or(sel, pick)
        cur = jnp.where(pick, neg_inf, cur)
    wsel = jnp.where(sel, scores, 0.0)
    wd = wsel / jnp.sum(wsel, axis=1, keepdims=True) * ROUTED_SCALE
    selb = jnp.where(sel, 1.0, 0.0).astype(BF16)
    row = lax.broadcasted_iota(I32, (tm, tm), 0)
    col = lax.broadcasted_iota(I32, (tm, tm), 1)
    pos = _dot((col < row).astype(BF16), selb) + carry_ref[0:1, :]
    total = carry_ref[0:1, :] + jnp.sum(selb.astype(F32), axis=0, keepdims=True)
    carry_ref[0:1, :] = total
    cnt_ref[...] = jnp.broadcast_to(total, cnt_ref.shape)
    r2 = lax.broadcasted_iota(I32, (LANES, LANES), 0)
    c2 = lax.broadcasted_iota(I32, (LANES, LANES), 1)
    slot = _dot(selb, (r2 < c2).astype(BF16))
    lane_f = lane.astype(F32)
    e8 = jnp.zeros((tm, LANES), F32)
    w8 = jnp.zeros((tm, LANES), F32)
    p8 = jnp.zeros((tm, LANES), F32)
    for k in range(TOP_K):
        mk = jnp.logical_and(sel, slot == k)
        put = lane == k
        e8 = jnp.where(put, jnp.sum(jnp.where(mk, lane_f, 0.0), axis=1, keepdims=True), e8)
        w8 = jnp.where(put, jnp.sum(jnp.where(mk, wd, 0.0), axis=1, keepdims=True), w8)
        p8 = jnp.where(put, jnp.sum(jnp.where(mk, pos, 0.0), axis=1, keepdims=True), p8)
    e8_ref[...] = e8.astype(I32)
    w8_ref[...] = w8
    p8_ref[...] = p8.astype(I32)


def _route(logits, router_bias_pad):
    T = logits.shape[0]
    tm = 512
    row = pl.BlockSpec((tm, LANES), lambda i: (i, 0))
    return pl.pallas_call(
        functools.partial(_route_kernel, tm=tm),
        out_shape=(jax.ShapeDtypeStruct((T, LANES), I32), jax.ShapeDtypeStruct((T, LANES), F32),
                   jax.ShapeDtypeStruct((T, LANES), I32), jax.ShapeDtypeStruct((SUBLANES, LANES), F32)),
        grid=(T // tm,),
        in_specs=[row, pl.BlockSpec((1, LANES), lambda i: (0, 0))],
        out_specs=(row, row, row, pl.BlockSpec((SUBLANES, LANES), lambda i: (0, 0))),
        scratch_shapes=[pltpu.VMEM((SUBLANES, LANES), F32)],
        compiler_params=_cparams(1, 32),
        name="route",
    )(logits, router_bias_pad)


def _row_copy(src, src_row, dst, dst_row, sem):
    return pltpu.make_async_copy(src.at[pl.ds(src_row, 1), :], dst.at[pl.ds(dst_row, 1), :], sem)


def _dispatch_kernel(pad_end_ref, padded_ref, dest_ref, hp_ref, xs_ref, zero_ref, sem_ref, *, tm, rows, n_blocks):
    @pl.when(pl.program_id(0) == 0)
    def _():
        zero_ref[...] = jnp.zeros_like(zero_ref)

        def fill(e, do_wait):
            @pl.when(padded_ref[e] > 0)
            def _():
                start = pl.multiple_of(pad_end_ref[e] - rows, rows)
                cp = pltpu.make_async_copy(zero_ref, xs_ref.at[pl.ds(start, rows), :], sem_ref)
                if do_wait:
                    cp.wait()
                else:
                    cp.start()

        lax.fori_loop(0, N_EXPERTS, lambda e, c: (fill(e, False), c)[1], 0)
        lax.fori_loop(0, N_EXPERTS, lambda e, c: (fill(e, True), c)[1], 0)

        def tail(b):
            return pltpu.make_async_copy(zero_ref, xs_ref.at[pl.ds(pl.multiple_of(b * rows, rows), rows), :], sem_ref)

        n_used = pad_end_ref[N_EXPERTS - 1] // rows
        lax.fori_loop(n_used, n_blocks, lambda b, c: (tail(b).start(), c)[1], 0)
        lax.fori_loop(n_used, n_blocks, lambda b, c: (tail(b).wait(), c)[1], 0)

    def issue(t, c):
        for k in range(TOP_K):
            _row_copy(hp_ref, t, xs_ref, dest_ref[t * TOP_K + k], sem_ref).start(priority=k % 2)
        return c

    def drain(t, c):
        for k in range(TOP_K):
            _row_copy(hp_ref, t, xs_ref, dest_ref[t * TOP_K + k], sem_ref).wait()
        return c

    lax.fori_loop(0, tm, issue, 0)
    lax.fori_loop(0, tm, drain, 0)


def _dispatch(hp, dest_flat, pad_end, padded, n_rows):
    T, W = hp.shape
    tm = TOKEN_TILE
    grid_spec = pltpu.PrefetchScalarGridSpec(
        num_scalar_prefetch=2,
        grid=(T // tm,),
        in_specs=[pl.BlockSpec((tm * TOP_K,), lambda i, pe, pd: (i,), memory_space=pltpu.SMEM),
                  pl.BlockSpec((tm, W), lambda i, pe, pd: (i, 0))],
        out_specs=pl.BlockSpec(memory_space=pl.ANY),
        scratch_shapes=[pltpu.VMEM((EXPERT_ROWS, W), U32), pltpu.SemaphoreType.DMA(())],
    )
    return pl.pallas_call(
        functools.partial(_dispatch_kernel, tm=tm, rows=EXPERT_ROWS, n_blocks=n_rows // EXPERT_ROWS),
        out_shape=jax.ShapeDtypeStruct((n_rows, W), U32),
        grid_spec=grid_spec,
        compiler_params=_cparams(1, 32),
        name="dispatch",
    )(pad_end, padded, dest_flat, hp)


def _expert_kernel(blk_e_ref, next_e_ref, n_used_ref, x_ref, wg_hbm, wu_hbm, wd_hbm, y_ref,
                   wg_f32, wu_f32, wd_f32, wg_bf, wu_bf, wd_bf, sem_ref, slot_ref, *, layer):
    i = pl.program_id(0)
    e = blk_e_ref[i]
    active = i < n_used_ref[0]
    first = jnp.logical_or(i == 0, e != blk_e_ref[jnp.maximum(i - 1, 0)])

    def fetch(expert, slot):
        return [pltpu.make_async_copy(w.at[layer, expert], buf.at[slot], sem_ref.at[slot])
                for w, buf in ((wg_hbm, wg_f32), (wu_hbm, wu_f32), (wd_hbm, wd_f32))]

    @pl.when(i == 0)
    def _():
        slot_ref[0] = 0
        for cp in fetch(e, 0):
            cp.start()

    @pl.when(jnp.logical_and(active, first))
    def _():
        slot = slot_ref[0]
        for cp in fetch(e, slot):
            cp.wait()
        nxt = next_e_ref[i]

        @pl.when(nxt >= 0)
        def _():
            for cp in fetch(nxt, 1 - slot):
                cp.start()

        wg_bf[...] = wg_f32[slot].astype(BF16)
        wu_bf[...] = wu_f32[slot].astype(BF16)
        wd_bf[...] = wd_f32[slot].astype(BF16)
        slot_ref[0] = 1 - slot

    @pl.when(active)
    def _():
        x = _unpack_pairs(x_ref[...])
        g = _dot(x, wg_bf[...])
        u = _dot(x, wu_bf[...])
        hb = (g * _sigmoid(g)) * u
        y_ref[...] = _pack_pairs(_dot(hb.astype(BF16), wd_bf[...]))

    @pl.when(jnp.logical_not(active))
    def _():
        y_ref[...] = jnp.zeros_like(y_ref)


def _experts(xs, blk_e, next_e, n_used, w_gate, w_up, w_down, layer):
    P, W = xs.shape
    _, E, D, Hx = w_gate.shape
    M = EXPERT_ROWS
    nblk = P // M
    xrow = lambda i, be, ne, nu: (jnp.minimum(i, nu[0] - 1), 0)
    hbm = pl.BlockSpec(memory_space=pl.ANY)
    grid_spec = pltpu.PrefetchScalarGridSpec(
        num_scalar_prefetch=3,
        grid=(nblk,),
        in_specs=[pl.BlockSpec((M, W), xrow), hbm, hbm, hbm],
        out_specs=pl.BlockSpec((M, W), lambda i, be, ne, nu: (i, 0)),
        scratch_shapes=[pltpu.VMEM((2, D, Hx), F32), pltpu.VMEM((2, D, Hx), F32), pltpu.VMEM((2, Hx, D), F32),
                        pltpu.VMEM((D, Hx), BF16), pltpu.VMEM((D, Hx), BF16), pltpu.VMEM((Hx, D), BF16),
                        pltpu.SemaphoreType.DMA((2,)), pltpu.SMEM((1,), I32)],
    )
    return pl.pallas_call(
        functools.partial(_expert_kernel, layer=layer),
        out_shape=jax.ShapeDtypeStruct((P, W), U32),
        grid_spec=grid_spec,
        compiler_params=_cparams(1, 52),
        name="experts",
    )(blk_e, next_e, n_used, xs, w_gate, w_up, w_down)


def _next_expert(blk_e, cnt):
    E = cnt.shape[0]
    idx = jnp.where(cnt > 0, jnp.arange(E, dtype=I32), E)
    later = jnp.concatenate([lax.cummin(idx, axis=0, reverse=True)[1:], jnp.full((1,), E, I32)])
    return jnp.where(later < E, later, -1)[blk_e]


def _unpack_pairs_f32(w):
    a = lax.bitcast_convert_type(w & jnp.uint32(0xFFFF0000), F32)
    b = lax.bitcast_convert_type(w << 16, F32)
    return jnp.concatenate([a, b], axis=1)


def _combine_kernel(dcur_ref, dnext_ref, x_ref, ysh_ref, w8_ref, g2_ref, fn_ref, ys_ref, o_ref, buf_ref, sem_ref,
                    *, tm, n_steps, final_norm):
    i = pl.program_id(0)
    slot = lax.rem(i, 2)

    def issue(dref, s):
        def body(t, c):
            for k in range(TOP_K):
                pltpu.make_async_copy(ys_ref.at[pl.ds(dref[t * TOP_K + k], 1), :],
                                      buf_ref.at[s, pl.ds(k * tm + t, 1), :], sem_ref.at[s]).start(priority=k % 2)
            return c
        lax.fori_loop(0, tm, body, 0)

    @pl.when(i == 0)
    def _():
        issue(dcur_ref, 0)

    @pl.when(i + 1 < n_steps)
    def _():
        issue(dnext_ref, 1 - slot)

    def drain(t, c):
        for k in range(TOP_K):
            pltpu.make_async_copy(ys_ref.at[pl.ds(0, 1), :], buf_ref.at[slot, pl.ds(0, 1), :],
                                  sem_ref.at[slot]).wait()
        return c

    lax.fori_loop(0, tm, drain, 0)
    w8 = w8_ref[...]
    y = _unpack_pairs_f32(ysh_ref[...])
    for k in range(TOP_K):
        y = y + w8[:, k:k + 1] * _unpack_pairs_f32(buf_ref[slot, k * tm:(k + 1) * tm, :])
    out = x_ref[...] + g2_ref[...] * y
    if final_norm:
        out = out * lax.rsqrt(jnp.mean(out * out, axis=-1, keepdims=True) + RMS_EPS) * fn_ref[...]
    o_ref[...] = out


def _combine(x, ysh, ys, dest_flat, w8, gate2, final_g, S, final_norm):
    T, D = x.shape
    W = ys.shape[1]
    tm = TOKEN_TILE
    nb = S // tm
    n_steps = T // tm
    row = lambda i: (i, 0)
    return pl.pallas_call(
        functools.partial(_combine_kernel, tm=tm, n_steps=n_steps, final_norm=final_norm),
        out_shape=jax.ShapeDtypeStruct((T, D), F32),
        grid=(n_steps,),
        in_specs=[pl.BlockSpec((tm * TOP_K,), lambda i: (i,), memory_space=pltpu.SMEM),
                  pl.BlockSpec((tm * TOP_K,), lambda i: (jnp.minimum(i + 1, n_steps - 1),), memory_space=pltpu.SMEM),
                  pl.BlockSpec((tm, D), row), pl.BlockSpec((tm, W), row),
                  pl.BlockSpec((tm, LANES), row),
                  pl.BlockSpec((None, 1, D), lambda i: (i // nb, 0, 0)),
                  pl.BlockSpec((1, D), lambda i: (0, 0)),
                  pl.BlockSpec(memory_space=pl.ANY)],
        out_specs=pl.BlockSpec((tm, D), row),
        scratch_shapes=[pltpu.VMEM((2, TOP_K * tm, W), U32), pltpu.SemaphoreType.DMA((2,))],
        compiler_params=_cparams(1, 32),
        name="combine",
    )(dest_flat, dest_flat, x, ysh, w8, gate2, final_g, ys)


def _pad_lanes(v, offset=0):
    out = jnp.zeros((1, LANES), F32)
    return out.at[0, offset:offset + v.shape[0]].set(v.astype(F32))


def kernel(x, c, positions, attn_norm, w_ada, b_ada, w_in, fox_bias, gla_w_gate, gla_b_gate, gla_norm,
           w_branch, w_out, ffn_norm, w_router, router_bias, w_exp_gate, w_exp_up, w_exp_down,
           w_sh_gate, w_sh_up, w_sh_down, final_norm):
    B, S, D = x.shape
    L = w_ada.shape[0]
    T = B * S
    E = N_EXPERTS
    M = EXPERT_ROWS
    xf = x.reshape(T, D)

    c_pad = jnp.zeros((SUBLANES, D), F32).at[:B].set(c)
    mod = _ada_mod(c_pad, w_ada, b_ada)

    half = ROPE_DIM // 2
    inv_freq = jnp.power(ROPE_THETA, -jnp.arange(half, dtype=F32) * 2.0 / ROPE_DIM)
    freq_lanes = _pad_lanes(jnp.concatenate([inv_freq, inv_freq]))
    pos_col = positions.reshape(T, 1)

    n_blocks = (T * TOP_K + E * (M - 1) + M - 1) // M
    P = n_blocks * M

    for l in range(L):
        m6 = mod[l, :B].reshape(B, 6, 1, D)
        shift1, scale1, gate1, shift2, scale2, gate2 = (m6[:, n] for n in range(6))

        h1 = _norm_mod(xf, attn_norm[l].reshape(1, D), scale1, shift1, S)
        proj = _in_proj(h1, w_in, l)
        misc = _misc_proj(h1, w_in, l)

        o_a = _sb_attention(proj, B, S, COL_SB)
        qa, ka, va = _fox_prep(proj, misc, _pad_lanes(fox_bias[l], MISC_FOX_F), B, S)
        o_b = _flash_attention(qa, ka, va, B, S)
        qa, ka, va = _moba_prep(proj, pos_col, freq_lanes, B, S)
        o_c = _flash_attention(qa, ka, va, B, S)
        wg_pad = jnp.zeros((LANES, gla_w_gate.shape[2]), F32).at[MISC_GLR:MISC_GLR + GLA_GATE_RANK].set(gla_w_gate[l])
        o_d = _gla(proj, misc, wg_pad, gla_b_gate[l].reshape(1, -1), gla_norm[l].reshape(1, -1), B, S)

        merged = _merge((o_a, o_b, o_c, o_d), proj, w_branch, l)
        xf = _matmul(merged, w_out, tm=512, tn=1024, residual=(xf, gate1), S=S, layer=l)

        wr_pad = jnp.zeros((D, LANES), F32).at[:, :E].set(w_router[l])
        hp, logits = _norm_mod(xf, ffn_norm[l].reshape(1, D), scale2, shift2, S, w_router_pad=wr_pad)
        e8, w8, p8, counts = _route(logits, _pad_lanes(router_bias[l]))
        cnt = counts[0, :E].astype(I32)
        padded = (cnt + M - 1) // M * M
        pad_end = jnp.cumsum(padded)
        pad_start = pad_end - padded
        dest = (pad_start[e8[:, :TOP_K]] + p8[:, :TOP_K]).reshape(T * TOP_K)
        blk_start = jnp.arange(n_blocks, dtype=I32) * M
        blk_e = jnp.minimum(jnp.sum((pad_end[None, :] <= blk_start[:, None]).astype(I32), axis=1), E - 1)
        n_used = (pad_end[E - 1:] // M).astype(I32)

        xs = _dispatch(hp, dest, pad_end.astype(I32), padded.astype(I32), P)
        ys = _experts(xs, blk_e, _next_expert(blk_e, cnt), n_used, w_exp_gate, w_exp_up, w_exp_down, l)
        ysh = _experts(hp, jnp.zeros((T // M,), I32), jnp.full((T // M,), -1, I32), jnp.full((1,), T // M, I32),
                       w_sh_gate[:, None], w_sh_up[:, None], w_sh_down[:, None], l)
        last = l == L - 1
        xf = _combine(xf, ysh, ys, dest, w8, gate2, final_norm.reshape(1, D), S, final_norm=last)

    return xf.reshape(B, S, D)
```

```python
import functools

import jax
import jax.numpy as jnp
from jax import lax
from jax.experimental import pallas as pl
from jax.experimental.pallas import tpu as pltpu

F32 = jnp.float32
BF16 = jnp.bfloat16
I32 = jnp.int32
U32 = jnp.uint32

HEAD_DIM = 128
N_HEADS = 4
BRANCH_WIDTH = N_HEADS * HEAD_DIM
GLA_HEAD_K = 64
GLA_CHUNK = 64
GLA_GATE_RANK = 16
GLA_GATE_NORMALIZER = 16.0
MOBA_BLOCK = 256
MOBA_TOPK = 3
ROPE_THETA = 500000.0
ROPE_DIM = HEAD_DIM // 4
N_EXPERTS = 64
GROUP_SIZE = 8
TOPK_GROUPS = 4
TOP_K = 8
ROUTED_SCALE = 2.5
RMS_EPS = 1e-6

LANES = 128
SUBLANES = 8
MIB = 1024 * 1024
NEG_BIG = -1e30

ATT_BLOCK = 256
EXPERT_ROWS = 256
TOKEN_TILE = 128

COL_SB, COL_FOX, COL_MOBA = 0, 12, 24
COL_GQ, COL_GK, COL_GV, COL_GR, COL_GATES = 36, 38, 40, 44, 48
N_MAIN_COLS = 112 * LANES
MISC_FOX_F, MISC_GLR = 0, 4


def _cparams(n_axes, vmem_mib):
    return pltpu.CompilerParams(dimension_semantics=("arbitrary",) * n_axes,
                                vmem_limit_bytes=vmem_mib * MIB)


def _softplus(z):
    return jnp.maximum(z, 0.0) + jnp.log(1.0 + jnp.exp(-jnp.abs(z)))


def _log_sigmoid(z):
    return -_softplus(-z)


def _sigmoid(z):
    return 1.0 / (1.0 + jnp.exp(-z))


def _dot(a, b, precision=None):
    return jnp.dot(a, b, preferred_element_type=F32, precision=precision)


def _dot_nt(a, b, precision=None):
    return lax.dot_general(a, b, (((1,), (1,)), ((), ())), preferred_element_type=F32,
                           precision=precision)


def _dot_tn(a, b, precision=None):
    return lax.dot_general(a, b, (((0,), (0,)), ((), ())), preferred_element_type=F32,
                           precision=precision)


HIGHEST = lax.Precision.HIGHEST


def _split3(x):
    hi = x.astype(BF16)
    r1 = x - hi.astype(F32)
    mid = r1.astype(BF16)
    lo = (r1 - mid.astype(F32)).astype(BF16)
    return hi, mid, lo


def _ada_kernel(c_ref, w_ref, b_ref, o_ref):
    c = c_ref[...]
    c_act = (c * _sigmoid(c)).astype(BF16)
    o_ref[...] = _dot(c_act, w_ref[...].astype(BF16)) + b_ref[...]


def _ada_mod(c_pad, w_ada, b_ada):
    L, D, N = w_ada.shape
    tn = 1536
    return pl.pallas_call(
        _ada_kernel,
        out_shape=jax.ShapeDtypeStruct((L, SUBLANES, N), F32),
        grid=(L, N // tn),
        in_specs=[pl.BlockSpec((SUBLANES, D), lambda l, j: (0, 0)),
                  pl.BlockSpec((None, D, tn), lambda l, j: (l, 0, j)),
                  pl.BlockSpec((None, 1, tn), lambda l, j: (l, 0, j))],
        out_specs=pl.BlockSpec((None, SUBLANES, tn), lambda l, j: (l, 0, j)),
        compiler_params=_cparams(2, 40),
        name="ada_mod",
    )(c_pad, w_ada, b_ada.reshape(L, 1, N))


def _normed(x, g, scale, shift):
    y = x * lax.rsqrt(jnp.mean(x * x, axis=-1, keepdims=True) + RMS_EPS)
    return (y * g) * (1.0 + scale) + shift


def _norm_mod_kernel(x_ref, g_ref, sc_ref, sh_ref, h_ref):
    h_ref[...] = _normed(x_ref[...], g_ref[...], sc_ref[...], sh_ref[...]).astype(BF16)


def _pack_pairs(h):
    half = h.shape[1] // 2
    hi = lax.bitcast_convert_type(h[:, :half].astype(BF16).astype(F32), U32)
    lo = lax.bitcast_convert_type(h[:, half:].astype(BF16).astype(F32), U32)
    return hi | (lo >> 16)


def _unpack_pairs(w):
    a = lax.bitcast_convert_type(w & jnp.uint32(0xFFFF0000), F32).astype(BF16)
    b = lax.bitcast_convert_type(w << 16, F32).astype(BF16)
    return jnp.concatenate([a, b], axis=1)


def _rows_per_token(width):
    return width // LANES


def _store_token_rows(ref, val):
    n, width = val.shape
    sub = _rows_per_token(width)
    for s in range(sub):
        ref[pl.ds(s, n, stride=sub), :] = val[:, s * LANES:(s + 1) * LANES]


def _load_token_rows(ref, first_token, n, width):
    sub = _rows_per_token(width)
    return jnp.concatenate([ref[pl.ds(first_token * sub + s, n, stride=sub), :] for s in range(sub)], axis=1)


def _norm_route_kernel(x_ref, g_ref, sc_ref, sh_ref, wr_ref, hp_ref, lg_ref):
    h = _normed(x_ref[...], g_ref[...], sc_ref[...], sh_ref[...])
    _store_token_rows(hp_ref, _pack_pairs(h))
    lg_ref[...] = _dot(h, wr_ref[...], precision=HIGHEST)


def _norm_mod(x, g, scale, shift, S, w_router_pad=None):
    T, D = x.shape
    tm = 512
    nb = S // tm
    row = lambda i: (i, 0)
    per_batch = pl.BlockSpec((None, 1, D), lambda i: (i // nb, 0, 0))
    in_specs = [pl.BlockSpec((tm, D), row), pl.BlockSpec((1, D), lambda i: (0, 0)), per_batch, per_batch]
    if w_router_pad is None:
        return pl.pallas_call(
            _norm_mod_kernel, out_shape=jax.ShapeDtypeStruct((T, D), BF16), grid=(T // tm,),
            in_specs=in_specs, out_specs=pl.BlockSpec((tm, D), row),
            compiler_params=_cparams(1, 32), name="norm_mod",
        )(x, g, scale, shift)
    return pl.pallas_call(
        _norm_route_kernel,
        out_shape=(jax.ShapeDtypeStruct((T * (D // 2) // LANES, LANES), U32), jax.ShapeDtypeStruct((T, LANES), F32)),
        grid=(T // tm,),
        in_specs=in_specs + [pl.BlockSpec((D, LANES), lambda i: (0, 0))],
        out_specs=(pl.BlockSpec((tm * (D // 2) // LANES, LANES), row), pl.BlockSpec((tm, LANES), row)),
        compiler_params=_cparams(1, 32), name="norm_route",
    )(x, g, scale, shift, w_router_pad)


def _mm_kernel(*refs, cast_w, residual):
    if residual:
        a_ref, w_ref, x_ref, g_ref, o_ref = refs[:5]
        scratch = refs[5:]
    else:
        a_ref, w_ref, o_ref = refs[:3]
        scratch = refs[3:]
    if cast_w:
        wbf_ref, = scratch

        @pl.when(pl.program_id(1) == 0)
        def _():
            wbf_ref[...] = w_ref[...].astype(BF16)

        w = wbf_ref[...]
    else:
        w = w_ref[...]
    acc = _dot(a_ref[...], w)
    if residual:
        acc = x_ref[...] + g_ref[...] * acc
    o_ref[...] = acc.astype(o_ref.dtype)


def _matmul(a, w, tm, tn, out_dtype=F32, residual=None, S=None, vmem_mib=48, layer=None):
    M, K = a.shape
    N = w.shape[-1]
    cast_w = w.dtype != BF16
    if layer is None:
        w_spec = pl.BlockSpec((K, tn), lambda j, i: (0, j))
    else:
        w_spec = pl.BlockSpec((None, K, tn), lambda j, i: (layer, 0, j))
    in_specs = [pl.BlockSpec((tm, K), lambda j, i: (i, 0)), w_spec]
    args = [a, w]
    if residual is not None:
        x, gate = residual
        nb = S // tm
        in_specs += [pl.BlockSpec((tm, tn), lambda j, i: (i, j)),
                     pl.BlockSpec((None, 1, tn), lambda j, i: (i // nb, 0, j))]
        args += [x, gate]
    return pl.pallas_call(
        functools.partial(_mm_kernel, cast_w=cast_w, residual=residual is not None),
        out_shape=jax.ShapeDtypeStruct((M, N), out_dtype),
        grid=(N // tn, M // tm),
        in_specs=in_specs,
        out_specs=pl.BlockSpec((tm, tn), lambda j, i: (i, j)),
        scratch_shapes=[pltpu.VMEM((K, tn), BF16)] if cast_w else [],
        compiler_params=_cparams(2, vmem_mib),
        name="matmul_res" if residual is not None else "matmul",
    )(*args)


IN_TN = 1024
IN_SEGMENTS = ((0, COL_MOBA * LANES // IN_TN, 0),
               (COL_MOBA * LANES // IN_TN, COL_GATES * LANES // IN_TN, N_HEADS),
               (COL_GATES * LANES // IN_TN, N_MAIN_COLS // IN_TN, N_HEADS + GLA_GATE_RANK))
ROW_CHUNK = 256


def _in_proj_kernel(a_ref, w_ref, wn_ref, o_ref, wbf_ref, *, tn):
    j = pl.program_id(0)

    @pl.when(pl.program_id(1) == 0)
    def _():
        K = w_ref.shape[0]
        for lo, hi, shift in IN_SEGMENTS:
            @pl.when(jnp.logical_and(j >= lo, j < hi))
            def _():
                for r in range(0, K, ROW_CHUNK):
                    rows = slice(r, r + ROW_CHUNK)
                    if shift == 0:
                        wbf_ref[rows, :] = w_ref[rows, :].astype(BF16)
                    else:
                        wide = jnp.concatenate([w_ref[rows, :], wn_ref[rows, :]], axis=1)
                        wbf_ref[rows, :] = wide[:, shift:shift + tn].astype(BF16)

    o_ref[...] = _dot(a_ref[...], wbf_ref[...])


def _in_proj(a, w_in, layer, tm=512):
    M, K = a.shape
    tn = IN_TN
    return pl.pallas_call(
        functools.partial(_in_proj_kernel, tn=tn),
        out_shape=jax.ShapeDtypeStruct((M, N_MAIN_COLS), F32),
        grid=(N_MAIN_COLS // tn, M // tm),
        in_specs=[pl.BlockSpec((tm, K), lambda j, i: (i, 0)),
                  pl.BlockSpec((K, tn), lambda j, i: (layer, j)),
                  pl.BlockSpec((K, LANES), lambda j, i: (layer, (j + 1) * (tn // LANES)))],
        out_specs=pl.BlockSpec((tm, tn), lambda j, i: (i, j)),
        scratch_shapes=[pltpu.VMEM((K, tn), BF16)],
        compiler_params=_cparams(2, 48),
        name="in_proj",
    )(a, w_in, w_in)


def _misc_proj_kernel(a_ref, wf_ref, wg_ref, o_ref, wbf_ref):
    @pl.when(pl.program_id(0) == 0)
    def _():
        lane = lax.broadcasted_iota(I32, wf_ref.shape, 1)
        w = jnp.where(lane < MISC_GLR, wf_ref[...],
                      jnp.where(lane < MISC_GLR + GLA_GATE_RANK, wg_ref[...], 0.0))
        wbf_ref[...] = w.astype(BF16)

    o_ref[...] = _dot(a_ref[...], wbf_ref[...])


def _misc_proj(a, w_in, layer, tm=1024):
    M, K = a.shape
    return pl.pallas_call(
        _misc_proj_kernel,
        out_shape=jax.ShapeDtypeStruct((M, LANES), F32),
        grid=(M // tm,),
        in_specs=[pl.BlockSpec((tm, K), lambda i: (i, 0)),
                  pl.BlockSpec((K, LANES), lambda i: (layer, COL_MOBA)),
                  pl.BlockSpec((K, LANES), lambda i: (layer, COL_GATES))],
        out_specs=pl.BlockSpec((tm, LANES), lambda i: (i, 0)),
        scratch_shapes=[pltpu.VMEM((K, LANES), BF16)],
        compiler_params=_cparams(1, 32),
        name="misc_proj",
    )(a, w_in, w_in)


SB_LOG_WEIGHT_FLOOR = -110.0


def _sb_kernel(q_ref, k_ref, v_ref, o_ref, *, blk, scale):
    i = pl.program_id(1)
    row = lax.broadcasted_iota(I32, (blk, blk), 0)
    col = lax.broadcasted_iota(I32, (blk, blk), 1)
    later = (row > col).astype(BF16)

    def cond(state):
        jj, alive, _ = state
        return jnp.logical_and(jj <= i, alive)

    def body(state):
        jj, _, heads = state
        j = i - jj
        start = pl.multiple_of(j * blk, blk)
        past = jnp.logical_or(j < i, col < row)
        new, top = [], None
        for h in range(N_HEADS):
            c, acc = heads[h]
            hs = slice(h * HEAD_DIM, (h + 1) * HEAD_DIM)
            q = (q_ref[:, hs] * scale).astype(BF16)
            k = k_ref[pl.ds(start, blk), hs].astype(BF16)
            v = v_ref[pl.ds(start, blk), hs].astype(BF16)
            z = _dot_nt(q, k)
            ls = jnp.where(past, -_softplus(z), 0.0)
            hi = ls.astype(BF16)
            lo = (ls - hi.astype(F32)).astype(BF16)
            between = _dot(hi, later) + _dot(lo, later)
            w = jnp.where(past, jnp.exp(z + ls + between + c), 0.0)
            acc = acc + _dot(w.astype(BF16), v)
            c = c + jnp.sum(ls, axis=1, keepdims=True)
            new.append((c, acc))
            top = c if top is None else jnp.maximum(top, c)
        return jj + 1, jnp.max(top) > SB_LOG_WEIGHT_FLOOR, tuple(new)

    init = tuple((jnp.zeros((blk, 1), F32), jnp.zeros((blk, HEAD_DIM), F32)) for _ in range(N_HEADS))
    _, _, heads = lax.while_loop(cond, body, (jnp.int32(0), jnp.bool_(True), init))
    for h in range(N_HEADS):
        o_ref[:, h * HEAD_DIM:(h + 1) * HEAD_DIM] = heads[h][1]


def _sb_attention(proj, B, S, col0):
    T = proj.shape[0]
    blk = ATT_BLOCK
    nq = S // blk
    W = BRANCH_WIDTH
    cb = col0 * LANES // W
    return pl.pallas_call(
        functools.partial(_sb_kernel, blk=blk, scale=HEAD_DIM ** -0.5),
        out_shape=jax.ShapeDtypeStruct((T, W), F32),
        grid=(B, nq),
        in_specs=[pl.BlockSpec((blk, W), lambda b, i: (b * nq + i, cb)),
                  pl.BlockSpec((S, W), lambda b, i: (b, cb + 1)),
                  pl.BlockSpec((S, W), lambda b, i: (b, cb + 2))],
        out_specs=pl.BlockSpec((blk, W), lambda b, i: (b * nq + i, 0)),
        compiler_params=_cparams(2, 44),
        name="sb_attention",
    )(proj, proj, proj)


def _flash_kernel(qa_ref, ka_ref, v_ref, o_ref, *, blk):
    i = pl.program_id(1)
    row = lax.broadcasted_iota(I32, (blk, blk), 0)
    col = lax.broadcasted_iota(I32, (blk, blk), 1)
    A = 2 * HEAD_DIM

    def step(j, carry):
        start = pl.multiple_of(j * blk, blk)
        keep = jnp.logical_or(j < i, col <= row)
        new = []
        for h in range(N_HEADS):
            m, l, acc = carry[h]
            s = _dot_nt(qa_ref[:, h * A:(h + 1) * A], ka_ref[pl.ds(start, blk), h * A:(h + 1) * A])
            s = jnp.where(keep, s, NEG_BIG)
            m_new = jnp.maximum(m, jnp.max(s, axis=1, keepdims=True))
            alpha = jnp.exp(m - m_new)
            p = jnp.exp(s - m_new)
            l = alpha * l + jnp.sum(p, axis=1, keepdims=True)
            acc = alpha * acc + _dot(p.astype(BF16), v_ref[pl.ds(start, blk), h * HEAD_DIM:(h + 1) * HEAD_DIM])
            new.append((m_new, l, acc))
        return tuple(new)

    init = tuple((jnp.full((blk, 1), NEG_BIG, F32), jnp.zeros((blk, 1), F32),
                  jnp.zeros((blk, HEAD_DIM), F32)) for _ in range(N_HEADS))
    heads = lax.fori_loop(0, i + 1, step, init)
    for h in range(N_HEADS):
        _, l, acc = heads[h]
        o_ref[:, h * HEAD_DIM:(h + 1) * HEAD_DIM] = acc / l


def _flash_attention(qa, ka, va, B, S):
    T = qa.shape[0]
    blk = ATT_BLOCK
    nq = S // blk
    W = BRANCH_WIDTH
    return pl.pallas_call(
        functools.partial(_flash_kernel, blk=blk),
        out_shape=jax.ShapeDtypeStruct((T, W), F32),
        grid=(B, nq),
        in_specs=[pl.BlockSpec((blk, 2 * W), lambda b, i: (b * nq + i, 0)),
                  pl.BlockSpec((S, 2 * W), lambda b, i: (b, 0)),
                  pl.BlockSpec((S, W), lambda b, i: (b, 0))],
        out_specs=pl.BlockSpec((blk, W), lambda b, i: (b * nq + i, 0)),
        compiler_params=_cparams(2, 40),
        name="flash_attention",
    )(qa, ka, va)


def _fox_prep_kernel(q_ref, k_ref, v_ref, misc_ref, bias_ref, qa_ref, ka_ref, va_ref, carry_ref, *, tb, scale):
    va_ref[...] = v_ref[...].astype(BF16)
    @pl.when(pl.program_id(1) == 0)
    def _():
        carry_ref[...] = jnp.zeros_like(carry_ref)

    lane = lax.broadcasted_iota(I32, (tb, LANES), 1)
    lf = jnp.where(lane < N_HEADS, _log_sigmoid(misc_ref[...] + bias_ref[...]), 0.0)
    row = lax.broadcasted_iota(I32, (tb, tb), 0)
    col = lax.broadcasted_iota(I32, (tb, tb), 1)
    incl = (col <= row).astype(F32)
    F = _dot(incl, lf, precision=HIGHEST) + carry_ref[0:1, :]
    carry_ref[0:1, :] = F[tb - 1:tb, :]
    for h in range(N_HEADS):
        Fh = jnp.broadcast_to(F[:, h:h + 1], (tb, LANES))
        hi, mid, lo = (p.astype(F32) for p in _split3(Fh))
        ones = jnp.ones((tb, LANES), F32)
        zeros = jnp.zeros((tb, LANES), F32)
        eq = jnp.where(lane == 0, hi, jnp.where(lane == 1, mid, jnp.where(lane == 2, lo,
             jnp.where(lane < 6, ones, zeros))))
        ek = jnp.where(lane < 3, ones, jnp.where(lane == 3, -hi, jnp.where(lane == 4, -mid,
             jnp.where(lane == 5, -lo, zeros))))
        hs = slice(h * HEAD_DIM, (h + 1) * HEAD_DIM)
        qa_ref[:, 2 * h * HEAD_DIM:(2 * h + 1) * HEAD_DIM] = (q_ref[:, hs] * scale).astype(BF16)
        qa_ref[:, (2 * h + 1) * HEAD_DIM:(2 * h + 2) * HEAD_DIM] = eq.astype(BF16)
        ka_ref[:, 2 * h * HEAD_DIM:(2 * h + 1) * HEAD_DIM] = k_ref[:, hs].astype(BF16)
        ka_ref[:, (2 * h + 1) * HEAD_DIM:(2 * h + 2) * HEAD_DIM] = ek.astype(BF16)


def _fox_prep(proj, misc, fox_bias_pad, B, S):
    T = proj.shape[0]
    tb = 512
    nb = S // tb
    qblk = COL_FOX * LANES // BRANCH_WIDTH
    out = jax.ShapeDtypeStruct((T, 2 * BRANCH_WIDTH), BF16)
    return pl.pallas_call(
        functools.partial(_fox_prep_kernel, tb=tb, scale=HEAD_DIM ** -0.5),
        out_shape=(out, out, jax.ShapeDtypeStruct((T, BRANCH_WIDTH), BF16)),
        grid=(B, nb),
        in_specs=[pl.BlockSpec((tb, BRANCH_WIDTH), lambda b, i: (b * nb + i, qblk)),
                  pl.BlockSpec((tb, BRANCH_WIDTH), lambda b, i: (b * nb + i, qblk + 1)),
                  pl.BlockSpec((tb, BRANCH_WIDTH), lambda b, i: (b * nb + i, qblk + 2)),
                  pl.BlockSpec((tb, LANES), lambda b, i: (b * nb + i, 0)),
                  pl.BlockSpec((1, LANES), lambda b, i: (0, 0))],
        out_specs=(pl.BlockSpec((tb, 2 * BRANCH_WIDTH), lambda b, i: (b * nb + i, 0)),
                   pl.BlockSpec((tb, 2 * BRANCH_WIDTH), lambda b, i: (b * nb + i, 0)),
                   pl.BlockSpec((tb, BRANCH_WIDTH), lambda b, i: (b * nb + i, 0))),
        scratch_shapes=[pltpu.VMEM((SUBLANES, LANES), F32)],
        compiler_params=_cparams(2, 32),
        name="fox_prep",
    )(proj, proj, proj, misc, fox_bias_pad)


def _rope(x, cos, sin, lane):
    half = ROPE_DIM // 2
    up = pltpu.roll(x, half, 1)
    down = pltpu.roll(x, LANES - half, 1)
    rot = jnp.where(lane < half, -down * sin, jnp.where(lane < ROPE_DIM, up * sin, 0.0))
    return x * jnp.where(lane < ROPE_DIM, cos, 1.0) + rot


def _angles(pos_ref, freq_ref):
    ang = pos_ref[...].astype(F32) * freq_ref[...]
    return jnp.cos(ang), jnp.sin(ang)


def _moba_k_kernel(k_ref, v_ref, pos_ref, freq_ref, ka_ref, va_ref, kmean_ref, *, tb):
    i = pl.program_id(1)
    va_ref[...] = v_ref[...].astype(BF16)

    @pl.when(i == 0)
    def _():
        kmean_ref[...] = jnp.zeros_like(kmean_ref)

    lane = lax.broadcasted_iota(I32, (tb, LANES), 1)
    cos, sin = _angles(pos_ref, freq_ref)
    onehot = jnp.where(lane == i, 1.0, 0.0).astype(BF16)
    this_row = lax.broadcasted_iota(I32, (LANES, HEAD_DIM), 0) == i
    for h in range(N_HEADS):
        hs = slice(h * HEAD_DIM, (h + 1) * HEAD_DIM)
        kr = _rope(k_ref[:, hs], cos, sin, lane)
        mean = jnp.sum(kr, axis=0, keepdims=True) * (1.0 / tb)
        kmean_ref[:, hs] = jnp.where(this_row, mean, kmean_ref[:, hs])
        ka_ref[:, 2 * h * HEAD_DIM:(2 * h + 1) * HEAD_DIM] = kr.astype(BF16)
        ka_ref[:, (2 * h + 1) * HEAD_DIM:(2 * h + 2) * HEAD_DIM] = onehot


def _moba_q_kernel(q_ref, pos_ref, freq_ref, kmean_ref, qa_ref, *, tb, scale):
    i = pl.program_id(1)
    lane = lax.broadcasted_iota(I32, (tb, LANES), 1)
    cos, sin = _angles(pos_ref, freq_ref)
    neg_inf = jnp.float32(-jnp.inf)
    for h in range(N_HEADS):
        hs = slice(h * HEAD_DIM, (h + 1) * HEAD_DIM)
        qr = _rope(q_ref[:, hs], cos, sin, lane)
        gate = _dot_nt(qr, kmean_ref[:, hs], precision=HIGHEST)
        cur = jnp.where(lane < i, gate, neg_inf)
        chosen = lane == i
        for _ in range(MOBA_TOPK):
            m = jnp.max(cur, axis=1, keepdims=True)
            first = jnp.min(jnp.where(jnp.logical_and(cur == m, m > neg_inf), lane, LANES),
                            axis=1, keepdims=True)
            pick = lane == first
            chosen = jnp.logical_or(chosen, pick)
            cur = jnp.where(pick, neg_inf, cur)
        bias = jnp.where(jnp.logical_or(chosen, lane >= LANES // 2), 0.0, NEG_BIG)
        qa_ref[:, 2 * h * HEAD_DIM:(2 * h + 1) * HEAD_DIM] = (qr * scale).astype(BF16)
        qa_ref[:, (2 * h + 1) * HEAD_DIM:(2 * h + 2) * HEAD_DIM] = bias.astype(BF16)


def _moba_prep(proj, pos_col, freq_lanes, B, S):
    T = proj.shape[0]
    tb = MOBA_BLOCK
    nb = S // tb
    qblk = COL_MOBA * LANES // BRANCH_WIDTH
    aug = jax.ShapeDtypeStruct((T, 2 * BRANCH_WIDTH), BF16)
    row_spec = lambda c: pl.BlockSpec((tb, BRANCH_WIDTH), lambda b, i: (b * nb + i, c))
    pos_spec = pl.BlockSpec((tb, 1), lambda b, i: (b * nb + i, 0))
    freq_spec = pl.BlockSpec((1, LANES), lambda b, i: (0, 0))
    aug_spec = pl.BlockSpec((tb, 2 * BRANCH_WIDTH), lambda b, i: (b * nb + i, 0))
    kmean_spec = pl.BlockSpec((None, LANES, BRANCH_WIDTH), lambda b, i: (b, 0, 0))
    ka, va, kmean = pl.pallas_call(
        functools.partial(_moba_k_kernel, tb=tb),
        out_shape=(aug, jax.ShapeDtypeStruct((T, BRANCH_WIDTH), BF16),
                   jax.ShapeDtypeStruct((B, LANES, BRANCH_WIDTH), F32)),
        grid=(B, nb),
        in_specs=[row_spec(qblk + 1), row_spec(qblk + 2), pos_spec, freq_spec],
        out_specs=(aug_spec, pl.BlockSpec((tb, BRANCH_WIDTH), lambda b, i: (b * nb + i, 0)), kmean_spec),
        compiler_params=_cparams(2, 32),
        name="moba_k_prep",
    )(proj, proj, pos_col, freq_lanes)
    qa = pl.pallas_call(
        functools.partial(_moba_q_kernel, tb=tb, scale=HEAD_DIM ** -0.5),
        out_shape=aug,
        grid=(B, nb),
        in_specs=[row_spec(qblk), pos_spec, freq_spec, kmean_spec],
        out_specs=aug_spec,
        compiler_params=_cparams(2, 32),
        name="moba_q_prep",
    )(proj, pos_col, freq_lanes, kmean)
    return qa, ka, va


def _gla_kernel(q_ref, k_ref, v_ref, gr_ref, misc_ref, wg_ref, bg_ref, gn_ref, o_ref, state_ref, *, tb):
    C = GLA_CHUNK
    @pl.when(pl.program_id(1) == 0)
    def _():
        state_ref[...] = jnp.zeros_like(state_ref)

    la_all = _log_sigmoid(_dot(misc_ref[...], wg_ref[...], precision=HIGHEST) + bg_ref[...]) \
        * (1.0 / GLA_GATE_NORMALIZER)
    rowc = lax.broadcasted_iota(I32, (C, C), 0)
    colc = lax.broadcasted_iota(I32, (C, C), 1)
    incl = (colc <= rowc).astype(F32)
    causal = colc <= rowc
    lane = lax.broadcasted_iota(I32, (C, LANES), 1)
    row2 = lax.broadcasted_iota(I32, (2 * GLA_HEAD_K, LANES), 0)
    ones_cv = jnp.ones((C, LANES), F32)
    qscale = GLA_HEAD_K ** -0.5
    gn = gn_ref[...]
    for c in range(tb // C):
        rs = slice(c * C, (c + 1) * C)
        for p in range(2):
            ps = slice(p * LANES, (p + 1) * LANES)
            la = la_all[rs, ps]
            b = _dot(incl, la, precision=HIGHEST)
            b_last = b[C - 1:C, :]
            eb = jnp.exp(b)
            q_dec = q_ref[rs, ps] * qscale * eb
            k = k_ref[rs, ps]
            k_inv = (k * jnp.exp(-b)).astype(BF16)
            k_end = (k * jnp.exp(b_last - b)).astype(BF16)
            decay = jnp.exp(_dot_tn(la, ones_cv, precision=HIGHEST))
            st = state_ref[p * LANES:(p + 1) * LANES, :]
            new_rows = []
            for e in range(2):
                h = 2 * p + e
                own = jnp.logical_and(lane >= e * GLA_HEAD_K, lane < (e + 1) * GLA_HEAD_K)
                qh = jnp.where(own, q_dec, 0.0).astype(BF16)
                v = v_ref[rs, h * HEAD_DIM:(h + 1) * HEAD_DIM].astype(BF16)
                attn = jnp.where(causal, _dot_nt(qh, k_inv), 0.0)
                st_h = jnp.where(jnp.logical_and(row2 >= e * GLA_HEAD_K, row2 < (e + 1) * GLA_HEAD_K), st, 0.0)
                o = _dot(attn.astype(BF16), v) + _dot(qh, st_h.astype(BF16))
                y = o * lax.rsqrt(jnp.mean(o * o, axis=-1, keepdims=True) + RMS_EPS) * gn
                g = gr_ref[rs, h * HEAD_DIM:(h + 1) * HEAD_DIM]
                o_ref[rs, h * HEAD_DIM:(h + 1) * HEAD_DIM] = y * (g * _sigmoid(g))
                new_rows.append(_dot_tn(k_end, v))
            kv = jnp.where(row2 < GLA_HEAD_K, new_rows[0], new_rows[1])
            state_ref[p * LANES:(p + 1) * LANES, :] = decay * st + kv


def _gla(proj, misc, w_gate_pad, b_gate, gla_norm, B, S):
    T = proj.shape[0]
    tb = 512
    nb = S // tb
    kd = 2 * LANES
    rows = lambda w, c: pl.BlockSpec((tb, w), lambda b, i: (b * nb + i, c))
    const = lambda shape: pl.BlockSpec(shape, lambda b, i: (0, 0))
    return pl.pallas_call(
        functools.partial(_gla_kernel, tb=tb),
        out_shape=jax.ShapeDtypeStruct((T, BRANCH_WIDTH), F32),
        grid=(B, nb),
        in_specs=[rows(kd, COL_GQ * LANES // kd), rows(kd, COL_GK * LANES // kd),
                  rows(BRANCH_WIDTH, COL_GV * LANES // BRANCH_WIDTH),
                  rows(BRANCH_WIDTH, COL_GR * LANES // BRANCH_WIDTH),
                  rows(LANES, 0), const((LANES, kd)), const((1, kd)), const((1, HEAD_DIM))],
        out_specs=rows(BRANCH_WIDTH, 0),
        scratch_shapes=[pltpu.VMEM((kd, HEAD_DIM), F32)],
        compiler_params=_cparams(2, 32),
        name="gla",
    )(proj, proj, proj, proj, misc, w_gate_pad, b_gate, gla_norm)


def _merge_kernel(oa_ref, ob_ref, oc_ref, od_ref, g0_ref, g1_ref, g2_ref, g3_ref, w_ref, o_ref, wbf_ref):
    @pl.when(pl.program_id(1) == 0)
    def _():
        wbf_ref[...] = w_ref[...].astype(BF16)

    acc = None
    for n, (b_ref, g_ref) in enumerate(((oa_ref, g0_ref), (ob_ref, g1_ref), (oc_ref, g2_ref), (od_ref, g3_ref))):
        term = _sigmoid(g_ref[...]) * _dot(b_ref[...].astype(BF16), wbf_ref[n])
        acc = term if acc is None else acc + term
    o_ref[...] = acc.astype(BF16)


def _merge(branches, proj, w_branch, layer):
    T = proj.shape[0]
    _, nbr, W, D = w_branch.shape
    tm, tn = 512, 512
    g0 = COL_GATES * LANES // tn
    per = D // tn
    br_spec = pl.BlockSpec((tm, W), lambda j, i: (i, 0))
    gate_spec = lambda n: pl.BlockSpec((tm, tn), lambda j, i: (i, g0 + n * per + j))
    return pl.pallas_call(
        _merge_kernel,
        out_shape=jax.ShapeDtypeStruct((T, D), BF16),
        grid=(D // tn, T // tm),
        in_specs=[br_spec] * 4 + [gate_spec(n) for n in range(4)]
                 + [pl.BlockSpec((None, nbr, W, tn), lambda j, i: (layer, 0, 0, j))],
        out_specs=pl.BlockSpec((tm, tn), lambda j, i: (i, j)),
        scratch_shapes=[pltpu.VMEM((nbr, W, tn), BF16)],
        compiler_params=_cparams(2, 40),
        name="branch_merge",
    )(*branches, proj, proj, proj, proj, w_branch)


def _butterfly(x, lane, op):
    for s in (1, 2, 4):
        up = pltpu.roll(x, s, 1)
        down = pltpu.roll(x, LANES - s, 1)
        x = op(x, jnp.where((lane & s) != 0, up, down))
    return x


def _route_kernel(lg_ref, bias_ref, e8_ref, w8_ref, p8_ref, cnt_ref, carry_ref, *, tm):
    @pl.when(pl.program_id(0) == 0)
    def _():
        carry_ref[...] = jnp.zeros_like(carry_ref)

    neg_inf = jnp.float32(-jnp.inf)
    lane = lax.broadcasted_iota(I32, (tm, LANES), 1)
    valid = lane < N_EXPERTS
    scores = _sigmoid(lg_ref[...])
    biased = jnp.where(valid, scores + bias_ref[...], neg_inf)
    g1 = _butterfly(biased, lane, jnp.maximum)
    first = _butterfly(jnp.where(biased == g1, lane, LANES), lane, jnp.minimum)
    g2 = _butterfly(jnp.where(lane == first, neg_inf, biased), lane, jnp.maximum)
    gs = g1 + g2
    gs = jnp.where(valid, gs, pltpu.roll(gs, N_EXPERTS, 1))
    gidx = lane >> 3
    beaten = jnp.zeros((tm, LANES), I32)
    for r in range(1, N_EXPERTS // GROUP_SIZE):
        other = pltpu.roll(gs, GROUP_SIZE * r, 1)
        og = (gidx - r) & (N_EXPERTS // GROUP_SIZE - 1)
        wins = jnp.logical_or(other > gs, jnp.logical_and(other == gs, og < gidx))
        beaten = beaten + wins.astype(I32)
    cur = jnp.where(jnp.logical_and(beaten < TOPK_GROUPS, valid), biased, neg_inf)
    sel = jnp.zeros((tm, LANES), jnp.bool_)
    for _ in range(TOP_K):
        m = jnp.max(cur, axis=1, keepdims=True)
        pick = lane == jnp.min(jnp.where(cur == m, lane, LANES), axis=1, keepdims=True)
        sel = jnp.logical_or(sel, pick)
        cur = jnp.where(pick, neg_inf, cur)
    wsel = jnp.where(sel, scores, 0.0)
    wd = wsel / jnp.sum(wsel, axis=1, keepdims=True) * ROUTED_SCALE
    selb = jnp.where(sel, 1.0, 0.0).astype(BF16)
    row = lax.broadcasted_iota(I32, (tm, tm), 0)
    col = lax.broadcasted_iota(I32, (tm, tm), 1)
    pos = _dot((col < row).astype(BF16), selb) + carry_ref[0:1, :]
    total = carry_ref[0:1, :] + jnp.sum(selb.astype(F32), axis=0, keepdims=True)
    carry_ref[0:1, :] = total
    cnt_ref[...] = jnp.broadcast_to(total, cnt_ref.shape)
    r2 = lax.broadcasted_iota(I32, (LANES, LANES), 0)
    c2 = lax.broadcasted_iota(I32, (LANES, LANES), 1)
    slot = _dot(selb, (r2 < c2).astype(BF16))
    lane_f = lane.astype(F32)
    e8 = jnp.zeros((tm, LANES), F32)
    w8 = jnp.zeros((tm, LANES), F32)
    p8 = jnp.zeros((tm, LANES), F32)
    for k in range(TOP_K):
        mk = jnp.logical_and(sel, slot == k)
        put = lane == k
        e8 = jnp.where(put, jnp.sum(jnp.where(mk, lane_f, 0.0), axis=1, keepdims=True), e8)
        w8 = jnp.where(put, jnp.sum(jnp.where(mk, wd, 0.0), axis=1, keepdims=True), w8)
        p8 = jnp.where(put, jnp.sum(jnp.where(mk, pos, 0.0), axis=1, keepdims=True), p8)
    e8_ref[...] = e8.astype(I32)
    w8_ref[...] = w8
    p8_ref[...] = p8.astype(I32)


def _route(logits, router_bias_pad):
    T = logits.shape[0]
    tm = 512
    row = pl.BlockSpec((tm, LANES), lambda i: (i, 0))
    return pl.pallas_call(
        functools.partial(_route_kernel, tm=tm),
        out_shape=(jax.ShapeDtypeStruct((T, LANES), I32), jax.ShapeDtypeStruct((T, LANES), F32),
                   jax.ShapeDtypeStruct((T, LANES), I32), jax.ShapeDtypeStruct((SUBLANES, LANES), F32)),
        grid=(T // tm,),
        in_specs=[row, pl.BlockSpec((1, LANES), lambda i: (0, 0))],
        out_specs=(row, row, row, pl.BlockSpec((SUBLANES, LANES), lambda i: (0, 0))),
        scratch_shapes=[pltpu.VMEM((SUBLANES, LANES), F32)],
        compiler_params=_cparams(1, 32),
        name="route",
    )(logits, router_bias_pad)


def _token_copy(src, src_token, dst, dst_token, sem, sub):
    return pltpu.make_async_copy(src.at[pl.ds(pl.multiple_of(src_token * sub, sub), sub), :],
                                 dst.at[pl.ds(pl.multiple_of(dst_token * sub, sub), sub), :], sem)


def _dispatch_kernel(pad_end_ref, padded_ref, dest_ref, hp_ref, xs_ref, zero_ref, sem_ref, *, tm, rows, n_blocks, sub):
    @pl.when(pl.program_id(0) == 0)
    def _():
        zero_ref[...] = jnp.zeros_like(zero_ref)

        def fill(e, do_wait):
            @pl.when(padded_ref[e] > 0)
            def _():
                start = pl.multiple_of((pad_end_ref[e] - rows) * sub, rows * sub)
                cp = pltpu.make_async_copy(zero_ref, xs_ref.at[pl.ds(start, rows * sub), :], sem_ref)
                if do_wait:
                    cp.wait()
                else:
                    cp.start()

        lax.fori_loop(0, N_EXPERTS, lambda e, c: (fill(e, False), c)[1], 0)
        lax.fori_loop(0, N_EXPERTS, lambda e, c: (fill(e, True), c)[1], 0)

        def tail(b):
            return pltpu.make_async_copy(zero_ref, xs_ref.at[pl.ds(pl.multiple_of(b * rows * sub, rows * sub), rows * sub), :],
                                         sem_ref)

        n_used = pad_end_ref[N_EXPERTS - 1] // rows
        lax.fori_loop(n_used, n_blocks, lambda b, c: (tail(b).start(), c)[1], 0)
        lax.fori_loop(n_used, n_blocks, lambda b, c: (tail(b).wait(), c)[1], 0)

    def issue(t, c):
        for k in range(TOP_K):
            _token_copy(hp_ref, t, xs_ref, dest_ref[t * TOP_K + k], sem_ref, sub).start(priority=k % 2)
        return c

    def drain(t, c):
        for k in range(TOP_K):
            _token_copy(hp_ref, t, xs_ref, dest_ref[t * TOP_K + k], sem_ref, sub).wait()
        return c

    lax.fori_loop(0, tm, issue, 0)
    lax.fori_loop(0, tm, drain, 0)


def _dispatch(hp, dest_flat, pad_end, padded, n_rows, sub):
    tm = TOKEN_TILE
    T = hp.shape[0] // sub
    grid_spec = pltpu.PrefetchScalarGridSpec(
        num_scalar_prefetch=2,
        grid=(T // tm,),
        in_specs=[pl.BlockSpec((tm * TOP_K,), lambda i, pe, pd: (i,), memory_space=pltpu.SMEM),
                  pl.BlockSpec((tm * sub, LANES), lambda i, pe, pd: (i, 0))],
        out_specs=pl.BlockSpec(memory_space=pl.ANY),
        scratch_shapes=[pltpu.VMEM((EXPERT_ROWS * sub, LANES), U32), pltpu.SemaphoreType.DMA(())],
    )
    return pl.pallas_call(
        functools.partial(_dispatch_kernel, tm=tm, rows=EXPERT_ROWS, n_blocks=n_rows // EXPERT_ROWS, sub=sub),
        out_shape=jax.ShapeDtypeStruct((n_rows * sub, LANES), U32),
        grid_spec=grid_spec,
        compiler_params=_cparams(1, 32),
        name="dispatch",
    )(pad_end, padded, dest_flat, hp)


def _expert_kernel(blk_e_ref, next_e_ref, n_used_ref, x_ref, wg_hbm, wu_hbm, wd_hbm, y_ref,
                   wg_f32, wu_f32, wd_f32, wg_bf, wu_bf, wd_bf, sem_ref, slot_ref, *, layer, rows, width):
    i = pl.program_id(0)
    e = blk_e_ref[i]
    active = i < n_used_ref[0]
    first = jnp.logical_or(i == 0, e != blk_e_ref[jnp.maximum(i - 1, 0)])

    def fetch(expert, slot):
        return [pltpu.make_async_copy(w.at[layer, expert], buf.at[slot], sem_ref.at[slot])
                for w, buf in ((wg_hbm, wg_f32), (wu_hbm, wu_f32), (wd_hbm, wd_f32))]

    @pl.when(i == 0)
    def _():
        slot_ref[0] = 0
        for cp in fetch(e, 0):
            cp.start()

    @pl.when(jnp.logical_and(active, first))
    def _():
        slot = slot_ref[0]
        for cp in fetch(e, slot):
            cp.wait()
        nxt = next_e_ref[i]

        @pl.when(nxt >= 0)
        def _():
            for cp in fetch(nxt, 1 - slot):
                cp.start()

        wg_bf[...] = wg_f32[slot].astype(BF16)
        wu_bf[...] = wu_f32[slot].astype(BF16)
        wd_bf[...] = wd_f32[slot].astype(BF16)
        slot_ref[0] = 1 - slot

    @pl.when(active)
    def _():
        x = _unpack_pairs(_load_token_rows(x_ref, 0, rows, width))
        g = _dot(x, wg_bf[...])
        u = _dot(x, wu_bf[...])
        hb = (g * _sigmoid(g)) * u
        _store_token_rows(y_ref, _pack_pairs(_dot(hb.astype(BF16), wd_bf[...])))

    @pl.when(jnp.logical_not(active))
    def _():
        y_ref[...] = jnp.zeros_like(y_ref)


def _experts(xs, blk_e, next_e, n_used, w_gate, w_up, w_down, layer):
    _, E, D, Hx = w_gate.shape
    W = D // 2
    sub = _rows_per_token(W)
    M = EXPERT_ROWS
    nblk = xs.shape[0] // (M * sub)
    xrow = lambda i, be, ne, nu: (jnp.minimum(i, nu[0] - 1), 0)
    hbm = pl.BlockSpec(memory_space=pl.ANY)
    grid_spec = pltpu.PrefetchScalarGridSpec(
        num_scalar_prefetch=3,
        grid=(nblk,),
        in_specs=[pl.BlockSpec((M * sub, LANES), xrow), hbm, hbm, hbm],
        out_specs=pl.BlockSpec((M * sub, LANES), lambda i, be, ne, nu: (i, 0)),
        scratch_shapes=[pltpu.VMEM((2, D, Hx), F32), pltpu.VMEM((2, D, Hx), F32), pltpu.VMEM((2, Hx, D), F32),
                        pltpu.VMEM((D, Hx), BF16), pltpu.VMEM((D, Hx), BF16), pltpu.VMEM((Hx, D), BF16),
                        pltpu.SemaphoreType.DMA((2,)), pltpu.SMEM((1,), I32)],
    )
    return pl.pallas_call(
        functools.partial(_expert_kernel, layer=layer, rows=M, width=W),
        out_shape=jax.ShapeDtypeStruct(xs.shape, U32),
        grid_spec=grid_spec,
        compiler_params=_cparams(1, 52),
        name="experts",
    )(blk_e, next_e, n_used, xs, w_gate, w_up, w_down)


def _next_expert(blk_e, cnt):
    E = cnt.shape[0]
    idx = jnp.where(cnt > 0, jnp.arange(E, dtype=I32), E)
    later = jnp.concatenate([lax.cummin(idx, axis=0, reverse=True)[1:], jnp.full((1,), E, I32)])
    return jnp.where(later < E, later, -1)[blk_e]


def _unpack_pairs_f32(w):
    a = lax.bitcast_convert_type(w & jnp.uint32(0xFFFF0000), F32)
    b = lax.bitcast_convert_type(w << 16, F32)
    return jnp.concatenate([a, b], axis=1)


def _combine_kernel(dcur_ref, dnext_ref, x_ref, ysh_ref, w8_ref, g2_ref, fn_ref, ys_ref, o_ref, buf_ref, sem_ref,
                    *, tm, n_steps, width, final_norm):
    i = pl.program_id(0)
    slot = lax.rem(i, 2)
    sub = _rows_per_token(width)

    def issue(dref, s):
        def body(t, c):
            for k in range(TOP_K):
                _token_copy(ys_ref, dref[t * TOP_K + k], buf_ref.at[s], k * tm + t, sem_ref.at[s], sub).start(priority=k % 2)
            return c
        lax.fori_loop(0, tm, body, 0)

    @pl.when(i == 0)
    def _():
        issue(dcur_ref, 0)

    @pl.when(i + 1 < n_steps)
    def _():
        issue(dnext_ref, 1 - slot)

    def drain(t, c):
        for k in range(TOP_K):
            _token_copy(ys_ref, 0, buf_ref.at[slot], 0, sem_ref.at[slot], sub).wait()
        return c

    lax.fori_loop(0, tm, drain, 0)
    w8 = w8_ref[...]
    y = _unpack_pairs_f32(_load_token_rows(ysh_ref, 0, tm, width))
    gathered = buf_ref.at[slot]
    for k in range(TOP_K):
        y = y + w8[:, k:k + 1] * _unpack_pairs_f32(_load_token_rows(gathered, k * tm, tm, width))
    out = x_ref[...] + g2_ref[...] * y
    if final_norm:
        out = out * lax.rsqrt(jnp.mean(out * out, axis=-1, keepdims=True) + RMS_EPS) * fn_ref[...]
    o_ref[...] = out


def _combine(x, ysh, ys, dest_flat, w8, gate2, final_g, S, final_norm):
    T, D = x.shape
    W = D // 2
    sub = _rows_per_token(W)
    tm = TOKEN_TILE
    nb = S // tm
    n_steps = T // tm
    row = lambda i: (i, 0)
    return pl.pallas_call(
        functools.partial(_combine_kernel, tm=tm, n_steps=n_steps, width=W, final_norm=final_norm),
        out_shape=jax.ShapeDtypeStruct((T, D), F32),
        grid=(n_steps,),
        in_specs=[pl.BlockSpec((tm * TOP_K,), lambda i: (i,), memory_space=pltpu.SMEM),
                  pl.BlockSpec((tm * TOP_K,), lambda i: (jnp.minimum(i + 1, n_steps - 1),), memory_space=pltpu.SMEM),
                  pl.BlockSpec((tm, D), row), pl.BlockSpec((tm * sub, LANES), row),
                  pl.BlockSpec((tm, LANES), row),
                  pl.BlockSpec((None, 1, D), lambda i: (i // nb, 0, 0)),
                  pl.BlockSpec((1, D), lambda i: (0, 0)),
                  pl.BlockSpec(memory_space=pl.ANY)],
        out_specs=pl.BlockSpec((tm, D), row),
        scratch_shapes=[pltpu.VMEM((2, TOP_K * tm * sub, LANES), U32), pltpu.SemaphoreType.DMA((2,))],
        compiler_params=_cparams(1, 32),
        name="combine",
    )(dest_flat, dest_flat, x, ysh, w8, gate2, final_g, ys)


def _pad_lanes(v, offset=0):
    out = jnp.zeros((1, LANES), F32)
    return out.at[0, offset:offset + v.shape[0]].set(v.astype(F32))


def kernel(x, c, positions, attn_norm, w_ada, b_ada, w_in, fox_bias, gla_w_gate, gla_b_gate, gla_norm,
           w_branch, w_out, ffn_norm, w_router, router_bias, w_exp_gate, w_exp_up, w_exp_down,
           w_sh_gate, w_sh_up, w_sh_down, final_norm):
    B, S, D = x.shape
    L = w_ada.shape[0]
    T = B * S
    E = N_EXPERTS
    M = EXPERT_ROWS
    xf = x.reshape(T, D)

    c_pad = jnp.zeros((SUBLANES, D), F32).at[:B].set(c)
    mod = _ada_mod(c_pad, w_ada, b_ada)

    half = ROPE_DIM // 2
    inv_freq = jnp.power(ROPE_THETA, -jnp.arange(half, dtype=F32) * 2.0 / ROPE_DIM)
    freq_lanes = _pad_lanes(jnp.concatenate([inv_freq, inv_freq]))
    pos_col = positions.reshape(T, 1)

    w_in2 = w_in.reshape(L * D, w_in.shape[2])
    n_blocks = (T * TOP_K + E * (M - 1) + M - 1) // M
    P = n_blocks * M

    for l in range(L):
        m6 = mod[l, :B].reshape(B, 6, 1, D)
        shift1, scale1, gate1, shift2, scale2, gate2 = (m6[:, n] for n in range(6))

        h1 = _norm_mod(xf, attn_norm[l].reshape(1, D), scale1, shift1, S)
        proj = _in_proj(h1, w_in2, l)
        misc = _misc_proj(h1, w_in2, l)

        o_a = _sb_attention(proj, B, S, COL_SB)
        qa, ka, va = _fox_prep(proj, misc, _pad_lanes(fox_bias[l], MISC_FOX_F), B, S)
        o_b = _flash_attention(qa, ka, va, B, S)
        qa, ka, va = _moba_prep(proj, pos_col, freq_lanes, B, S)
        o_c = _flash_attention(qa, ka, va, B, S)
        wg_pad = jnp.zeros((LANES, gla_w_gate.shape[2]), F32).at[MISC_GLR:MISC_GLR + GLA_GATE_RANK].set(gla_w_gate[l])
        o_d = _gla(proj, misc, wg_pad, gla_b_gate[l].reshape(1, -1), gla_norm[l].reshape(1, -1), B, S)

        merged = _merge((o_a, o_b, o_c, o_d), proj, w_branch, l)
        xf = _matmul(merged, w_out, tm=512, tn=1024, residual=(xf, gate1), S=S, layer=l)

        wr_pad = jnp.zeros((D, LANES), F32).at[:, :E].set(w_router[l])
        hp, logits = _norm_mod(xf, ffn_norm[l].reshape(1, D), scale2, shift2, S, w_router_pad=wr_pad)
        e8, w8, p8, counts = _route(logits, _pad_lanes(router_bias[l]))
        cnt = counts[0, :E].astype(I32)
        padded = (cnt + M - 1) // M * M
        pad_end = jnp.cumsum(padded)
        pad_start = pad_end - padded
        dest = (pad_start[e8[:, :TOP_K]] + p8[:, :TOP_K]).reshape(T * TOP_K)
        blk_start = jnp.arange(n_blocks, dtype=I32) * M
        blk_e = jnp.minimum(jnp.sum((pad_end[None, :] <= blk_start[:, None]).astype(I32), axis=1), E - 1)
        n_used = (pad_end[E - 1:] // M).astype(I32)

        xs = _dispatch(hp, dest, pad_end.astype(I32), padded.astype(I32), P, _rows_per_token(D // 2))
        ys = _experts(xs, blk_e, _next_expert(blk_e, cnt), n_used, w_exp_gate, w_exp_up, w_exp_down, l)
        ysh = _experts(hp, jnp.zeros((T // M,), I32), jnp.full((T // M,), -1, I32), jnp.full((1,), T // M, I32),
                       w_sh_gate[:, None], w_sh_up[:, None], w_sh_down[:, None], l)
        last = l == L - 1
        xf = _combine(xf, ysh, ys, dest, w8, gate2, final_norm.reshape(1, D), S, final_norm=last)

    return xf.reshape(B, S, D)
```

```python
import functools

import jax
import jax.numpy as jnp
from jax import lax
from jax.experimental import pallas as pl
from jax.experimental.pallas import tpu as pltpu

F32 = jnp.float32
BF16 = jnp.bfloat16
I32 = jnp.int32
U32 = jnp.uint32

HEAD_DIM = 128
N_HEADS = 4
BRANCH_WIDTH = N_HEADS * HEAD_DIM
GLA_HEAD_K = 64
GLA_CHUNK = 64
GLA_GATE_RANK = 16
GLA_GATE_NORMALIZER = 16.0
MOBA_BLOCK = 256
MOBA_TOPK = 3
ROPE_THETA = 500000.0
ROPE_DIM = HEAD_DIM // 4
N_EXPERTS = 64
GROUP_SIZE = 8
TOPK_GROUPS = 4
TOP_K = 8
ROUTED_SCALE = 2.5
RMS_EPS = 1e-6

LANES = 128
SUBLANES = 8
MIB = 1024 * 1024
NEG_BIG = -1e30

ATT_BLOCK = 256
EXPERT_ROWS = 256
TOKEN_TILE = 128

COL_SB, COL_FOX, COL_MOBA = 0, 12, 24
COL_GQ, COL_GK, COL_GV, COL_GR, COL_GATES = 36, 38, 40, 44, 48
N_MAIN_COLS = 112 * LANES
MISC_FOX_F, MISC_GLR = 0, 4


def _cparams(n_axes, vmem_mib):
    return pltpu.CompilerParams(dimension_semantics=("arbitrary",) * n_axes,
                                vmem_limit_bytes=vmem_mib * MIB)


def _softplus(z):
    return jnp.maximum(z, 0.0) + jnp.log(1.0 + jnp.exp(-jnp.abs(z)))


def _log_sigmoid(z):
    return -_softplus(-z)


def _sigmoid(z):
    return 1.0 / (1.0 + jnp.exp(-z))


def _dot(a, b, precision=None):
    return jnp.dot(a, b, preferred_element_type=F32, precision=precision)


def _dot_nt(a, b, precision=None):
    return lax.dot_general(a, b, (((1,), (1,)), ((), ())), preferred_element_type=F32,
                           precision=precision)


def _dot_tn(a, b, precision=None):
    return lax.dot_general(a, b, (((0,), (0,)), ((), ())), preferred_element_type=F32,
                           precision=precision)


HIGHEST = lax.Precision.HIGHEST


def _split3(x):
    hi = x.astype(BF16)
    r1 = x - hi.astype(F32)
    mid = r1.astype(BF16)
    lo = (r1 - mid.astype(F32)).astype(BF16)
    return hi, mid, lo


def _ada_kernel(c_ref, w_ref, b_ref, o_ref):
    c = c_ref[...]
    c_act = (c * _sigmoid(c)).astype(BF16)
    o_ref[...] = _dot(c_act, w_ref[...].astype(BF16)) + b_ref[...]


def _ada_mod(c_pad, w_ada, b_ada):
    L, D, N = w_ada.shape
    tn = 1536
    return pl.pallas_call(
        _ada_kernel,
        out_shape=jax.ShapeDtypeStruct((L, SUBLANES, N), F32),
        grid=(L, N // tn),
        in_specs=[pl.BlockSpec((SUBLANES, D), lambda l, j: (0, 0)),
                  pl.BlockSpec((None, D, tn), lambda l, j: (l, 0, j)),
                  pl.BlockSpec((None, 1, tn), lambda l, j: (l, 0, j))],
        out_specs=pl.BlockSpec((None, SUBLANES, tn), lambda l, j: (l, 0, j)),
        compiler_params=_cparams(2, 40),
        name="ada_mod",
    )(c_pad, w_ada, b_ada.reshape(L, 1, N))


def _normed(x, g, scale, shift):
    y = x * lax.rsqrt(jnp.mean(x * x, axis=-1, keepdims=True) + RMS_EPS)
    return (y * g) * (1.0 + scale) + shift


def _norm_mod_kernel(x_ref, g_ref, sc_ref, sh_ref, h_ref):
    h_ref[...] = _normed(x_ref[...], g_ref[...], sc_ref[...], sh_ref[...]).astype(BF16)


def _pack_pairs(h):
    half = h.shape[1] // 2
    hi = lax.bitcast_convert_type(h[:, :half].astype(BF16).astype(F32), U32)
    lo = lax.bitcast_convert_type(h[:, half:].astype(BF16).astype(F32), U32)
    return hi | (lo >> 16)


def _unpack_pairs(w):
    a = lax.bitcast_convert_type(w & jnp.uint32(0xFFFF0000), F32).astype(BF16)
    b = lax.bitcast_convert_type(w << 16, F32).astype(BF16)
    return jnp.concatenate([a, b], axis=1)


def _rows_per_token(width):
    return width // LANES


def _store_token_rows(ref, val):
    n, width = val.shape
    sub = _rows_per_token(width)
    for s in range(sub):
        ref[pl.ds(s, n, stride=sub), :] = val[:, s * LANES:(s + 1) * LANES]


def _load_token_rows(ref, first_token, n, width):
    sub = _rows_per_token(width)
    return jnp.concatenate([ref[pl.ds(first_token * sub + s, n, stride=sub), :] for s in range(sub)], axis=1)


def _norm_route_kernel(x_ref, g_ref, sc_ref, sh_ref, wr_ref, hp_ref, lg_ref):
    h = _normed(x_ref[...], g_ref[...], sc_ref[...], sh_ref[...])
    _store_token_rows(hp_ref, _pack_pairs(h))
    lg_ref[...] = _dot(h, wr_ref[...], precision=HIGHEST)


def _norm_mod(x, g, scale, shift, S, w_router_pad=None):
    T, D = x.shape
    tm = 512
    nb = S // tm
    row = lambda i: (i, 0)
    per_batch = pl.BlockSpec((None, 1, D), lambda i: (i // nb, 0, 0))
    in_specs = [pl.BlockSpec((tm, D), row), pl.BlockSpec((1, D), lambda i: (0, 0)), per_batch, per_batch]
    if w_router_pad is None:
        return pl.pallas_call(
            _norm_mod_kernel, out_shape=jax.ShapeDtypeStruct((T, D), BF16), grid=(T // tm,),
            in_specs=in_specs, out_specs=pl.BlockSpec((tm, D), row),
            compiler_params=_cparams(1, 32), name="norm_mod",
        )(x, g, scale, shift)
    return pl.pallas_call(
        _norm_route_kernel,
        out_shape=(jax.ShapeDtypeStruct((T * (D // 2) // LANES, LANES), U32), jax.ShapeDtypeStruct((T, LANES), F32)),
        grid=(T // tm,),
        in_specs=in_specs + [pl.BlockSpec((D, LANES), lambda i: (0, 0))],
        out_specs=(pl.BlockSpec((tm * (D // 2) // LANES, LANES), row), pl.BlockSpec((tm, LANES), row)),
        compiler_params=_cparams(1, 32), name="norm_route",
    )(x, g, scale, shift, w_router_pad)


def _mm_kernel(*refs, cast_w, residual):
    if residual:
        a_ref, w_ref, x_ref, g_ref, o_ref = refs[:5]
        scratch = refs[5:]
    else:
        a_ref, w_ref, o_ref = refs[:3]
        scratch = refs[3:]
    if cast_w:
        wbf_ref, = scratch

        @pl.when(pl.program_id(1) == 0)
        def _():
            wbf_ref[...] = w_ref[...].astype(BF16)

        w = wbf_ref[...]
    else:
        w = w_ref[...]
    acc = _dot(a_ref[...], w)
    if residual:
        acc = x_ref[...] + g_ref[...] * acc
    o_ref[...] = acc.astype(o_ref.dtype)


def _matmul(a, w, tm, tn, out_dtype=F32, residual=None, S=None, vmem_mib=48, layer=None):
    M, K = a.shape
    N = w.shape[-1]
    cast_w = w.dtype != BF16
    if layer is None:
        w_spec = pl.BlockSpec((K, tn), lambda j, i: (0, j))
    else:
        w_spec = pl.BlockSpec((None, K, tn), lambda j, i: (layer, 0, j))
    in_specs = [pl.BlockSpec((tm, K), lambda j, i: (i, 0)), w_spec]
    args = [a, w]
    if residual is not None:
        x, gate = residual
        nb = S // tm
        in_specs += [pl.BlockSpec((tm, tn), lambda j, i: (i, j)),
                     pl.BlockSpec((None, 1, tn), lambda j, i: (i // nb, 0, j))]
        args += [x, gate]
    return pl.pallas_call(
        functools.partial(_mm_kernel, cast_w=cast_w, residual=residual is not None),
        out_shape=jax.ShapeDtypeStruct((M, N), out_dtype),
        grid=(N // tn, M // tm),
        in_specs=in_specs,
        out_specs=pl.BlockSpec((tm, tn), lambda j, i: (i, j)),
        scratch_shapes=[pltpu.VMEM((K, tn), BF16)] if cast_w else [],
        compiler_params=_cparams(2, vmem_mib),
        name="matmul_res" if residual is not None else "matmul",
    )(*args)


IN_TN = 1024
IN_SEGMENTS = ((0, COL_MOBA * LANES // IN_TN, 0),
               (COL_MOBA * LANES // IN_TN, COL_GATES * LANES // IN_TN, N_HEADS),
               (COL_GATES * LANES // IN_TN, N_MAIN_COLS // IN_TN, N_HEADS + GLA_GATE_RANK))
ROW_CHUNK = 256


def _in_proj_kernel(a_ref, w_ref, wn_ref, o_ref, wbf_ref, *, tn):
    j = pl.program_id(0)

    @pl.when(pl.program_id(1) == 0)
    def _():
        K = w_ref.shape[0]
        for lo, hi, shift in IN_SEGMENTS:
            @pl.when(jnp.logical_and(j >= lo, j < hi))
            def _():
                for r in range(0, K, ROW_CHUNK):
                    rows = slice(r, r + ROW_CHUNK)
                    if shift == 0:
                        wbf_ref[rows, :] = w_ref[rows, :].astype(BF16)
                    else:
                        wide = jnp.concatenate([w_ref[rows, :], wn_ref[rows, :]], axis=1)
                        wbf_ref[rows, :] = wide[:, shift:shift + tn].astype(BF16)

    o_ref[...] = _dot(a_ref[...], wbf_ref[...]).astype(o_ref.dtype)


def _in_proj(a, w_in, layer, tm=512):
    M, K = a.shape
    tn = IN_TN
    return pl.pallas_call(
        functools.partial(_in_proj_kernel, tn=tn),
        out_shape=jax.ShapeDtypeStruct((M, N_MAIN_COLS), BF16),
        grid=(N_MAIN_COLS // tn, M // tm),
        in_specs=[pl.BlockSpec((tm, K), lambda j, i: (i, 0)),
                  pl.BlockSpec((K, tn), lambda j, i: (layer, j)),
                  pl.BlockSpec((K, LANES), lambda j, i: (layer, (j + 1) * (tn // LANES)))],
        out_specs=pl.BlockSpec((tm, tn), lambda j, i: (i, j)),
        scratch_shapes=[pltpu.VMEM((K, tn), BF16)],
        compiler_params=_cparams(2, 48),
        name="in_proj",
    )(a, w_in, w_in)


def _misc_proj_kernel(a_ref, wf_ref, wg_ref, o_ref, wbf_ref):
    @pl.when(pl.program_id(0) == 0)
    def _():
        lane = lax.broadcasted_iota(I32, wf_ref.shape, 1)
        w = jnp.where(lane < MISC_GLR, wf_ref[...],
                      jnp.where(lane < MISC_GLR + GLA_GATE_RANK, wg_ref[...], 0.0))
        wbf_ref[...] = w.astype(BF16)

    o_ref[...] = _dot(a_ref[...], wbf_ref[...])


def _misc_proj(a, w_in, layer, tm=1024):
    M, K = a.shape
    return pl.pallas_call(
        _misc_proj_kernel,
        out_shape=jax.ShapeDtypeStruct((M, LANES), F32),
        grid=(M // tm,),
        in_specs=[pl.BlockSpec((tm, K), lambda i: (i, 0)),
                  pl.BlockSpec((K, LANES), lambda i: (layer, COL_MOBA)),
                  pl.BlockSpec((K, LANES), lambda i: (layer, COL_GATES))],
        out_specs=pl.BlockSpec((tm, LANES), lambda i: (i, 0)),
        scratch_shapes=[pltpu.VMEM((K, LANES), BF16)],
        compiler_params=_cparams(1, 32),
        name="misc_proj",
    )(a, w_in, w_in)


SB_LOG_WEIGHT_FLOOR = -110.0


def _sb_kernel(q_ref, k_ref, v_ref, o_ref, *, blk, scale):
    i = pl.program_id(1)
    row = lax.broadcasted_iota(I32, (blk, blk), 0)
    col = lax.broadcasted_iota(I32, (blk, blk), 1)
    later = (row > col).astype(BF16)

    def cond(state):
        jj, alive, _ = state
        return jnp.logical_and(jj <= i, alive)

    def body(state):
        jj, _, heads = state
        j = i - jj
        start = pl.multiple_of(j * blk, blk)
        past = jnp.logical_or(j < i, col < row)
        new, top = [], None
        for h in range(N_HEADS):
            c, acc = heads[h]
            hs = slice(h * HEAD_DIM, (h + 1) * HEAD_DIM)
            q = (q_ref[:, hs].astype(F32) * scale).astype(BF16)
            k = k_ref[pl.ds(start, blk), hs]
            v = v_ref[pl.ds(start, blk), hs]
            z = _dot_nt(q, k)
            ls = jnp.where(past, -_softplus(z), 0.0)
            hi = ls.astype(BF16)
            lo = (ls - hi.astype(F32)).astype(BF16)
            between = _dot(hi, later) + _dot(lo, later)
            w = jnp.where(past, jnp.exp(z + ls + between + c), 0.0)
            acc = acc + _dot(w.astype(BF16), v)
            c = c + jnp.sum(ls, axis=1, keepdims=True)
            new.append((c, acc))
            top = c if top is None else jnp.maximum(top, c)
        return jj + 1, jnp.max(top) > SB_LOG_WEIGHT_FLOOR, tuple(new)

    init = tuple((jnp.zeros((blk, 1), F32), jnp.zeros((blk, HEAD_DIM), F32)) for _ in range(N_HEADS))
    _, _, heads = lax.while_loop(cond, body, (jnp.int32(0), jnp.bool_(True), init))
    for h in range(N_HEADS):
        o_ref[:, h * HEAD_DIM:(h + 1) * HEAD_DIM] = heads[h][1]


def _sb_attention(proj, B, S, col0):
    T = proj.shape[0]
    blk = ATT_BLOCK
    nq = S // blk
    W = BRANCH_WIDTH
    cb = col0 * LANES // W
    return pl.pallas_call(
        functools.partial(_sb_kernel, blk=blk, scale=HEAD_DIM ** -0.5),
        out_shape=jax.ShapeDtypeStruct((T, W), F32),
        grid=(B, nq),
        in_specs=[pl.BlockSpec((blk, W), lambda b, i: (b * nq + i, cb)),
                  pl.BlockSpec((S, W), lambda b, i: (b, cb + 1)),
                  pl.BlockSpec((S, W), lambda b, i: (b, cb + 2))],
        out_specs=pl.BlockSpec((blk, W), lambda b, i: (b * nq + i, 0)),
        compiler_params=_cparams(2, 44),
        name="sb_attention",
    )(proj, proj, proj)


def _flash_kernel(qa_ref, ka_ref, v_ref, o_ref, *, blk):
    i = pl.program_id(1)
    row = lax.broadcasted_iota(I32, (blk, blk), 0)
    col = lax.broadcasted_iota(I32, (blk, blk), 1)
    A = 2 * HEAD_DIM

    def step(j, carry, diagonal=False):
        start = pl.multiple_of(j * blk, blk)
        new = []
        for h in range(N_HEADS):
            m, l, acc = carry[h]
            s = _dot_nt(qa_ref[:, h * A:(h + 1) * A], ka_ref[pl.ds(start, blk), h * A:(h + 1) * A])
            if diagonal:
                s = jnp.where(col <= row, s, NEG_BIG)
            m_new = jnp.maximum(m, jnp.max(s, axis=1, keepdims=True))
            alpha = jnp.exp(m - m_new)
            p = jnp.exp(s - m_new)
            l = alpha * l + jnp.sum(p, axis=1, keepdims=True)
            acc = alpha * acc + _dot(p.astype(BF16), v_ref[pl.ds(start, blk), h * HEAD_DIM:(h + 1) * HEAD_DIM])
            new.append((m_new, l, acc))
        return tuple(new)

    init = tuple((jnp.full((blk, 1), NEG_BIG, F32), jnp.zeros((blk, 1), F32),
                  jnp.zeros((blk, HEAD_DIM), F32)) for _ in range(N_HEADS))
    heads = step(i, lax.fori_loop(0, i, step, init), diagonal=True)
    for h in range(N_HEADS):
        _, l, acc = heads[h]
        o_ref[:, h * HEAD_DIM:(h + 1) * HEAD_DIM] = acc / l


def _flash_attention(qa, ka, va, B, S):
    T = qa.shape[0]
    blk = ATT_BLOCK
    nq = S // blk
    W = BRANCH_WIDTH
    return pl.pallas_call(
        functools.partial(_flash_kernel, blk=blk),
        out_shape=jax.ShapeDtypeStruct((T, W), F32),
        grid=(B, nq),
        in_specs=[pl.BlockSpec((blk, 2 * W), lambda b, i: (b * nq + i, 0)),
                  pl.BlockSpec((S, 2 * W), lambda b, i: (b, 0)),
                  pl.BlockSpec((S, W), lambda b, i: (b, 0))],
        out_specs=pl.BlockSpec((blk, W), lambda b, i: (b * nq + i, 0)),
        compiler_params=_cparams(2, 40),
        name="flash_attention",
    )(qa, ka, va)


def _fox_prep_kernel(q_ref, k_ref, v_ref, misc_ref, bias_ref, qa_ref, ka_ref, va_ref, carry_ref, *, tb, scale):
    va_ref[...] = v_ref[...].astype(BF16)
    @pl.when(pl.program_id(1) == 0)
    def _():
        carry_ref[...] = jnp.zeros_like(carry_ref)

    lane = lax.broadcasted_iota(I32, (tb, LANES), 1)
    lf = jnp.where(lane < N_HEADS, _log_sigmoid(misc_ref[...] + bias_ref[...]), 0.0)
    row = lax.broadcasted_iota(I32, (tb, tb), 0)
    col = lax.broadcasted_iota(I32, (tb, tb), 1)
    incl = (col <= row).astype(F32)
    F = _dot(incl, lf, precision=HIGHEST) + carry_ref[0:1, :]
    carry_ref[0:1, :] = F[tb - 1:tb, :]
    for h in range(N_HEADS):
        Fh = jnp.broadcast_to(F[:, h:h + 1], (tb, LANES))
        hi, mid, lo = (p.astype(F32) for p in _split3(Fh))
        ones = jnp.ones((tb, LANES), F32)
        zeros = jnp.zeros((tb, LANES), F32)
        eq = jnp.where(lane == 0, hi, jnp.where(lane == 1, mid, jnp.where(lane == 2, lo,
             jnp.where(lane < 6, ones, zeros))))
        ek = jnp.where(lane < 3, ones, jnp.where(lane == 3, -hi, jnp.where(lane == 4, -mid,
             jnp.where(lane == 5, -lo, zeros))))
        hs = slice(h * HEAD_DIM, (h + 1) * HEAD_DIM)
        qa_ref[:, 2 * h * HEAD_DIM:(2 * h + 1) * HEAD_DIM] = (q_ref[:, hs].astype(F32) * scale).astype(BF16)
        qa_ref[:, (2 * h + 1) * HEAD_DIM:(2 * h + 2) * HEAD_DIM] = eq.astype(BF16)
        ka_ref[:, 2 * h * HEAD_DIM:(2 * h + 1) * HEAD_DIM] = k_ref[:, hs].astype(BF16)
        ka_ref[:, (2 * h + 1) * HEAD_DIM:(2 * h + 2) * HEAD_DIM] = ek.astype(BF16)


def _fox_prep(proj, misc, fox_bias_pad, B, S):
    T = proj.shape[0]
    tb = 512
    nb = S // tb
    qblk = COL_FOX * LANES // BRANCH_WIDTH
    out = jax.ShapeDtypeStruct((T, 2 * BRANCH_WIDTH), BF16)
    return pl.pallas_call(
        functools.partial(_fox_prep_kernel, tb=tb, scale=HEAD_DIM ** -0.5),
        out_shape=(out, out, jax.ShapeDtypeStruct((T, BRANCH_WIDTH), BF16)),
        grid=(B, nb),
        in_specs=[pl.BlockSpec((tb, BRANCH_WIDTH), lambda b, i: (b * nb + i, qblk)),
                  pl.BlockSpec((tb, BRANCH_WIDTH), lambda b, i: (b * nb + i, qblk + 1)),
                  pl.BlockSpec((tb, BRANCH_WIDTH), lambda b, i: (b * nb + i, qblk + 2)),
                  pl.BlockSpec((tb, LANES), lambda b, i: (b * nb + i, 0)),
                  pl.BlockSpec((1, LANES), lambda b, i: (0, 0))],
        out_specs=(pl.BlockSpec((tb, 2 * BRANCH_WIDTH), lambda b, i: (b * nb + i, 0)),
                   pl.BlockSpec((tb, 2 * BRANCH_WIDTH), lambda b, i: (b * nb + i, 0)),
                   pl.BlockSpec((tb, BRANCH_WIDTH), lambda b, i: (b * nb + i, 0))),
        scratch_shapes=[pltpu.VMEM((SUBLANES, LANES), F32)],
        compiler_params=_cparams(2, 32),
        name="fox_prep",
    )(proj, proj, proj, misc, fox_bias_pad)


def _rope(x, cos, sin, lane):
    half = ROPE_DIM // 2
    up = pltpu.roll(x, half, 1)
    down = pltpu.roll(x, LANES - half, 1)
    rot = jnp.where(lane < half, -down * sin, jnp.where(lane < ROPE_DIM, up * sin, 0.0))
    return x * jnp.where(lane < ROPE_DIM, cos, 1.0) + rot


def _angles(pos_ref, freq_ref):
    ang = pos_ref[...].astype(F32) * freq_ref[...]
    return jnp.cos(ang), jnp.sin(ang)


def _moba_k_kernel(k_ref, v_ref, pos_ref, freq_ref, ka_ref, va_ref, kmean_ref, *, tb):
    i = pl.program_id(1)
    va_ref[...] = v_ref[...].astype(BF16)

    @pl.when(i == 0)
    def _():
        kmean_ref[...] = jnp.zeros_like(kmean_ref)

    lane = lax.broadcasted_iota(I32, (tb, LANES), 1)
    cos, sin = _angles(pos_ref, freq_ref)
    onehot = jnp.where(lane == i, 1.0, 0.0).astype(BF16)
    this_row = lax.broadcasted_iota(I32, (LANES, HEAD_DIM), 0) == i
    for h in range(N_HEADS):
        hs = slice(h * HEAD_DIM, (h + 1) * HEAD_DIM)
        kr = _rope(k_ref[:, hs].astype(F32), cos, sin, lane)
        mean = jnp.sum(kr, axis=0, keepdims=True) * (1.0 / tb)
        kmean_ref[:, hs] = jnp.where(this_row, mean, kmean_ref[:, hs])
        ka_ref[:, 2 * h * HEAD_DIM:(2 * h + 1) * HEAD_DIM] = kr.astype(BF16)
        ka_ref[:, (2 * h + 1) * HEAD_DIM:(2 * h + 2) * HEAD_DIM] = onehot


def _moba_q_kernel(q_ref, pos_ref, freq_ref, kmean_ref, qa_ref, *, tb, scale):
    i = pl.program_id(1)
    lane = lax.broadcasted_iota(I32, (tb, LANES), 1)
    cos, sin = _angles(pos_ref, freq_ref)
    neg_inf = jnp.float32(-jnp.inf)
    for h in range(N_HEADS):
        hs = slice(h * HEAD_DIM, (h + 1) * HEAD_DIM)
        qr = _rope(q_ref[:, hs].astype(F32), cos, sin, lane)
        gate = _dot_nt(qr, kmean_ref[:, hs], precision=HIGHEST)
        cur = jnp.where(lane < i, gate, neg_inf)
        chosen = lane == i
        for _ in range(MOBA_TOPK):
            m = jnp.max(cur, axis=1, keepdims=True)
            first = jnp.min(jnp.where(jnp.logical_and(cur == m, m > neg_inf), lane, LANES),
                            axis=1, keepdims=True)
            pick = lane == first
            chosen = jnp.logical_or(chosen, pick)
            cur = jnp.where(pick, neg_inf, cur)
        bias = jnp.where(jnp.logical_or(chosen, lane >= LANES // 2), 0.0, NEG_BIG)
        qa_ref[:, 2 * h * HEAD_DIM:(2 * h + 1) * HEAD_DIM] = (qr * scale).astype(BF16)
        qa_ref[:, (2 * h + 1) * HEAD_DIM:(2 * h + 2) * HEAD_DIM] = bias.astype(BF16)


def _moba_prep(proj, pos_col, freq_lanes, B, S):
    T = proj.shape[0]
    tb = MOBA_BLOCK
    nb = S // tb
    qblk = COL_MOBA * LANES // BRANCH_WIDTH
    aug = jax.ShapeDtypeStruct((T, 2 * BRANCH_WIDTH), BF16)
    row_spec = lambda c: pl.BlockSpec((tb, BRANCH_WIDTH), lambda b, i: (b * nb + i, c))
    pos_spec = pl.BlockSpec((tb, 1), lambda b, i: (b * nb + i, 0))
    freq_spec = pl.BlockSpec((1, LANES), lambda b, i: (0, 0))
    aug_spec = pl.BlockSpec((tb, 2 * BRANCH_WIDTH), lambda b, i: (b * nb + i, 0))
    kmean_spec = pl.BlockSpec((None, LANES, BRANCH_WIDTH), lambda b, i: (b, 0, 0))
    ka, va, kmean = pl.pallas_call(
        functools.partial(_moba_k_kernel, tb=tb),
        out_shape=(aug, jax.ShapeDtypeStruct((T, BRANCH_WIDTH), BF16),
                   jax.ShapeDtypeStruct((B, LANES, BRANCH_WIDTH), F32)),
        grid=(B, nb),
        in_specs=[row_spec(qblk + 1), row_spec(qblk + 2), pos_spec, freq_spec],
        out_specs=(aug_spec, pl.BlockSpec((tb, BRANCH_WIDTH), lambda b, i: (b * nb + i, 0)), kmean_spec),
        compiler_params=_cparams(2, 32),
        name="moba_k_prep",
    )(proj, proj, pos_col, freq_lanes)
    qa = pl.pallas_call(
        functools.partial(_moba_q_kernel, tb=tb, scale=HEAD_DIM ** -0.5),
        out_shape=aug,
        grid=(B, nb),
        in_specs=[row_spec(qblk), pos_spec, freq_spec, kmean_spec],
        out_specs=aug_spec,
        compiler_params=_cparams(2, 32),
        name="moba_q_prep",
    )(proj, pos_col, freq_lanes, kmean)
    return qa, ka, va


def _gla_kernel(q_ref, k_ref, v_ref, gr_ref, misc_ref, wg_ref, bg_ref, gn_ref, o_ref, state_ref, *, tb):
    C = GLA_CHUNK
    @pl.when(pl.program_id(1) == 0)
    def _():
        state_ref[...] = jnp.zeros_like(state_ref)

    la_all = _log_sigmoid(_dot(misc_ref[...], wg_ref[...], precision=HIGHEST) + bg_ref[...]) \
        * (1.0 / GLA_GATE_NORMALIZER)
    rowc = lax.broadcasted_iota(I32, (C, C), 0)
    colc = lax.broadcasted_iota(I32, (C, C), 1)
    incl = (colc <= rowc).astype(F32)
    causal = colc <= rowc
    lane = lax.broadcasted_iota(I32, (C, LANES), 1)
    row2 = lax.broadcasted_iota(I32, (2 * GLA_HEAD_K, LANES), 0)
    ones_cv = jnp.ones((C, LANES), F32)
    qscale = GLA_HEAD_K ** -0.5
    gn = gn_ref[...]
    for c in range(tb // C):
        rs = slice(c * C, (c + 1) * C)
        for p in range(2):
            ps = slice(p * LANES, (p + 1) * LANES)
            la = la_all[rs, ps]
            b = _dot(incl, la, precision=HIGHEST)
            b_last = b[C - 1:C, :]
            eb = jnp.exp(b)
            q_dec = q_ref[rs, ps].astype(F32) * qscale * eb
            k = k_ref[rs, ps].astype(F32)
            k_inv = (k * jnp.exp(-b)).astype(BF16)
            k_end = (k * jnp.exp(b_last - b)).astype(BF16)
            decay = jnp.exp(_dot_tn(la, ones_cv, precision=HIGHEST))
            st = state_ref[p * LANES:(p + 1) * LANES, :]
            new_rows = []
            for e in range(2):
                h = 2 * p + e
                own = jnp.logical_and(lane >= e * GLA_HEAD_K, lane < (e + 1) * GLA_HEAD_K)
                qh = jnp.where(own, q_dec, 0.0).astype(BF16)
                v = v_ref[rs, h * HEAD_DIM:(h + 1) * HEAD_DIM].astype(BF16)
                attn = jnp.where(causal, _dot_nt(qh, k_inv), 0.0)
                st_h = jnp.where(jnp.logical_and(row2 >= e * GLA_HEAD_K, row2 < (e + 1) * GLA_HEAD_K), st, 0.0)
                o = _dot(attn.astype(BF16), v) + _dot(qh, st_h.astype(BF16))
                y = o * lax.rsqrt(jnp.mean(o * o, axis=-1, keepdims=True) + RMS_EPS) * gn
                g = gr_ref[rs, h * HEAD_DIM:(h + 1) * HEAD_DIM].astype(F32)
                o_ref[rs, h * HEAD_DIM:(h + 1) * HEAD_DIM] = y * (g * _sigmoid(g))
                new_rows.append(_dot_tn(k_end, v))
            kv = jnp.where(row2 < GLA_HEAD_K, new_rows[0], new_rows[1])
            state_ref[p * LANES:(p + 1) * LANES, :] = decay * st + kv


def _gla(proj, misc, w_gate_pad, b_gate, gla_norm, B, S):
    T = proj.shape[0]
    tb = 512
    nb = S // tb
    kd = 2 * LANES
    rows = lambda w, c: pl.BlockSpec((tb, w), lambda b, i: (b * nb + i, c))
    const = lambda shape: pl.BlockSpec(shape, lambda b, i: (0, 0))
    return pl.pallas_call(
        functools.partial(_gla_kernel, tb=tb),
        out_shape=jax.ShapeDtypeStruct((T, BRANCH_WIDTH), F32),
        grid=(B, nb),
        in_specs=[rows(kd, COL_GQ * LANES // kd), rows(kd, COL_GK * LANES // kd),
                  rows(BRANCH_WIDTH, COL_GV * LANES // BRANCH_WIDTH),
                  rows(BRANCH_WIDTH, COL_GR * LANES // BRANCH_WIDTH),
                  rows(LANES, 0), const((LANES, kd)), const((1, kd)), const((1, HEAD_DIM))],
        out_specs=rows(BRANCH_WIDTH, 0),
        scratch_shapes=[pltpu.VMEM((kd, HEAD_DIM), F32)],
        compiler_params=_cparams(2, 32),
        name="gla",
    )(proj, proj, proj, proj, misc, w_gate_pad, b_gate, gla_norm)


def _merge_kernel(oa_ref, ob_ref, oc_ref, od_ref, g0_ref, g1_ref, g2_ref, g3_ref, w_ref, o_ref):
    acc = None
    for n, (b_ref, g_ref) in enumerate(((oa_ref, g0_ref), (ob_ref, g1_ref), (oc_ref, g2_ref), (od_ref, g3_ref))):
        term = _sigmoid(g_ref[...].astype(F32)) * _dot(b_ref[...].astype(BF16), w_ref[n])
        acc = term if acc is None else acc + term
    o_ref[...] = acc.astype(BF16)


def _merge(branches, proj, w_branch_bf16, layer):
    T = proj.shape[0]
    _, nbr, W, D = w_branch_bf16.shape
    tm = 256
    g0 = COL_GATES * LANES // D
    br_spec = pl.BlockSpec((tm, W), lambda i: (i, 0))
    gate_spec = lambda n: pl.BlockSpec((tm, D), lambda i: (i, g0 + n))
    return pl.pallas_call(
        _merge_kernel,
        out_shape=jax.ShapeDtypeStruct((T, D), BF16),
        grid=(T // tm,),
        in_specs=[br_spec] * 4 + [gate_spec(n) for n in range(4)]
                 + [pl.BlockSpec((None, nbr, W, D), lambda i: (layer, 0, 0, 0))],
        out_specs=pl.BlockSpec((tm, D), lambda i: (i, 0)),
        compiler_params=_cparams(1, 44),
        name="branch_merge",
    )(*branches, proj, proj, proj, proj, w_branch_bf16)


def _butterfly(x, lane, op):
    for s in (1, 2, 4):
        up = pltpu.roll(x, s, 1)
        down = pltpu.roll(x, LANES - s, 1)
        x = op(x, jnp.where((lane & s) != 0, up, down))
    return x


def _route_kernel(lg_ref, bias_ref, e8_ref, w8_ref, p8_ref, cnt_ref, carry_ref, *, tm):
    @pl.when(pl.program_id(0) == 0)
    def _():
        carry_ref[...] = jnp.zeros_like(carry_ref)

    neg_inf = jnp.float32(-jnp.inf)
    lane = lax.broadcasted_iota(I32, (tm, LANES), 1)
    valid = lane < N_EXPERTS
    scores = _sigmoid(lg_ref[...])
    biased = jnp.where(valid, scores + bias_ref[...], neg_inf)
    g1 = _butterfly(biased, lane, jnp.maximum)
    first = _butterfly(jnp.where(biased == g1, lane, LANES), lane, jnp.minimum)
    g2 = _butterfly(jnp.where(lane == first, neg_inf, biased), lane, jnp.maximum)
    gs = g1 + g2
    gs = jnp.where(valid, gs, pltpu.roll(gs, N_EXPERTS, 1))
    gidx = lane >> 3
    beaten = jnp.zeros((tm, LANES), I32)
    for r in range(1, N_EXPERTS // GROUP_SIZE):
        other = pltpu.roll(gs, GROUP_SIZE * r, 1)
        og = (gidx - r) & (N_EXPERTS // GROUP_SIZE - 1)
        wins = jnp.logical_or(other > gs, jnp.logical_and(other == gs, og < gidx))
        beaten = beaten + wins.astype(I32)
    cur = jnp.where(jnp.logical_and(beaten < TOPK_GROUPS, valid), biased, neg_inf)
    sel = jnp.zeros((tm, LANES), jnp.bool_)
    for _ in range(TOP_K):
        m = jnp.max(cur, axis=1, keepdims=True)
        pick = lane == jnp.min(jnp.where(cur == m, lane, LANES), axis=1, keepdims=True)
        sel = jnp.logical_or(sel, pick)
        cur = jnp.where(pick, neg_inf, cur)
    wsel = jnp.where(sel, scores, 0.0)
    wd = wsel / jnp.sum(wsel, axis=1, keepdims=True) * ROUTED_SCALE
    selb = jnp.where(sel, 1.0, 0.0).astype(BF16)
    row = lax.broadcasted_iota(I32, (tm, tm), 0)
    col = lax.broadcasted_iota(I32, (tm, tm), 1)
    pos = _dot((col < row).astype(BF16), selb) + carry_ref[0:1, :]
    total = carry_ref[0:1, :] + jnp.sum(selb.astype(F32), axis=0, keepdims=True)
    carry_ref[0:1, :] = total
    cnt_ref[...] = jnp.broadcast_to(total, cnt_ref.shape)
    r2 = lax.broadcasted_iota(I32, (LANES, LANES), 0)
    c2 = lax.broadcasted_iota(I32, (LANES, LANES), 1)
    slot = _dot(selb, (r2 < c2).astype(BF16))
    lane_f = lane.astype(F32)
    e8 = jnp.zeros((tm, LANES), F32)
    w8 = jnp.zeros((tm, LANES), F32)
    p8 = jnp.zeros((tm, LANES), F32)
    for k in range(TOP_K):
        mk = jnp.logical_and(sel, slot == k)
        put = lane == k
        e8 = jnp.where(put, jnp.sum(jnp.where(mk, lane_f, 0.0), axis=1, keepdims=True), e8)
        w8 = jnp.where(put, jnp.sum(jnp.where(mk, wd, 0.0), axis=1, keepdims=True), w8)
        p8 = jnp.where(put, jnp.sum(jnp.where(mk, pos, 0.0), axis=1, keepdims=True), p8)
    e8_ref[...] = e8.astype(I32)
    w8_ref[...] = w8
    p8_ref[...] = p8.astype(I32)


def _route(logits, router_bias_pad):
    T = logits.shape[0]
    tm = 512
    row = pl.BlockSpec((tm, LANES), lambda i: (i, 0))
    return pl.pallas_call(
        functools.partial(_route_kernel, tm=tm),
        out_shape=(jax.ShapeDtypeStruct((T, LANES), I32), jax.ShapeDtypeStruct((T, LANES), F32),
                   jax.ShapeDtypeStruct((T, LANES), I32), jax.ShapeDtypeStruct((SUBLANES, LANES), F32)),
        grid=(T // tm,),
        in_specs=[row, pl.BlockSpec((1, LANES), lambda i: (0, 0))],
        out_specs=(row, row, row, pl.BlockSpec((SUBLANES, LANES), lambda i: (0, 0))),
        scratch_shapes=[pltpu.VMEM((SUBLANES, LANES), F32)],
        compiler_params=_cparams(1, 32),
        name="route",
    )(logits, router_bias_pad)


def _token_copy(src, src_token, dst, dst_token, sem, sub):
    return pltpu.make_async_copy(src.at[pl.ds(pl.multiple_of(src_token * sub, sub), sub), :],
                                 dst.at[pl.ds(pl.multiple_of(dst_token * sub, sub), sub), :], sem)


def _dispatch_kernel(pad_end_ref, padded_ref, dest_ref, hp_ref, xs_ref, zero_ref, sem_ref, *, tm, rows, n_blocks, sub):
    @pl.when(pl.program_id(0) == 0)
    def _():
        zero_ref[...] = jnp.zeros_like(zero_ref)

        def fill(e, do_wait):
            @pl.when(padded_ref[e] > 0)
            def _():
                start = pl.multiple_of((pad_end_ref[e] - rows) * sub, rows * sub)
                cp = pltpu.make_async_copy(zero_ref, xs_ref.at[pl.ds(start, rows * sub), :], sem_ref)
                if do_wait:
                    cp.wait()
                else:
                    cp.start()

        lax.fori_loop(0, N_EXPERTS, lambda e, c: (fill(e, False), c)[1], 0)
        lax.fori_loop(0, N_EXPERTS, lambda e, c: (fill(e, True), c)[1], 0)

        def tail(b):
            return pltpu.make_async_copy(zero_ref, xs_ref.at[pl.ds(pl.multiple_of(b * rows * sub, rows * sub), rows * sub), :],
                                         sem_ref)

        n_used = pad_end_ref[N_EXPERTS - 1] // rows
        lax.fori_loop(n_used, n_blocks, lambda b, c: (tail(b).start(), c)[1], 0)
        lax.fori_loop(n_used, n_blocks, lambda b, c: (tail(b).wait(), c)[1], 0)

    def issue(t, c):
        for k in range(TOP_K):
            _token_copy(hp_ref, t, xs_ref, dest_ref[t * TOP_K + k], sem_ref, sub).start(priority=k % 2)
        return c

    def drain(t, c):
        for k in range(TOP_K):
            _token_copy(hp_ref, t, xs_ref, dest_ref[t * TOP_K + k], sem_ref, sub).wait()
        return c

    lax.fori_loop(0, tm, issue, 0)
    lax.fori_loop(0, tm, drain, 0)


def _dispatch(hp, dest_flat, pad_end, padded, n_rows, sub):
    tm = TOKEN_TILE
    T = hp.shape[0] // sub
    grid_spec = pltpu.PrefetchScalarGridSpec(
        num_scalar_prefetch=2,
        grid=(T // tm,),
        in_specs=[pl.BlockSpec((tm * TOP_K,), lambda i, pe, pd: (i,), memory_space=pltpu.SMEM),
                  pl.BlockSpec((tm * sub, LANES), lambda i, pe, pd: (i, 0))],
        out_specs=pl.BlockSpec(memory_space=pl.ANY),
        scratch_shapes=[pltpu.VMEM((EXPERT_ROWS * sub, LANES), U32), pltpu.SemaphoreType.DMA(())],
    )
    return pl.pallas_call(
        functools.partial(_dispatch_kernel, tm=tm, rows=EXPERT_ROWS, n_blocks=n_rows // EXPERT_ROWS, sub=sub),
        out_shape=jax.ShapeDtypeStruct((n_rows * sub, LANES), U32),
        grid_spec=grid_spec,
        compiler_params=_cparams(1, 32),
        name="dispatch",
    )(pad_end, padded, dest_flat, hp)


def _expert_kernel(blk_e_ref, next_e_ref, n_used_ref, x_ref, wg_hbm, wu_hbm, wd_hbm, y_ref,
                   wg_f32, wu_f32, wd_f32, wg_bf, wu_bf, wd_bf, sem_ref, slot_ref, *, layer, rows, width):
    i = pl.program_id(0)
    e = blk_e_ref[i]
    active = i < n_used_ref[0]
    first = jnp.logical_or(i == 0, e != blk_e_ref[jnp.maximum(i - 1, 0)])

    def fetch(expert, slot):
        return [pltpu.make_async_copy(w.at[layer, expert], buf.at[slot], sem_ref.at[slot])
                for w, buf in ((wg_hbm, wg_f32), (wu_hbm, wu_f32), (wd_hbm, wd_f32))]

    @pl.when(i == 0)
    def _():
        slot_ref[0] = 0
        for cp in fetch(e, 0):
            cp.start()

    @pl.when(jnp.logical_and(active, first))
    def _():
        slot = slot_ref[0]
        for cp in fetch(e, slot):
            cp.wait()
        nxt = next_e_ref[i]

        @pl.when(nxt >= 0)
        def _():
            for cp in fetch(nxt, 1 - slot):
                cp.start()

        wg_bf[...] = wg_f32[slot].astype(BF16)
        wu_bf[...] = wu_f32[slot].astype(BF16)
        wd_bf[...] = wd_f32[slot].astype(BF16)
        slot_ref[0] = 1 - slot

    @pl.when(active)
    def _():
        x = _unpack_pairs(_load_token_rows(x_ref, 0, rows, width))
        g = _dot(x, wg_bf[...])
        u = _dot(x, wu_bf[...])
        hb = (g * _sigmoid(g)) * u
        _store_token_rows(y_ref, _pack_pairs(_dot(hb.astype(BF16), wd_bf[...])))

    @pl.when(jnp.logical_not(active))
    def _():
        y_ref[...] = jnp.zeros_like(y_ref)


def _experts(xs, blk_e, next_e, n_used, w_gate, w_up, w_down, layer):
    _, E, D, Hx = w_gate.shape
    W = D // 2
    sub = _rows_per_token(W)
    M = EXPERT_ROWS
    nblk = xs.shape[0] // (M * sub)
    xrow = lambda i, be, ne, nu: (jnp.minimum(i, nu[0] - 1), 0)
    hbm = pl.BlockSpec(memory_space=pl.ANY)
    grid_spec = pltpu.PrefetchScalarGridSpec(
        num_scalar_prefetch=3,
        grid=(nblk,),
        in_specs=[pl.BlockSpec((M * sub, LANES), xrow), hbm, hbm, hbm],
        out_specs=pl.BlockSpec((M * sub, LANES), lambda i, be, ne, nu: (i, 0)),
        scratch_shapes=[pltpu.VMEM((2, D, Hx), F32), pltpu.VMEM((2, D, Hx), F32), pltpu.VMEM((2, Hx, D), F32),
                        pltpu.VMEM((D, Hx), BF16), pltpu.VMEM((D, Hx), BF16), pltpu.VMEM((Hx, D), BF16),
                        pltpu.SemaphoreType.DMA((2,)), pltpu.SMEM((1,), I32)],
    )
    return pl.pallas_call(
        functools.partial(_expert_kernel, layer=layer, rows=M, width=W),
        out_shape=jax.ShapeDtypeStruct(xs.shape, U32),
        grid_spec=grid_spec,
        compiler_params=_cparams(1, 52),
        name="experts",
    )(blk_e, next_e, n_used, xs, w_gate, w_up, w_down)


def _next_expert(blk_e, cnt):
    E = cnt.shape[0]
    idx = jnp.where(cnt > 0, jnp.arange(E, dtype=I32), E)
    later = jnp.concatenate([lax.cummin(idx, axis=0, reverse=True)[1:], jnp.full((1,), E, I32)])
    return jnp.where(later < E, later, -1)[blk_e]


def _unpack_pairs_f32(w):
    a = lax.bitcast_convert_type(w & jnp.uint32(0xFFFF0000), F32)
    b = lax.bitcast_convert_type(w << 16, F32)
    return jnp.concatenate([a, b], axis=1)


def _combine_kernel(dcur_ref, dnext_ref, x_ref, ysh_ref, w8_ref, g2_ref, fn_ref, ys_ref, o_ref, buf_ref, sem_ref,
                    *, tm, n_steps, width, final_norm):
    i = pl.program_id(0)
    slot = lax.rem(i, 2)
    sub = _rows_per_token(width)

    def issue(dref, s):
        def body(t, c):
            for k in range(TOP_K):
                _token_copy(ys_ref, dref[t * TOP_K + k], buf_ref.at[s], k * tm + t, sem_ref.at[s], sub).start(priority=k % 2)
            return c
        lax.fori_loop(0, tm, body, 0)

    @pl.when(i == 0)
    def _():
        issue(dcur_ref, 0)

    @pl.when(i + 1 < n_steps)
    def _():
        issue(dnext_ref, 1 - slot)

    def drain(t, c):
        for k in range(TOP_K):
            _token_copy(ys_ref, 0, buf_ref.at[slot], 0, sem_ref.at[slot], sub).wait()
        return c

    lax.fori_loop(0, tm, drain, 0)
    w8 = w8_ref[...]
    y = _unpack_pairs_f32(_load_token_rows(ysh_ref, 0, tm, width))
    gathered = buf_ref.at[slot]
    for k in range(TOP_K):
        y = y + w8[:, k:k + 1] * _unpack_pairs_f32(_load_token_rows(gathered, k * tm, tm, width))
    out = x_ref[...] + g2_ref[...] * y
    if final_norm:
        out = out * lax.rsqrt(jnp.mean(out * out, axis=-1, keepdims=True) + RMS_EPS) * fn_ref[...]
    o_ref[...] = out


def _combine(x, ysh, ys, dest_flat, w8, gate2, final_g, S, final_norm):
    T, D = x.shape
    W = D // 2
    sub = _rows_per_token(W)
    tm = TOKEN_TILE
    nb = S // tm
    n_steps = T // tm
    row = lambda i: (i, 0)
    return pl.pallas_call(
        functools.partial(_combine_kernel, tm=tm, n_steps=n_steps, width=W, final_norm=final_norm),
        out_shape=jax.ShapeDtypeStruct((T, D), F32),
        grid=(n_steps,),
        in_specs=[pl.BlockSpec((tm * TOP_K,), lambda i: (i,), memory_space=pltpu.SMEM),
                  pl.BlockSpec((tm * TOP_K,), lambda i: (jnp.minimum(i + 1, n_steps - 1),), memory_space=pltpu.SMEM),
                  pl.BlockSpec((tm, D), row), pl.BlockSpec((tm * sub, LANES), row),
                  pl.BlockSpec((tm, LANES), row),
                  pl.BlockSpec((None, 1, D), lambda i: (i // nb, 0, 0)),
                  pl.BlockSpec((1, D), lambda i: (0, 0)),
                  pl.BlockSpec(memory_space=pl.ANY)],
        out_specs=pl.BlockSpec((tm, D), row),
        scratch_shapes=[pltpu.VMEM((2, TOP_K * tm * sub, LANES), U32), pltpu.SemaphoreType.DMA((2,))],
        compiler_params=_cparams(1, 32),
        name="combine",
    )(dest_flat, dest_flat, x, ysh, w8, gate2, final_g, ys)


def _pad_lanes(v, offset=0):
    out = jnp.zeros((1, LANES), F32)
    return out.at[0, offset:offset + v.shape[0]].set(v.astype(F32))


def kernel(x, c, positions, attn_norm, w_ada, b_ada, w_in, fox_bias, gla_w_gate, gla_b_gate, gla_norm,
           w_branch, w_out, ffn_norm, w_router, router_bias, w_exp_gate, w_exp_up, w_exp_down,
           w_sh_gate, w_sh_up, w_sh_down, final_norm):
    B, S, D = x.shape
    L = w_ada.shape[0]
    T = B * S
    E = N_EXPERTS
    M = EXPERT_ROWS
    xf = x.reshape(T, D)

    c_pad = jnp.zeros((SUBLANES, D), F32).at[:B].set(c)
    mod = _ada_mod(c_pad, w_ada, b_ada)

    half = ROPE_DIM // 2
    inv_freq = jnp.power(ROPE_THETA, -jnp.arange(half, dtype=F32) * 2.0 / ROPE_DIM)
    freq_lanes = _pad_lanes(jnp.concatenate([inv_freq, inv_freq]))
    pos_col = positions.reshape(T, 1)

    w_in2 = w_in.reshape(L * D, w_in.shape[2])
    w_branch_bf16 = w_branch.astype(BF16)
    n_blocks = (T * TOP_K + E * (M - 1) + M - 1) // M
    P = n_blocks * M

    for l in range(L):
        m6 = mod[l, :B].reshape(B, 6, 1, D)
        shift1, scale1, gate1, shift2, scale2, gate2 = (m6[:, n] for n in range(6))

        h1 = _norm_mod(xf, attn_norm[l].reshape(1, D), scale1, shift1, S)
        proj = _in_proj(h1, w_in2, l)
        misc = _misc_proj(h1, w_in2, l)

        o_a = _sb_attention(proj, B, S, COL_SB)
        qa, ka, va = _fox_prep(proj, misc, _pad_lanes(fox_bias[l], MISC_FOX_F), B, S)
        o_b = _flash_attention(qa, ka, va, B, S)
        qa, ka, va = _moba_prep(proj, pos_col, freq_lanes, B, S)
        o_c = _flash_attention(qa, ka, va, B, S)
        wg_pad = jnp.zeros((LANES, gla_w_gate.shape[2]), F32).at[MISC_GLR:MISC_GLR + GLA_GATE_RANK].set(gla_w_gate[l])
        o_d = _gla(proj, misc, wg_pad, gla_b_gate[l].reshape(1, -1), gla_norm[l].reshape(1, -1), B, S)

        merged = _merge((o_a, o_b, o_c, o_d), proj, w_branch_bf16, l)
        xf = _matmul(merged, w_out, tm=512, tn=1024, residual=(xf, gate1), S=S, layer=l)

        wr_pad = jnp.zeros((D, LANES), F32).at[:, :E].set(w_router[l])
        hp, logits = _norm_mod(xf, ffn_norm[l].reshape(1, D), scale2, shift2, S, w_router_pad=wr_pad)
        e8, w8, p8, counts = _route(logits, _pad_lanes(router_bias[l]))
        cnt = counts[0, :E].astype(I32)
        padded = (cnt + M - 1) // M * M
        pad_end = jnp.cumsum(padded)
        pad_start = pad_end - padded
        dest = (pad_start[e8[:, :TOP_K]] + p8[:, :TOP_K]).reshape(T * TOP_K)
        blk_start = jnp.arange(n_blocks, dtype=I32) * M
        blk_e = jnp.minimum(jnp.sum((pad_end[None, :] <= blk_start[:, None]).astype(I32), axis=1), E - 1)
        n_used = (pad_end[E - 1:] // M).astype(I32)

        xs = _dispatch(hp, dest, pad_end.astype(I32), padded.astype(I32), P, _rows_per_token(D // 2))
        ys = _experts(xs, blk_e, _next_expert(blk_e, cnt), n_used, w_exp_gate, w_exp_up, w_exp_down, l)
        ysh = _experts(hp, jnp.zeros((T // M,), I32), jnp.full((T // M,), -1, I32), jnp.full((1,), T // M, I32),
                       w_sh_gate[:, None], w_sh_up[:, None], w_sh_down[:, None], l)
        last = l == L - 1
        xf = _combine(xf, ysh, ys, dest, w8, gate2, final_norm.reshape(1, D), S, final_norm=last)

    return xf.reshape(B, S, D)
```

```python
import functools

import jax
import jax.numpy as jnp
from jax import lax
from jax.experimental import pallas as pl
from jax.experimental.pallas import tpu as pltpu

F32 = jnp.float32
BF16 = jnp.bfloat16
I32 = jnp.int32
U32 = jnp.uint32

HEAD_DIM = 128
N_HEADS = 4
BRANCH_WIDTH = N_HEADS * HEAD_DIM
GLA_HEAD_K = 64
GLA_CHUNK = 64
GLA_GATE_RANK = 16
GLA_GATE_NORMALIZER = 16.0
MOBA_BLOCK = 256
MOBA_TOPK = 3
ROPE_THETA = 500000.0
ROPE_DIM = HEAD_DIM // 4
N_EXPERTS = 64
GROUP_SIZE = 8
TOPK_GROUPS = 4
TOP_K = 8
ROUTED_SCALE = 2.5
RMS_EPS = 1e-6

LANES = 128
SUBLANES = 8
MIB = 1024 * 1024
NEG_BIG = -1e30

ATT_BLOCK = 256
FLASH_Q_BLOCK = 512
EXPERT_ROWS = 256
TOKEN_TILE = 128

COL_SB, COL_FOX, COL_MOBA = 0, 12, 24
COL_GQ, COL_GK, COL_GV, COL_GR, COL_GATES = 36, 38, 40, 44, 48
N_MAIN_COLS = 112 * LANES
MISC_FOX_F, MISC_GLR = 0, 4


def _cparams(n_axes, vmem_mib):
    return pltpu.CompilerParams(dimension_semantics=("arbitrary",) * n_axes,
                                vmem_limit_bytes=vmem_mib * MIB)


def _softplus(z):
    return jnp.maximum(z, 0.0) + jnp.log(1.0 + jnp.exp(-jnp.abs(z)))


def _log_sigmoid(z):
    return -_softplus(-z)


def _sigmoid(z):
    return 1.0 / (1.0 + jnp.exp(-z))


def _dot(a, b, precision=None):
    return jnp.dot(a, b, preferred_element_type=F32, precision=precision)


def _dot_nt(a, b, precision=None):
    return lax.dot_general(a, b, (((1,), (1,)), ((), ())), preferred_element_type=F32,
                           precision=precision)


def _dot_tn(a, b, precision=None):
    return lax.dot_general(a, b, (((0,), (0,)), ((), ())), preferred_element_type=F32,
                           precision=precision)


HIGHEST = lax.Precision.HIGHEST


def _split3(x):
    hi = x.astype(BF16)
    r1 = x - hi.astype(F32)
    mid = r1.astype(BF16)
    lo = (r1 - mid.astype(F32)).astype(BF16)
    return hi, mid, lo


def _ada_kernel(c_ref, w_ref, b_ref, o_ref):
    c = c_ref[...]
    c_act = (c * _sigmoid(c)).astype(BF16)
    o_ref[...] = _dot(c_act, w_ref[...].astype(BF16)) + b_ref[...]


def _ada_mod(c_pad, w_ada, b_ada):
    L, D, N = w_ada.shape
    tn = 1536
    return pl.pallas_call(
        _ada_kernel,
        out_shape=jax.ShapeDtypeStruct((L, SUBLANES, N), F32),
        grid=(L, N // tn),
        in_specs=[pl.BlockSpec((SUBLANES, D), lambda l, j: (0, 0)),
                  pl.BlockSpec((None, D, tn), lambda l, j: (l, 0, j)),
                  pl.BlockSpec((None, 1, tn), lambda l, j: (l, 0, j))],
        out_specs=pl.BlockSpec((None, SUBLANES, tn), lambda l, j: (l, 0, j)),
        compiler_params=_cparams(2, 40),
        name="ada_mod",
    )(c_pad, w_ada, b_ada.reshape(L, 1, N))


def _normed(x, g, scale, shift):
    y = x * lax.rsqrt(jnp.mean(x * x, axis=-1, keepdims=True) + RMS_EPS)
    return (y * g) * (1.0 + scale) + shift


def _norm_mod_kernel(x_ref, g_ref, sc_ref, sh_ref, h_ref):
    h_ref[...] = _normed(x_ref[...], g_ref[...], sc_ref[...], sh_ref[...]).astype(BF16)


def _pack_pairs(h):
    half = h.shape[1] // 2
    hi = lax.bitcast_convert_type(h[:, :half].astype(BF16).astype(F32), U32)
    lo = lax.bitcast_convert_type(h[:, half:].astype(BF16).astype(F32), U32)
    return hi | (lo >> 16)


def _unpack_pairs(w):
    a = lax.bitcast_convert_type(w & jnp.uint32(0xFFFF0000), F32).astype(BF16)
    b = lax.bitcast_convert_type(w << 16, F32).astype(BF16)
    return jnp.concatenate([a, b], axis=1)


def _rows_per_token(width):
    return width // LANES


def _store_token_rows(ref, val):
    n, width = val.shape
    sub = _rows_per_token(width)
    for s in range(sub):
        ref[pl.ds(s, n, stride=sub), :] = val[:, s * LANES:(s + 1) * LANES]


def _load_token_rows(ref, first_token, n, width):
    sub = _rows_per_token(width)
    return jnp.concatenate([ref[pl.ds(first_token * sub + s, n, stride=sub), :] for s in range(sub)], axis=1)


def _norm_route_kernel(x_ref, g_ref, sc_ref, sh_ref, wr_ref, hp_ref, lg_ref):
    h = _normed(x_ref[...], g_ref[...], sc_ref[...], sh_ref[...])
    _store_token_rows(hp_ref, _pack_pairs(h))
    lg_ref[...] = _dot(h, wr_ref[...], precision=HIGHEST)


def _norm_mod(x, g, scale, shift, S, w_router_pad=None):
    T, D = x.shape
    tm = 512
    nb = S // tm
    row = lambda i: (i, 0)
    per_batch = pl.BlockSpec((None, 1, D), lambda i: (i // nb, 0, 0))
    in_specs = [pl.BlockSpec((tm, D), row), pl.BlockSpec((1, D), lambda i: (0, 0)), per_batch, per_batch]
    if w_router_pad is None:
        return pl.pallas_call(
            _norm_mod_kernel, out_shape=jax.ShapeDtypeStruct((T, D), BF16), grid=(T // tm,),
            in_specs=in_specs, out_specs=pl.BlockSpec((tm, D), row),
            compiler_params=_cparams(1, 32), name="norm_mod",
        )(x, g, scale, shift)
    return pl.pallas_call(
        _norm_route_kernel,
        out_shape=(jax.ShapeDtypeStruct((T * (D // 2) // LANES, LANES), U32), jax.ShapeDtypeStruct((T, LANES), F32)),
        grid=(T // tm,),
        in_specs=in_specs + [pl.BlockSpec((D, LANES), lambda i: (0, 0))],
        out_specs=(pl.BlockSpec((tm * (D // 2) // LANES, LANES), row), pl.BlockSpec((tm, LANES), row)),
        compiler_params=_cparams(1, 32), name="norm_route",
    )(x, g, scale, shift, w_router_pad)


def _mm_kernel(*refs, cast_w, residual):
    if residual:
        a_ref, w_ref, x_ref, g_ref, o_ref = refs[:5]
        scratch = refs[5:]
    else:
        a_ref, w_ref, o_ref = refs[:3]
        scratch = refs[3:]
    if cast_w:
        wbf_ref, = scratch

        @pl.when(pl.program_id(1) == 0)
        def _():
            wbf_ref[...] = w_ref[...].astype(BF16)

        w = wbf_ref[...]
    else:
        w = w_ref[...]
    acc = _dot(a_ref[...], w)
    if residual:
        acc = x_ref[...] + g_ref[...] * acc
    o_ref[...] = acc.astype(o_ref.dtype)


def _matmul(a, w, tm, tn, out_dtype=F32, residual=None, S=None, vmem_mib=48, layer=None):
    M, K = a.shape
    N = w.shape[-1]
    cast_w = w.dtype != BF16
    if layer is None:
        w_spec = pl.BlockSpec((K, tn), lambda j, i: (0, j))
    else:
        w_spec = pl.BlockSpec((None, K, tn), lambda j, i: (layer, 0, j))
    in_specs = [pl.BlockSpec((tm, K), lambda j, i: (i, 0)), w_spec]
    args = [a, w]
    if residual is not None:
        x, gate = residual
        nb = S // tm
        in_specs += [pl.BlockSpec((tm, tn), lambda j, i: (i, j)),
                     pl.BlockSpec((None, 1, tn), lambda j, i: (i // nb, 0, j))]
        args += [x, gate]
    return pl.pallas_call(
        functools.partial(_mm_kernel, cast_w=cast_w, residual=residual is not None),
        out_shape=jax.ShapeDtypeStruct((M, N), out_dtype),
        grid=(N // tn, M // tm),
        in_specs=in_specs,
        out_specs=pl.BlockSpec((tm, tn), lambda j, i: (i, j)),
        scratch_shapes=[pltpu.VMEM((K, tn), BF16)] if cast_w else [],
        compiler_params=_cparams(2, vmem_mib),
        name="matmul_res" if residual is not None else "matmul",
    )(*args)


IN_TN = 1024
IN_SEGMENTS = ((0, COL_MOBA * LANES // IN_TN, 0),
               (COL_MOBA * LANES // IN_TN, COL_GATES * LANES // IN_TN, N_HEADS),
               (COL_GATES * LANES // IN_TN, N_MAIN_COLS // IN_TN, N_HEADS + GLA_GATE_RANK))
ROW_CHUNK = 256


def _in_proj_kernel(a_ref, w_ref, wn_ref, o_ref, wbf_ref, *, tn):
    j = pl.program_id(0)

    @pl.when(pl.program_id(1) == 0)
    def _():
        K = w_ref.shape[0]
        for lo, hi, shift in IN_SEGMENTS:
            @pl.when(jnp.logical_and(j >= lo, j < hi))
            def _():
                for r in range(0, K, ROW_CHUNK):
                    rows = slice(r, r + ROW_CHUNK)
                    if shift == 0:
                        wbf_ref[rows, :] = w_ref[rows, :].astype(BF16)
                    else:
                        wide = jnp.concatenate([w_ref[rows, :], wn_ref[rows, :]], axis=1)
                        wbf_ref[rows, :] = wide[:, shift:shift + tn].astype(BF16)

    o_ref[...] = _dot(a_ref[...], wbf_ref[...]).astype(o_ref.dtype)


def _in_proj(a, w_in, layer, tm=1024):
    M, K = a.shape
    tn = IN_TN
    return pl.pallas_call(
        functools.partial(_in_proj_kernel, tn=tn),
        out_shape=jax.ShapeDtypeStruct((M, N_MAIN_COLS), BF16),
        grid=(N_MAIN_COLS // tn, M // tm),
        in_specs=[pl.BlockSpec((tm, K), lambda j, i: (i, 0)),
                  pl.BlockSpec((K, tn), lambda j, i: (layer, j)),
                  pl.BlockSpec((K, LANES), lambda j, i: (layer, (j + 1) * (tn // LANES)))],
        out_specs=pl.BlockSpec((tm, tn), lambda j, i: (i, j)),
        scratch_shapes=[pltpu.VMEM((K, tn), BF16)],
        compiler_params=_cparams(2, 48),
        name="in_proj",
    )(a, w_in, w_in)


def _misc_proj_kernel(a_ref, wf_ref, wg_ref, o_ref, wbf_ref):
    @pl.when(pl.program_id(0) == 0)
    def _():
        lane = lax.broadcasted_iota(I32, wf_ref.shape, 1)
        w = jnp.where(lane < MISC_GLR, wf_ref[...],
                      jnp.where(lane < MISC_GLR + GLA_GATE_RANK, wg_ref[...], 0.0))
        wbf_ref[...] = w.astype(BF16)

    o_ref[...] = _dot(a_ref[...], wbf_ref[...])


def _misc_proj(a, w_in, layer, tm=1024):
    M, K = a.shape
    return pl.pallas_call(
        _misc_proj_kernel,
        out_shape=jax.ShapeDtypeStruct((M, LANES), F32),
        grid=(M // tm,),
        in_specs=[pl.BlockSpec((tm, K), lambda i: (i, 0)),
                  pl.BlockSpec((K, LANES), lambda i: (layer, COL_MOBA)),
                  pl.BlockSpec((K, LANES), lambda i: (layer, COL_GATES))],
        out_specs=pl.BlockSpec((tm, LANES), lambda i: (i, 0)),
        scratch_shapes=[pltpu.VMEM((K, LANES), BF16)],
        compiler_params=_cparams(1, 32),
        name="misc_proj",
    )(a, w_in, w_in)


SB_LOG_WEIGHT_FLOOR = -110.0


def _sb_kernel(q_ref, k_ref, v_ref, o_ref, *, blk, scale):
    i = pl.program_id(1)
    row = lax.broadcasted_iota(I32, (blk, blk), 0)
    col = lax.broadcasted_iota(I32, (blk, blk), 1)
    later = (row > col).astype(BF16)

    def cond(state):
        jj, alive, _ = state
        return jnp.logical_and(jj <= i, alive)

    def body(state):
        jj, _, heads = state
        j = i - jj
        start = pl.multiple_of(j * blk, blk)
        past = jnp.logical_or(j < i, col < row)
        new, top = [], None
        for h in range(N_HEADS):
            c, acc = heads[h]
            hs = slice(h * HEAD_DIM, (h + 1) * HEAD_DIM)
            q = (q_ref[:, hs].astype(F32) * scale).astype(BF16)
            k = k_ref[pl.ds(start, blk), hs]
            v = v_ref[pl.ds(start, blk), hs]
            z = _dot_nt(q, k)
            ls = jnp.where(past, -_softplus(z), 0.0)
            hi = ls.astype(BF16)
            lo = (ls - hi.astype(F32)).astype(BF16)
            between = _dot(hi, later) + _dot(lo, later)
            w = jnp.where(past, jnp.exp(z + ls + between + c), 0.0)
            acc = acc + _dot(w.astype(BF16), v)
            c = c + jnp.sum(ls, axis=1, keepdims=True)
            new.append((c, acc))
            top = c if top is None else jnp.maximum(top, c)
        return jj + 1, jnp.max(top) > SB_LOG_WEIGHT_FLOOR, tuple(new)

    init = tuple((jnp.zeros((blk, 1), F32), jnp.zeros((blk, HEAD_DIM), F32)) for _ in range(N_HEADS))
    _, _, heads = lax.while_loop(cond, body, (jnp.int32(0), jnp.bool_(True), init))
    for h in range(N_HEADS):
        o_ref[:, h * HEAD_DIM:(h + 1) * HEAD_DIM] = heads[h][1]


def _sb_attention(proj, B, S, col0):
    T = proj.shape[0]
    blk = ATT_BLOCK
    nq = S // blk
    W = BRANCH_WIDTH
    cb = col0 * LANES // W
    return pl.pallas_call(
        functools.partial(_sb_kernel, blk=blk, scale=HEAD_DIM ** -0.5),
        out_shape=jax.ShapeDtypeStruct((T, W), F32),
        grid=(B, nq),
        in_specs=[pl.BlockSpec((blk, W), lambda b, i: (b * nq + i, cb)),
                  pl.BlockSpec((S, W), lambda b, i: (b, cb + 1)),
                  pl.BlockSpec((S, W), lambda b, i: (b, cb + 2))],
        out_specs=pl.BlockSpec((blk, W), lambda b, i: (b * nq + i, 0)),
        compiler_params=_cparams(2, 44),
        name="sb_attention",
    )(proj, proj, proj)


def _flash_kernel(qa_ref, ka_ref, v_ref, o_ref, *, bq, bk):
    i = pl.program_id(1)
    A = 2 * HEAD_DIM
    ratio = bq // bk
    row = lax.broadcasted_iota(I32, (bq, bk), 0)
    col = lax.broadcasted_iota(I32, (bq, bk), 1)

    def step(j, carry, key_offset=None):
        start = pl.multiple_of(j * bk, bk)
        new = []
        for h in range(N_HEADS):
            m, l, acc = carry[h]
            s = _dot_nt(qa_ref[:, h * A:(h + 1) * A], ka_ref[pl.ds(start, bk), h * A:(h + 1) * A])
            if key_offset is not None:
                s = jnp.where(col + key_offset <= row, s, NEG_BIG)
            m_new = jnp.maximum(m, jnp.max(s, axis=1, keepdims=True))
            alpha = jnp.exp(m - m_new)
            p = jnp.exp(s - m_new)
            l = alpha * l + jnp.sum(p, axis=1, keepdims=True)
            acc = alpha * acc + _dot(p.astype(BF16), v_ref[pl.ds(start, bk), h * HEAD_DIM:(h + 1) * HEAD_DIM])
            new.append((m_new, l, acc))
        return tuple(new)

    init = tuple((jnp.full((bq, 1), NEG_BIG, F32), jnp.zeros((bq, 1), F32),
                  jnp.zeros((bq, HEAD_DIM), F32)) for _ in range(N_HEADS))
    heads = lax.fori_loop(0, i * ratio, step, init)
    for d in range(ratio):
        heads = step(i * ratio + d, heads, key_offset=d * bk)
    for h in range(N_HEADS):
        _, l, acc = heads[h]
        o_ref[:, h * HEAD_DIM:(h + 1) * HEAD_DIM] = acc / l


def _flash_attention(qa, ka, va, B, S):
    T = qa.shape[0]
    bq, bk = FLASH_Q_BLOCK, ATT_BLOCK
    nq = S // bq
    W = BRANCH_WIDTH
    return pl.pallas_call(
        functools.partial(_flash_kernel, bq=bq, bk=bk),
        out_shape=jax.ShapeDtypeStruct((T, W), F32),
        grid=(B, nq),
        in_specs=[pl.BlockSpec((bq, 2 * W), lambda b, i: (b * nq + i, 0)),
                  pl.BlockSpec((S, 2 * W), lambda b, i: (b, 0)),
                  pl.BlockSpec((S, W), lambda b, i: (b, 0))],
        out_specs=pl.BlockSpec((bq, W), lambda b, i: (b * nq + i, 0)),
        compiler_params=_cparams(2, 44),
        name="flash_attention",
    )(qa, ka, va)


def _fox_prep_kernel(q_ref, k_ref, v_ref, misc_ref, bias_ref, qa_ref, ka_ref, va_ref, carry_ref, *, tb, scale):
    va_ref[...] = v_ref[...].astype(BF16)
    @pl.when(pl.program_id(1) == 0)
    def _():
        carry_ref[...] = jnp.zeros_like(carry_ref)

    lane = lax.broadcasted_iota(I32, (tb, LANES), 1)
    lf = jnp.where(lane < N_HEADS, _log_sigmoid(misc_ref[...] + bias_ref[...]), 0.0)
    row = lax.broadcasted_iota(I32, (tb, tb), 0)
    col = lax.broadcasted_iota(I32, (tb, tb), 1)
    incl = (col <= row).astype(F32)
    F = _dot(incl, lf, precision=HIGHEST) + carry_ref[0:1, :]
    carry_ref[0:1, :] = F[tb - 1:tb, :]
    for h in range(N_HEADS):
        Fh = jnp.broadcast_to(F[:, h:h + 1], (tb, LANES))
        hi, mid, lo = (p.astype(F32) for p in _split3(Fh))
        ones = jnp.ones((tb, LANES), F32)
        zeros = jnp.zeros((tb, LANES), F32)
        eq = jnp.where(lane == 0, hi, jnp.where(lane == 1, mid, jnp.where(lane == 2, lo,
             jnp.where(lane < 6, ones, zeros))))
        ek = jnp.where(lane < 3, ones, jnp.where(lane == 3, -hi, jnp.where(lane == 4, -mid,
             jnp.where(lane == 5, -lo, zeros))))
        hs = slice(h * HEAD_DIM, (h + 1) * HEAD_DIM)
        qa_ref[:, 2 * h * HEAD_DIM:(2 * h + 1) * HEAD_DIM] = (q_ref[:, hs].astype(F32) * scale).astype(BF16)
        qa_ref[:, (2 * h + 1) * HEAD_DIM:(2 * h + 2) * HEAD_DIM] = eq.astype(BF16)
        ka_ref[:, 2 * h * HEAD_DIM:(2 * h + 1) * HEAD_DIM] = k_ref[:, hs].astype(BF16)
        ka_ref[:, (2 * h + 1) * HEAD_DIM:(2 * h + 2) * HEAD_DIM] = ek.astype(BF16)


def _fox_prep(proj, misc, fox_bias_pad, B, S):
    T = proj.shape[0]
    tb = 512
    nb = S // tb
    qblk = COL_FOX * LANES // BRANCH_WIDTH
    out = jax.ShapeDtypeStruct((T, 2 * BRANCH_WIDTH), BF16)
    return pl.pallas_call(
        functools.partial(_fox_prep_kernel, tb=tb, scale=HEAD_DIM ** -0.5),
        out_shape=(out, out, jax.ShapeDtypeStruct((T, BRANCH_WIDTH), BF16)),
        grid=(B, nb),
        in_specs=[pl.BlockSpec((tb, BRANCH_WIDTH), lambda b, i: (b * nb + i, qblk)),
                  pl.BlockSpec((tb, BRANCH_WIDTH), lambda b, i: (b * nb + i, qblk + 1)),
                  pl.BlockSpec((tb, BRANCH_WIDTH), lambda b, i: (b * nb + i, qblk + 2)),
                  pl.BlockSpec((tb, LANES), lambda b, i: (b * nb + i, 0)),
                  pl.BlockSpec((1, LANES), lambda b, i: (0, 0))],
        out_specs=(pl.BlockSpec((tb, 2 * BRANCH_WIDTH), lambda b, i: (b * nb + i, 0)),
                   pl.BlockSpec((tb, 2 * BRANCH_WIDTH), lambda b, i: (b * nb + i, 0)),
                   pl.BlockSpec((tb, BRANCH_WIDTH), lambda b, i: (b * nb + i, 0))),
        scratch_shapes=[pltpu.VMEM((SUBLANES, LANES), F32)],
        compiler_params=_cparams(2, 32),
        name="fox_prep",
    )(proj, proj, proj, misc, fox_bias_pad)


def _rope(x, cos, sin, lane):
    half = ROPE_DIM // 2
    up = pltpu.roll(x, half, 1)
    down = pltpu.roll(x, LANES - half, 1)
    rot = jnp.where(lane < half, -down * sin, jnp.where(lane < ROPE_DIM, up * sin, 0.0))
    return x * jnp.where(lane < ROPE_DIM, cos, 1.0) + rot


def _rope_table_kernel(pos_ref, freq_ref, cos_ref, sin_ref):
    ang = pos_ref[...].astype(F32) * freq_ref[...]
    cos_ref[...] = jnp.cos(ang)
    sin_ref[...] = jnp.sin(ang)


def _rope_tables(pos_col, freq_lanes):
    T = pos_col.shape[0]
    tb = 1024
    row = lambda i: (i, 0)
    out = jax.ShapeDtypeStruct((T, LANES), F32)
    return pl.pallas_call(
        _rope_table_kernel, out_shape=(out, out), grid=(T // tb,),
        in_specs=[pl.BlockSpec((tb, 1), row), pl.BlockSpec((1, LANES), lambda i: (0, 0))],
        out_specs=(pl.BlockSpec((tb, LANES), row), pl.BlockSpec((tb, LANES), row)),
        compiler_params=_cparams(1, 32), name="rope_tables",
    )(pos_col, freq_lanes)


def _moba_k_kernel(k_ref, v_ref, cos_ref, sin_ref, ka_ref, va_ref, kmean_ref, *, tb):
    i = pl.program_id(1)
    va_ref[...] = v_ref[...].astype(BF16)

    @pl.when(i == 0)
    def _():
        kmean_ref[...] = jnp.zeros_like(kmean_ref)

    lane = lax.broadcasted_iota(I32, (tb, LANES), 1)
    cos, sin = cos_ref[...], sin_ref[...]
    onehot = jnp.where(lane == i, 1.0, 0.0).astype(BF16)
    this_row = lax.broadcasted_iota(I32, (LANES, HEAD_DIM), 0) == i
    for h in range(N_HEADS):
        hs = slice(h * HEAD_DIM, (h + 1) * HEAD_DIM)
        kr = _rope(k_ref[:, hs].astype(F32), cos, sin, lane)
        mean = jnp.sum(kr, axis=0, keepdims=True) * (1.0 / tb)
        kmean_ref[:, hs] = jnp.where(this_row, mean, kmean_ref[:, hs])
        ka_ref[:, 2 * h * HEAD_DIM:(2 * h + 1) * HEAD_DIM] = kr.astype(BF16)
        ka_ref[:, (2 * h + 1) * HEAD_DIM:(2 * h + 2) * HEAD_DIM] = onehot


def _moba_q_kernel(q_ref, cos_ref, sin_ref, kmean_ref, qa_ref, *, tb, scale):
    i = pl.program_id(1)
    lane = lax.broadcasted_iota(I32, (tb, LANES), 1)
    cos, sin = cos_ref[...], sin_ref[...]
    neg_inf = jnp.float32(-jnp.inf)
    for h in range(N_HEADS):
        hs = slice(h * HEAD_DIM, (h + 1) * HEAD_DIM)
        qr = _rope(q_ref[:, hs].astype(F32), cos, sin, lane)
        gate = _dot_nt(qr, kmean_ref[:, hs], precision=HIGHEST)
        cur = jnp.where(lane < i, gate, neg_inf)
        chosen = lane == i
        for _ in range(MOBA_TOPK):
            m = jnp.max(cur, axis=1, keepdims=True)
            first = jnp.min(jnp.where(jnp.logical_and(cur == m, m > neg_inf), lane, LANES),
                            axis=1, keepdims=True)
            pick = lane == first
            chosen = jnp.logical_or(chosen, pick)
            cur = jnp.where(pick, neg_inf, cur)
        bias = jnp.where(jnp.logical_or(chosen, lane >= LANES // 2), 0.0, NEG_BIG)
        qa_ref[:, 2 * h * HEAD_DIM:(2 * h + 1) * HEAD_DIM] = (qr * scale).astype(BF16)
        qa_ref[:, (2 * h + 1) * HEAD_DIM:(2 * h + 2) * HEAD_DIM] = bias.astype(BF16)


def _moba_prep(proj, cos_tab, sin_tab, B, S):
    T = proj.shape[0]
    tb = MOBA_BLOCK
    nb = S // tb
    qblk = COL_MOBA * LANES // BRANCH_WIDTH
    aug = jax.ShapeDtypeStruct((T, 2 * BRANCH_WIDTH), BF16)
    row_spec = lambda c: pl.BlockSpec((tb, BRANCH_WIDTH), lambda b, i: (b * nb + i, c))
    trig_spec = pl.BlockSpec((tb, LANES), lambda b, i: (b * nb + i, 0))
    aug_spec = pl.BlockSpec((tb, 2 * BRANCH_WIDTH), lambda b, i: (b * nb + i, 0))
    kmean_spec = pl.BlockSpec((None, LANES, BRANCH_WIDTH), lambda b, i: (b, 0, 0))
    ka, va, kmean = pl.pallas_call(
        functools.partial(_moba_k_kernel, tb=tb),
        out_shape=(aug, jax.ShapeDtypeStruct((T, BRANCH_WIDTH), BF16),
                   jax.ShapeDtypeStruct((B, LANES, BRANCH_WIDTH), F32)),
        grid=(B, nb),
        in_specs=[row_spec(qblk + 1), row_spec(qblk + 2), trig_spec, trig_spec],
        out_specs=(aug_spec, pl.BlockSpec((tb, BRANCH_WIDTH), lambda b, i: (b * nb + i, 0)), kmean_spec),
        compiler_params=_cparams(2, 32),
        name="moba_k_prep",
    )(proj, proj, cos_tab, sin_tab)
    qa = pl.pallas_call(
        functools.partial(_moba_q_kernel, tb=tb, scale=HEAD_DIM ** -0.5),
        out_shape=aug,
        grid=(B, nb),
        in_specs=[row_spec(qblk), trig_spec, trig_spec, kmean_spec],
        out_specs=aug_spec,
        compiler_params=_cparams(2, 32),
        name="moba_q_prep",
    )(proj, cos_tab, sin_tab, kmean)
    return qa, ka, va


def _gla_kernel(q_ref, k_ref, v_ref, gr_ref, misc_ref, wg_ref, bg_ref, gn_ref, o_ref, state_ref, *, tb):
    C = GLA_CHUNK
    @pl.when(pl.program_id(1) == 0)
    def _():
        state_ref[...] = jnp.zeros_like(state_ref)

    la_all = _log_sigmoid(_dot(misc_ref[...], wg_ref[...], precision=HIGHEST) + bg_ref[...]) \
        * (1.0 / GLA_GATE_NORMALIZER)
    rowc = lax.broadcasted_iota(I32, (C, C), 0)
    colc = lax.broadcasted_iota(I32, (C, C), 1)
    incl = (colc <= rowc).astype(F32)
    causal = colc <= rowc
    lane = lax.broadcasted_iota(I32, (C, LANES), 1)
    row2 = lax.broadcasted_iota(I32, (2 * GLA_HEAD_K, LANES), 0)
    ones_cv = jnp.ones((C, LANES), F32)
    qscale = GLA_HEAD_K ** -0.5
    gn = gn_ref[...]
    for c in range(tb // C):
        rs = slice(c * C, (c + 1) * C)
        for p in range(2):
            ps = slice(p * LANES, (p + 1) * LANES)
            la = la_all[rs, ps]
            b = _dot(incl, la, precision=HIGHEST)
            b_last = b[C - 1:C, :]
            eb = jnp.exp(b)
            q_dec = q_ref[rs, ps].astype(F32) * qscale * eb
            k = k_ref[rs, ps].astype(F32)
            k_inv = (k * jnp.exp(-b)).astype(BF16)
            k_end = (k * jnp.exp(b_last - b)).astype(BF16)
            decay = jnp.exp(_dot_tn(la, ones_cv, precision=HIGHEST))
            st = state_ref[p * LANES:(p + 1) * LANES, :]
            new_rows = []
            for e in range(2):
                h = 2 * p + e
                own = jnp.logical_and(lane >= e * GLA_HEAD_K, lane < (e + 1) * GLA_HEAD_K)
                qh = jnp.where(own, q_dec, 0.0).astype(BF16)
                v = v_ref[rs, h * HEAD_DIM:(h + 1) * HEAD_DIM].astype(BF16)
                attn = jnp.where(causal, _dot_nt(qh, k_inv), 0.0)
                st_h = jnp.where(jnp.logical_and(row2 >= e * GLA_HEAD_K, row2 < (e + 1) * GLA_HEAD_K), st, 0.0)
                o = _dot(attn.astype(BF16), v) + _dot(qh, st_h.astype(BF16))
                y = o * lax.rsqrt(jnp.mean(o * o, axis=-1, keepdims=True) + RMS_EPS) * gn
                g = gr_ref[rs, h * HEAD_DIM:(h + 1) * HEAD_DIM].astype(F32)
                o_ref[rs, h * HEAD_DIM:(h + 1) * HEAD_DIM] = y * (g * _sigmoid(g))
                new_rows.append(_dot_tn(k_end, v))
            kv = jnp.where(row2 < GLA_HEAD_K, new_rows[0], new_rows[1])
            state_ref[p * LANES:(p + 1) * LANES, :] = decay * st + kv


def _gla(proj, misc, w_gate_pad, b_gate, gla_norm, B, S):
    T = proj.shape[0]
    tb = 512
    nb = S // tb
    kd = 2 * LANES
    rows = lambda w, c: pl.BlockSpec((tb, w), lambda b, i: (b * nb + i, c))
    const = lambda shape: pl.BlockSpec(shape, lambda b, i: (0, 0))
    return pl.pallas_call(
        functools.partial(_gla_kernel, tb=tb),
        out_shape=jax.ShapeDtypeStruct((T, BRANCH_WIDTH), F32),
        grid=(B, nb),
        in_specs=[rows(kd, COL_GQ * LANES // kd), rows(kd, COL_GK * LANES // kd),
                  rows(BRANCH_WIDTH, COL_GV * LANES // BRANCH_WIDTH),
                  rows(BRANCH_WIDTH, COL_GR * LANES // BRANCH_WIDTH),
                  rows(LANES, 0), const((LANES, kd)), const((1, kd)), const((1, HEAD_DIM))],
        out_specs=rows(BRANCH_WIDTH, 0),
        scratch_shapes=[pltpu.VMEM((kd, HEAD_DIM), F32)],
        compiler_params=_cparams(2, 32),
        name="gla",
    )(proj, proj, proj, proj, misc, w_gate_pad, b_gate, gla_norm)


def _merge_kernel(oa_ref, ob_ref, oc_ref, od_ref, g0_ref, g1_ref, g2_ref, g3_ref, w_ref, o_ref):
    acc = None
    for n, (b_ref, g_ref) in enumerate(((oa_ref, g0_ref), (ob_ref, g1_ref), (oc_ref, g2_ref), (od_ref, g3_ref))):
        term = _sigmoid(g_ref[...].astype(F32)) * _dot(b_ref[...].astype(BF16), w_ref[n])
        acc = term if acc is None else acc + term
    o_ref[...] = acc.astype(BF16)


def _merge(branches, proj, w_branch_bf16, layer):
    T = proj.shape[0]
    _, nbr, W, D = w_branch_bf16.shape
    tm = 256
    g0 = COL_GATES * LANES // D
    br_spec = pl.BlockSpec((tm, W), lambda i: (i, 0))
    gate_spec = lambda n: pl.BlockSpec((tm, D), lambda i: (i, g0 + n))
    return pl.pallas_call(
        _merge_kernel,
        out_shape=jax.ShapeDtypeStruct((T, D), BF16),
        grid=(T // tm,),
        in_specs=[br_spec] * 4 + [gate_spec(n) for n in range(4)]
                 + [pl.BlockSpec((None, nbr, W, D), lambda i: (layer, 0, 0, 0))],
        out_specs=pl.BlockSpec((tm, D), lambda i: (i, 0)),
        compiler_params=_cparams(1, 44),
        name="branch_merge",
    )(*branches, proj, proj, proj, proj, w_branch_bf16)


def _butterfly(x, lane, op):
    for s in (1, 2, 4):
        up = pltpu.roll(x, s, 1)
        down = pltpu.roll(x, LANES - s, 1)
        x = op(x, jnp.where((lane & s) != 0, up, down))
    return x


def _route_kernel(lg_ref, bias_ref, e8_ref, w8_ref, p8_ref, cnt_ref, carry_ref, *, tm):
    @pl.when(pl.program_id(0) == 0)
    def _():
        carry_ref[...] = jnp.zeros_like(carry_ref)

    neg_inf = jnp.float32(-jnp.inf)
    lane = lax.broadcasted_iota(I32, (tm, LANES), 1)
    valid = lane < N_EXPERTS
    scores = _sigmoid(lg_ref[...])
    biased = jnp.where(valid, scores + bias_ref[...], neg_inf)
    g1 = _butterfly(biased, lane, jnp.maximum)
    first = _butterfly(jnp.where(biased == g1, lane, LANES), lane, jnp.minimum)
    g2 = _butterfly(jnp.where(lane == first, neg_inf, biased), lane, jnp.maximum)
    gs = g1 + g2
    gs = jnp.where(valid, gs, pltpu.roll(gs, N_EXPERTS, 1))
    gidx = lane >> 3
    beaten = jnp.zeros((tm, LANES), I32)
    for r in range(1, N_EXPERTS // GROUP_SIZE):
        other = pltpu.roll(gs, GROUP_SIZE * r, 1)
        og = (gidx - r) & (N_EXPERTS // GROUP_SIZE - 1)
        wins = jnp.logical_or(other > gs, jnp.logical_and(other == gs, og < gidx))
        beaten = beaten + wins.astype(I32)
    cur = jnp.where(jnp.logical_and(beaten < TOPK_GROUPS, valid), biased, neg_inf)
    sel = jnp.zeros((tm, LANES), jnp.bool_)
    for _ in range(TOP_K):
        m = jnp.max(cur, axis=1, keepdims=True)
        pick = lane == jnp.min(jnp.where(cur == m, lane, LANES), axis=1, keepdims=True)
        sel = jnp.logical_or(sel, pick)
        cur = jnp.where(pick, neg_inf, cur)
    wsel = jnp.where(sel, scores, 0.0)
    wd = wsel / jnp.sum(wsel, axis=1, keepdims=True) * ROUTED_SCALE
    selb = jnp.where(sel, 1.0, 0.0).astype(BF16)
    row = lax.broadcasted_iota(I32, (tm, tm), 0)
    col = lax.broadcasted_iota(I32, (tm, tm), 1)
    pos = _dot((col < row).astype(BF16), selb) + carry_ref[0:1, :]
    total = carry_ref[0:1, :] + jnp.sum(selb.astype(F32), axis=0, keepdims=True)
    carry_ref[0:1, :] = total
    cnt_ref[...] = jnp.broadcast_to(total, cnt_ref.shape)
    r2 = lax.broadcasted_iota(I32, (LANES, LANES), 0)
    c2 = lax.broadcasted_iota(I32, (LANES, LANES), 1)
    slot = _dot(selb, (r2 < c2).astype(BF16))
    lane_f = lane.astype(F32)
    e8 = jnp.zeros((tm, LANES), F32)
    w8 = jnp.zeros((tm, LANES), F32)
    p8 = jnp.zeros((tm, LANES), F32)
    for k in range(TOP_K):
        mk = jnp.logical_and(sel, slot == k)
        put = lane == k
        e8 = jnp.where(put, jnp.sum(jnp.where(mk, lane_f, 0.0), axis=1, keepdims=True), e8)
        w8 = jnp.where(put, jnp.sum(jnp.where(mk, wd, 0.0), axis=1, keepdims=True), w8)
        p8 = jnp.where(put, jnp.sum(jnp.where(mk, pos, 0.0), axis=1, keepdims=True), p8)
    e8_ref[...] = e8.astype(I32)
    w8_ref[...] = w8
    p8_ref[...] = p8.astype(I32)


def _route(logits, router_bias_pad):
    T = logits.shape[0]
    tm = 512
    row = pl.BlockSpec((tm, LANES), lambda i: (i, 0))
    return pl.pallas_call(
        functools.partial(_route_kernel, tm=tm),
        out_shape=(jax.ShapeDtypeStruct((T, LANES), I32), jax.ShapeDtypeStruct((T, LANES), F32),
                   jax.ShapeDtypeStruct((T, LANES), I32), jax.ShapeDtypeStruct((SUBLANES, LANES), F32)),
        grid=(T // tm,),
        in_specs=[row, pl.BlockSpec((1, LANES), lambda i: (0, 0))],
        out_specs=(row, row, row, pl.BlockSpec((SUBLANES, LANES), lambda i: (0, 0))),
        scratch_shapes=[pltpu.VMEM((SUBLANES, LANES), F32)],
        compiler_params=_cparams(1, 32),
        name="route",
    )(logits, router_bias_pad)


def _dest_kernel(e8_ref, p8_ref, start_ref, d_ref):
    e8 = e8_ref[...]
    lane = lax.broadcasted_iota(I32, e8.shape, 1)
    starts = start_ref[...]
    dest = p8_ref[...]
    for k in range(TOP_K):
        hit = lane == e8[:, k:k + 1]
        base = jnp.sum(jnp.where(hit, starts, 0.0), axis=1, keepdims=True).astype(I32)
        dest = jnp.where(lane == k, dest + base, dest)
    d_ref[...] = dest


def _dest_rows(e8, p8, pad_start_lanes):
    T = e8.shape[0]
    tm = 1024
    row = pl.BlockSpec((tm, LANES), lambda i: (i, 0))
    return pl.pallas_call(
        _dest_kernel, out_shape=jax.ShapeDtypeStruct((T, LANES), I32), grid=(T // tm,),
        in_specs=[row, row, pl.BlockSpec((1, LANES), lambda i: (0, 0))], out_specs=row,
        compiler_params=_cparams(1, 32), name="dest_rows",
    )(e8, p8, pad_start_lanes)


def _token_copy(src, src_token, dst, dst_token, sem, sub):
    return pltpu.make_async_copy(src.at[pl.ds(pl.multiple_of(src_token * sub, sub), sub), :],
                                 dst.at[pl.ds(pl.multiple_of(dst_token * sub, sub), sub), :], sem)


def _dispatch_kernel(pad_end_ref, padded_ref, dest_ref, hp_ref, xs_ref, zero_ref, sem_ref, *, tm, rows, n_blocks, sub):
    @pl.when(pl.program_id(0) == 0)
    def _():
        zero_ref[...] = jnp.zeros_like(zero_ref)

        def fill(e, do_wait):
            @pl.when(padded_ref[e] > 0)
            def _():
                start = pl.multiple_of((pad_end_ref[e] - rows) * sub, rows * sub)
                cp = pltpu.make_async_copy(zero_ref, xs_ref.at[pl.ds(start, rows * sub), :], sem_ref)
                if do_wait:
                    cp.wait()
                else:
                    cp.start()

        lax.fori_loop(0, N_EXPERTS, lambda e, c: (fill(e, False), c)[1], 0)
        lax.fori_loop(0, N_EXPERTS, lambda e, c: (fill(e, True), c)[1], 0)

        def tail(b):
            return pltpu.make_async_copy(zero_ref, xs_ref.at[pl.ds(pl.multiple_of(b * rows * sub, rows * sub), rows * sub), :],
                                         sem_ref)

        n_used = pad_end_ref[N_EXPERTS - 1] // rows
        lax.fori_loop(n_used, n_blocks, lambda b, c: (tail(b).start(), c)[1], 0)
        lax.fori_loop(n_used, n_blocks, lambda b, c: (tail(b).wait(), c)[1], 0)

    def issue(t, c):
        for k in range(TOP_K):
            _token_copy(hp_ref, t, xs_ref, dest_ref[t * TOP_K + k], sem_ref, sub).start(priority=k % 2)
        return c

    def drain(t, c):
        for k in range(TOP_K):
            _token_copy(hp_ref, t, xs_ref, dest_ref[t * TOP_K + k], sem_ref, sub).wait()
        return c

    lax.fori_loop(0, tm, issue, 0)
    lax.fori_loop(0, tm, drain, 0)


def _dispatch(hp, dest_flat, pad_end, padded, n_rows, sub):
    tm = TOKEN_TILE
    T = hp.shape[0] // sub
    grid_spec = pltpu.PrefetchScalarGridSpec(
        num_scalar_prefetch=2,
        grid=(T // tm,),
        in_specs=[pl.BlockSpec((tm * TOP_K,), lambda i, pe, pd: (i,), memory_space=pltpu.SMEM),
                  pl.BlockSpec((tm * sub, LANES), lambda i, pe, pd: (i, 0))],
        out_specs=pl.BlockSpec(memory_space=pl.ANY),
        scratch_shapes=[pltpu.VMEM((EXPERT_ROWS * sub, LANES), U32), pltpu.SemaphoreType.DMA(())],
    )
    return pl.pallas_call(
        functools.partial(_dispatch_kernel, tm=tm, rows=EXPERT_ROWS, n_blocks=n_rows // EXPERT_ROWS, sub=sub),
        out_shape=jax.ShapeDtypeStruct((n_rows * sub, LANES), U32),
        grid_spec=grid_spec,
        compiler_params=_cparams(1, 32),
        name="dispatch",
    )(pad_end, padded, dest_flat, hp)


def _expert_kernel(blk_e_ref, next_e_ref, n_used_ref, x_ref, wg_hbm, wu_hbm, wd_hbm, y_ref,
                   wg_f32, wu_f32, wd_f32, wg_bf, wu_bf, wd_bf, sem_ref, slot_ref, *, layer, rows, width):
    i = pl.program_id(0)
    e = blk_e_ref[i]
    active = i < n_used_ref[0]
    first = jnp.logical_or(i == 0, e != blk_e_ref[jnp.maximum(i - 1, 0)])

    def fetch(expert, slot):
        return [pltpu.make_async_copy(w.at[layer, expert], buf.at[slot], sem_ref.at[slot])
                for w, buf in ((wg_hbm, wg_f32), (wu_hbm, wu_f32), (wd_hbm, wd_f32))]

    @pl.when(i == 0)
    def _():
        slot_ref[0] = 0
        for cp in fetch(e, 0):
            cp.start()

    @pl.when(jnp.logical_and(active, first))
    def _():
        slot = slot_ref[0]
        for cp in fetch(e, slot):
            cp.wait()
        nxt = next_e_ref[i]

        @pl.when(nxt >= 0)
        def _():
            for cp in fetch(nxt, 1 - slot):
                cp.start()

        wg_bf[...] = wg_f32[slot].astype(BF16)
        wu_bf[...] = wu_f32[slot].astype(BF16)
        wd_bf[...] = wd_f32[slot].astype(BF16)
        slot_ref[0] = 1 - slot

    @pl.when(active)
    def _():
        x = _unpack_pairs(_load_token_rows(x_ref, 0, rows, width))
        g = _dot(x, wg_bf[...])
        u = _dot(x, wu_bf[...])
        hb = (g * _sigmoid(g)) * u
        _store_token_rows(y_ref, _pack_pairs(_dot(hb.astype(BF16), wd_bf[...])))

    @pl.when(jnp.logical_not(active))
    def _():
        y_ref[...] = jnp.zeros_like(y_ref)


def _experts(xs, blk_e, next_e, n_used, w_gate, w_up, w_down, layer):
    _, E, D, Hx = w_gate.shape
    W = D // 2
    sub = _rows_per_token(W)
    M = EXPERT_ROWS
    nblk = xs.shape[0] // (M * sub)
    xrow = lambda i, be, ne, nu: (jnp.minimum(i, nu[0] - 1), 0)
    hbm = pl.BlockSpec(memory_space=pl.ANY)
    grid_spec = pltpu.PrefetchScalarGridSpec(
        num_scalar_prefetch=3,
        grid=(nblk,),
        in_specs=[pl.BlockSpec((M * sub, LANES), xrow), hbm, hbm, hbm],
        out_specs=pl.BlockSpec((M * sub, LANES), lambda i, be, ne, nu: (i, 0)),
        scratch_shapes=[pltpu.VMEM((2, D, Hx), F32), pltpu.VMEM((2, D, Hx), F32), pltpu.VMEM((2, Hx, D), F32),
                        pltpu.VMEM((D, Hx), BF16), pltpu.VMEM((D, Hx), BF16), pltpu.VMEM((Hx, D), BF16),
                        pltpu.SemaphoreType.DMA((2,)), pltpu.SMEM((1,), I32)],
    )
    return pl.pallas_call(
        functools.partial(_expert_kernel, layer=layer, rows=M, width=W),
        out_shape=jax.ShapeDtypeStruct(xs.shape, U32),
        grid_spec=grid_spec,
        compiler_params=_cparams(1, 52),
        name="experts",
    )(blk_e, next_e, n_used, xs, w_gate, w_up, w_down)


def _next_expert(blk_e, cnt):
    E = cnt.shape[0]
    idx = jnp.where(cnt > 0, jnp.arange(E, dtype=I32), E)
    later = jnp.concatenate([lax.cummin(idx, axis=0, reverse=True)[1:], jnp.full((1,), E, I32)])
    return jnp.where(later < E, later, -1)[blk_e]


def _unpack_pairs_f32(w):
    a = lax.bitcast_convert_type(w & jnp.uint32(0xFFFF0000), F32)
    b = lax.bitcast_convert_type(w << 16, F32)
    return jnp.concatenate([a, b], axis=1)


def _combine_kernel(dcur_ref, dnext_ref, x_ref, ysh_ref, w8_ref, g2_ref, fn_ref, ys_ref, o_ref, buf_ref, sem_ref,
                    *, tm, n_steps, width, final_norm):
    i = pl.program_id(0)
    slot = lax.rem(i, 2)
    sub = _rows_per_token(width)

    def issue(dref, s):
        def body(t, c):
            for k in range(TOP_K):
                _token_copy(ys_ref, dref[t * TOP_K + k], buf_ref.at[s], k * tm + t, sem_ref.at[s], sub).start(priority=k % 2)
            return c
        lax.fori_loop(0, tm, body, 0)

    @pl.when(i == 0)
    def _():
        issue(dcur_ref, 0)

    @pl.when(i + 1 < n_steps)
    def _():
        issue(dnext_ref, 1 - slot)

    def drain(t, c):
        for k in range(TOP_K):
            _token_copy(ys_ref, 0, buf_ref.at[slot], 0, sem_ref.at[slot], sub).wait()
        return c

    lax.fori_loop(0, tm, drain, 0)
    w8 = w8_ref[...]
    y = _unpack_pairs_f32(_load_token_rows(ysh_ref, 0, tm, width))
    gathered = buf_ref.at[slot]
    for k in range(TOP_K):
        y = y + w8[:, k:k + 1] * _unpack_pairs_f32(_load_token_rows(gathered, k * tm, tm, width))
    out = x_ref[...] + g2_ref[...] * y
    if final_norm:
        out = out * lax.rsqrt(jnp.mean(out * out, axis=-1, keepdims=True) + RMS_EPS) * fn_ref[...]
    o_ref[...] = out


def _combine(x, ysh, ys, dest_flat, w8, gate2, final_g, S, final_norm):
    T, D = x.shape
    W = D // 2
    sub = _rows_per_token(W)
    tm = TOKEN_TILE
    nb = S // tm
    n_steps = T // tm
    row = lambda i: (i, 0)
    return pl.pallas_call(
        functools.partial(_combine_kernel, tm=tm, n_steps=n_steps, width=W, final_norm=final_norm),
        out_shape=jax.ShapeDtypeStruct((T, D), F32),
        grid=(n_steps,),
        in_specs=[pl.BlockSpec((tm * TOP_K,), lambda i: (i,), memory_space=pltpu.SMEM),
                  pl.BlockSpec((tm * TOP_K,), lambda i: (jnp.minimum(i + 1, n_steps - 1),), memory_space=pltpu.SMEM),
                  pl.BlockSpec((tm, D), row), pl.BlockSpec((tm * sub, LANES), row),
                  pl.BlockSpec((tm, LANES), row),
                  pl.BlockSpec((None, 1, D), lambda i: (i // nb, 0, 0)),
                  pl.BlockSpec((1, D), lambda i: (0, 0)),
                  pl.BlockSpec(memory_space=pl.ANY)],
        out_specs=pl.BlockSpec((tm, D), row),
        scratch_shapes=[pltpu.VMEM((2, TOP_K * tm * sub, LANES), U32), pltpu.SemaphoreType.DMA((2,))],
        compiler_params=_cparams(1, 32),
        name="combine",
    )(dest_flat, dest_flat, x, ysh, w8, gate2, final_g, ys)


def _pad_lanes(v, offset=0):
    out = jnp.zeros((1, LANES), F32)
    return out.at[0, offset:offset + v.shape[0]].set(v.astype(F32))


def kernel(x, c, positions, attn_norm, w_ada, b_ada, w_in, fox_bias, gla_w_gate, gla_b_gate, gla_norm,
           w_branch, w_out, ffn_norm, w_router, router_bias, w_exp_gate, w_exp_up, w_exp_down,
           w_sh_gate, w_sh_up, w_sh_down, final_norm):
    B, S, D = x.shape
    L = w_ada.shape[0]
    T = B * S
    E = N_EXPERTS
    M = EXPERT_ROWS
    xf = x.reshape(T, D)

    c_pad = jnp.zeros((SUBLANES, D), F32).at[:B].set(c)
    mod = _ada_mod(c_pad, w_ada, b_ada)

    half = ROPE_DIM // 2
    inv_freq = jnp.power(ROPE_THETA, -jnp.arange(half, dtype=F32) * 2.0 / ROPE_DIM)
    freq_lanes = _pad_lanes(jnp.concatenate([inv_freq, inv_freq]))
    cos_tab, sin_tab = _rope_tables(positions.reshape(T, 1), freq_lanes)

    w_in2 = w_in.reshape(L * D, w_in.shape[2])
    w_branch_bf16 = w_branch.astype(BF16)
    n_blocks = (T * TOP_K + E * (M - 1) + M - 1) // M
    P = n_blocks * M

    for l in range(L):
        m6 = mod[l, :B].reshape(B, 6, 1, D)
        shift1, scale1, gate1, shift2, scale2, gate2 = (m6[:, n] for n in range(6))

        h1 = _norm_mod(xf, attn_norm[l].reshape(1, D), scale1, shift1, S)
        proj = _in_proj(h1, w_in2, l)
        misc = _misc_proj(h1, w_in2, l)

        o_a = _sb_attention(proj, B, S, COL_SB)
        qa, ka, va = _fox_prep(proj, misc, _pad_lanes(fox_bias[l], MISC_FOX_F), B, S)
        o_b = _flash_attention(qa, ka, va, B, S)
        qa, ka, va = _moba_prep(proj, cos_tab, sin_tab, B, S)
        o_c = _flash_attention(qa, ka, va, B, S)
        wg_pad = jnp.zeros((LANES, gla_w_gate.shape[2]), F32).at[MISC_GLR:MISC_GLR + GLA_GATE_RANK].set(gla_w_gate[l])
        o_d = _gla(proj, misc, wg_pad, gla_b_gate[l].reshape(1, -1), gla_norm[l].reshape(1, -1), B, S)

        merged = _merge((o_a, o_b, o_c, o_d), proj, w_branch_bf16, l)
        xf = _matmul(merged, w_out, tm=512, tn=1024, residual=(xf, gate1), S=S, layer=l)

        wr_pad = jnp.zeros((D, LANES), F32).at[:, :E].set(w_router[l])
        hp, logits = _norm_mod(xf, ffn_norm[l].reshape(1, D), scale2, shift2, S, w_router_pad=wr_pad)
        e8, w8, p8, counts = _route(logits, _pad_lanes(router_bias[l]))
        cnt = counts[0, :E].astype(I32)
        padded = (cnt + M - 1) // M * M
        pad_end = jnp.cumsum(padded)
        pad_start = pad_end - padded
        dest = _dest_rows(e8, p8, _pad_lanes(pad_start))[:, :TOP_K].reshape(T * TOP_K)
        blk_start = jnp.arange(n_blocks, dtype=I32) * M
        blk_e = jnp.minimum(jnp.sum((pad_end[None, :] <= blk_start[:, None]).astype(I32), axis=1), E - 1)
        n_used = (pad_end[E - 1:] // M).astype(I32)

        xs = _dispatch(hp, dest, pad_end.astype(I32), padded.astype(I32), P, _rows_per_token(D // 2))
        ys = _experts(xs, blk_e, _next_expert(blk_e, cnt), n_used, w_exp_gate, w_exp_up, w_exp_down, l)
        ysh = _experts(hp, jnp.zeros((T // M,), I32), jnp.full((T // M,), -1, I32), jnp.full((1,), T // M, I32),
                       w_sh_gate[:, None], w_sh_up[:, None], w_sh_down[:, None], l)
        last = l == L - 1
        xf = _combine(xf, ysh, ys, dest, w8, gate2, final_norm.reshape(1, D), S, final_norm=last)

    return xf.reshape(B, S, D)
```

```python
import functools

import jax
import jax.numpy as jnp
from jax import lax
from jax.experimental import pallas as pl
from jax.experimental.pallas import tpu as pltpu

F32 = jnp.float32
BF16 = jnp.bfloat16
I32 = jnp.int32
U32 = jnp.uint32

HEAD_DIM = 128
N_HEADS = 4
BRANCH_WIDTH = N_HEADS * HEAD_DIM
GLA_HEAD_K = 64
GLA_CHUNK = 64
GLA_GATE_RANK = 16
GLA_GATE_NORMALIZER = 16.0
MOBA_BLOCK = 256
MOBA_TOPK = 3
ROPE_THETA = 500000.0
ROPE_DIM = HEAD_DIM // 4
N_EXPERTS = 64
GROUP_SIZE = 8
TOPK_GROUPS = 4
TOP_K = 8
ROUTED_SCALE = 2.5
RMS_EPS = 1e-6

LANES = 128
SUBLANES = 8
MIB = 1024 * 1024
NEG_BIG = -1e30

ATT_BLOCK = 256
FLASH_Q_BLOCK = 512
EXPERT_ROWS = 256
TOKEN_TILE = 128

COL_SB, COL_FOX, COL_MOBA = 0, 12, 24
COL_GQ, COL_GK, COL_GV, COL_GR, COL_GATES = 36, 38, 40, 44, 48
N_MAIN_COLS = 112 * LANES
MISC_FOX_F, MISC_GLR = 0, 4


def _cparams(n_axes, vmem_mib):
    return pltpu.CompilerParams(dimension_semantics=("arbitrary",) * n_axes,
                                vmem_limit_bytes=vmem_mib * MIB)


def _softplus(z):
    return jnp.maximum(z, 0.0) + jnp.log(1.0 + jnp.exp(-jnp.abs(z)))


def _log_sigmoid(z):
    return -_softplus(-z)


def _sigmoid(z):
    return 1.0 / (1.0 + jnp.exp(-z))


def _dot(a, b, precision=None):
    return jnp.dot(a, b, preferred_element_type=F32, precision=precision)


def _dot_nt(a, b, precision=None):
    return lax.dot_general(a, b, (((1,), (1,)), ((), ())), preferred_element_type=F32,
                           precision=precision)


def _dot_tn(a, b, precision=None):
    return lax.dot_general(a, b, (((0,), (0,)), ((), ())), preferred_element_type=F32,
                           precision=precision)


HIGHEST = lax.Precision.HIGHEST


def _split3(x):
    hi = x.astype(BF16)
    r1 = x - hi.astype(F32)
    mid = r1.astype(BF16)
    lo = (r1 - mid.astype(F32)).astype(BF16)
    return hi, mid, lo


def _ada_kernel(c_ref, w_ref, b_ref, o_ref):
    c = c_ref[...]
    c_act = (c * _sigmoid(c)).astype(BF16)
    o_ref[...] = _dot(c_act, w_ref[...].astype(BF16)) + b_ref[...]


def _ada_mod(c_pad, w_ada, b_ada):
    L, D, N = w_ada.shape
    tn = 1536
    return pl.pallas_call(
        _ada_kernel,
        out_shape=jax.ShapeDtypeStruct((L, SUBLANES, N), F32),
        grid=(L, N // tn),
        in_specs=[pl.BlockSpec((SUBLANES, D), lambda l, j: (0, 0)),
                  pl.BlockSpec((None, D, tn), lambda l, j: (l, 0, j)),
                  pl.BlockSpec((None, 1, tn), lambda l, j: (l, 0, j))],
        out_specs=pl.BlockSpec((None, SUBLANES, tn), lambda l, j: (l, 0, j)),
        compiler_params=_cparams(2, 40),
        name="ada_mod",
    )(c_pad, w_ada, b_ada.reshape(L, 1, N))


def _normed(x, g, scale, shift):
    y = x * lax.rsqrt(jnp.mean(x * x, axis=-1, keepdims=True) + RMS_EPS)
    return (y * g) * (1.0 + scale) + shift


def _norm_mod_kernel(x_ref, g_ref, sc_ref, sh_ref, h_ref):
    h_ref[...] = _normed(x_ref[...], g_ref[...], sc_ref[...], sh_ref[...]).astype(BF16)


def _pack_pairs(h):
    half = h.shape[1] // 2
    hi = lax.bitcast_convert_type(h[:, :half].astype(BF16).astype(F32), U32)
    lo = lax.bitcast_convert_type(h[:, half:].astype(BF16).astype(F32), U32)
    return hi | (lo >> 16)


def _unpack_pairs(w):
    a = lax.bitcast_convert_type(w & jnp.uint32(0xFFFF0000), F32).astype(BF16)
    b = lax.bitcast_convert_type(w << 16, F32).astype(BF16)
    return jnp.concatenate([a, b], axis=1)


def _rows_per_token(width):
    return width // LANES


def _store_token_rows(ref, val):
    n, width = val.shape
    sub = _rows_per_token(width)
    for s in range(sub):
        ref[pl.ds(s, n, stride=sub), :] = val[:, s * LANES:(s + 1) * LANES]


def _load_token_rows(ref, first_token, n, width):
    sub = _rows_per_token(width)
    return jnp.concatenate([ref[pl.ds(first_token * sub + s, n, stride=sub), :] for s in range(sub)], axis=1)


def _norm_route_kernel(x_ref, g_ref, sc_ref, sh_ref, wr_ref, hp_ref, lg_ref):
    h = _normed(x_ref[...], g_ref[...], sc_ref[...], sh_ref[...])
    _store_token_rows(hp_ref, _pack_pairs(h))
    lg_ref[...] = _dot(h, wr_ref[...], precision=HIGHEST)


def _norm_mod(x, g, scale, shift, S, w_router_pad=None):
    T, D = x.shape
    tm = 512
    nb = S // tm
    row = lambda i: (i, 0)
    per_batch = pl.BlockSpec((None, 1, D), lambda i: (i // nb, 0, 0))
    in_specs = [pl.BlockSpec((tm, D), row), pl.BlockSpec((1, D), lambda i: (0, 0)), per_batch, per_batch]
    if w_router_pad is None:
        return pl.pallas_call(
            _norm_mod_kernel, out_shape=jax.ShapeDtypeStruct((T, D), BF16), grid=(T // tm,),
            in_specs=in_specs, out_specs=pl.BlockSpec((tm, D), row),
            compiler_params=_cparams(1, 32), name="norm_mod",
        )(x, g, scale, shift)
    return pl.pallas_call(
        _norm_route_kernel,
        out_shape=(jax.ShapeDtypeStruct((T * (D // 2) // LANES, LANES), U32), jax.ShapeDtypeStruct((T, LANES), F32)),
        grid=(T // tm,),
        in_specs=in_specs + [pl.BlockSpec((D, LANES), lambda i: (0, 0))],
        out_specs=(pl.BlockSpec((tm * (D // 2) // LANES, LANES), row), pl.BlockSpec((tm, LANES), row)),
        compiler_params=_cparams(1, 32), name="norm_route",
    )(x, g, scale, shift, w_router_pad)


def _mm_kernel(*refs, cast_w, residual):
    if residual:
        a_ref, w_ref, x_ref, g_ref, o_ref = refs[:5]
        scratch = refs[5:]
    else:
        a_ref, w_ref, o_ref = refs[:3]
        scratch = refs[3:]
    if cast_w:
        wbf_ref, = scratch

        @pl.when(pl.program_id(1) == 0)
        def _():
            wbf_ref[...] = w_ref[...].astype(BF16)

        w = wbf_ref[...]
    else:
        w = w_ref[...]
    acc = _dot(a_ref[...], w)
    if residual:
        acc = x_ref[...] + g_ref[...] * acc
    o_ref[...] = acc.astype(o_ref.dtype)


def _matmul(a, w, tm, tn, out_dtype=F32, residual=None, S=None, vmem_mib=48, layer=None):
    M, K = a.shape
    N = w.shape[-1]
    cast_w = w.dtype != BF16
    if layer is None:
        w_spec = pl.BlockSpec((K, tn), lambda j, i: (0, j))
    else:
        w_spec = pl.BlockSpec((None, K, tn), lambda j, i: (layer, 0, j))
    in_specs = [pl.BlockSpec((tm, K), lambda j, i: (i, 0)), w_spec]
    args = [a, w]
    if residual is not None:
        x, gate = residual
        nb = S // tm
        in_specs += [pl.BlockSpec((tm, tn), lambda j, i: (i, j)),
                     pl.BlockSpec((None, 1, tn), lambda j, i: (i // nb, 0, j))]
        args += [x, gate]
    return pl.pallas_call(
        functools.partial(_mm_kernel, cast_w=cast_w, residual=residual is not None),
        out_shape=jax.ShapeDtypeStruct((M, N), out_dtype),
        grid=(N // tn, M // tm),
        in_specs=in_specs,
        out_specs=pl.BlockSpec((tm, tn), lambda j, i: (i, j)),
        scratch_shapes=[pltpu.VMEM((K, tn), BF16)] if cast_w else [],
        compiler_params=_cparams(2, vmem_mib),
        name="matmul_res" if residual is not None else "matmul",
    )(*args)


def _nt_matmul_kernel(a_ref, w_ref, o_ref):
    o_ref[...] = _dot_nt(a_ref[...], w_ref[...]).astype(o_ref.dtype)


def _nt_matmul(a, wt, layer, tm, tn, out_dtype):
    M, K = a.shape
    N = wt.shape[1]
    return pl.pallas_call(
        _nt_matmul_kernel,
        out_shape=jax.ShapeDtypeStruct((M, N), out_dtype),
        grid=(N // tn, M // tm),
        in_specs=[pl.BlockSpec((tm, K), lambda j, i: (i, 0)),
                  pl.BlockSpec((None, tn, K), lambda j, i: (layer, j, 0))],
        out_specs=pl.BlockSpec((tm, tn), lambda j, i: (i, j)),
        compiler_params=_cparams(2, 48),
        name="in_proj",
    )(a, wt)


def _input_weights(w_in):
    c_fox_f = COL_MOBA * LANES
    c_glr = COL_GATES * LANES + N_HEADS
    wt = jnp.swapaxes(w_in, 1, 2)
    main = jnp.concatenate([wt[:, :c_fox_f], wt[:, c_fox_f + N_HEADS:c_glr],
                            wt[:, c_glr + GLA_GATE_RANK:]], axis=1).astype(BF16)
    pad = jnp.zeros((wt.shape[0], LANES - N_HEADS - GLA_GATE_RANK, wt.shape[2]), wt.dtype)
    misc = jnp.concatenate([wt[:, c_fox_f:c_fox_f + N_HEADS], wt[:, c_glr:c_glr + GLA_GATE_RANK], pad],
                           axis=1).astype(BF16)
    return main, misc


SB_LOG_WEIGHT_FLOOR = -110.0


def _sb_kernel(q_ref, k_ref, v_ref, o_ref, *, blk, scale):
    i = pl.program_id(1)
    row = lax.broadcasted_iota(I32, (blk, blk), 0)
    col = lax.broadcasted_iota(I32, (blk, blk), 1)
    later = (row > col).astype(BF16)

    def cond(state):
        jj, alive, _ = state
        return jnp.logical_and(jj <= i, alive)

    def body(state):
        jj, _, heads = state
        j = i - jj
        start = pl.multiple_of(j * blk, blk)
        past = jnp.logical_or(j < i, col < row)
        new, top = [], None
        for h in range(N_HEADS):
            c, acc = heads[h]
            hs = slice(h * HEAD_DIM, (h + 1) * HEAD_DIM)
            q = (q_ref[:, hs].astype(F32) * scale).astype(BF16)
            k = k_ref[pl.ds(start, blk), hs]
            v = v_ref[pl.ds(start, blk), hs]
            z = _dot_nt(q, k)
            ls = jnp.where(past, -_softplus(z), 0.0)
            hi = ls.astype(BF16)
            lo = (ls - hi.astype(F32)).astype(BF16)
            between = _dot(hi, later) + _dot(lo, later)
            w = jnp.where(past, jnp.exp(z + ls + between + c), 0.0)
            acc = acc + _dot(w.astype(BF16), v)
            c = c + jnp.sum(ls, axis=1, keepdims=True)
            new.append((c, acc))
            top = c if top is None else jnp.maximum(top, c)
        return jj + 1, jnp.max(top) > SB_LOG_WEIGHT_FLOOR, tuple(new)

    init = tuple((jnp.zeros((blk, 1), F32), jnp.zeros((blk, HEAD_DIM), F32)) for _ in range(N_HEADS))
    _, _, heads = lax.while_loop(cond, body, (jnp.int32(0), jnp.bool_(True), init))
    for h in range(N_HEADS):
        o_ref[:, h * HEAD_DIM:(h + 1) * HEAD_DIM] = heads[h][1]


def _sb_attention(proj, B, S, col0):
    T = proj.shape[0]
    blk = ATT_BLOCK
    nq = S // blk
    W = BRANCH_WIDTH
    cb = col0 * LANES // W
    return pl.pallas_call(
        functools.partial(_sb_kernel, blk=blk, scale=HEAD_DIM ** -0.5),
        out_shape=jax.ShapeDtypeStruct((T, W), F32),
        grid=(B, nq),
        in_specs=[pl.BlockSpec((blk, W), lambda b, i: (b * nq + i, cb)),
                  pl.BlockSpec((S, W), lambda b, i: (b, cb + 1)),
                  pl.BlockSpec((S, W), lambda b, i: (b, cb + 2))],
        out_specs=pl.BlockSpec((blk, W), lambda b, i: (b * nq + i, 0)),
        compiler_params=_cparams(2, 44),
        name="sb_attention",
    )(proj, proj, proj)


def _flash_kernel(qa_ref, ka_ref, v_ref, o_ref, *, bq, bk):
    i = pl.program_id(1)
    A = 2 * HEAD_DIM
    ratio = bq // bk
    row = lax.broadcasted_iota(I32, (bq, bk), 0)
    col = lax.broadcasted_iota(I32, (bq, bk), 1)

    def step(j, carry, key_offset=None):
        start = pl.multiple_of(j * bk, bk)
        new = []
        for h in range(N_HEADS):
            m, l, acc = carry[h]
            s = _dot_nt(qa_ref[:, h * A:(h + 1) * A], ka_ref[pl.ds(start, bk), h * A:(h + 1) * A])
            if key_offset is not None:
                s = jnp.where(col + key_offset <= row, s, NEG_BIG)
            m_new = jnp.maximum(m, jnp.max(s, axis=1, keepdims=True))
            alpha = jnp.exp(m - m_new)
            p = jnp.exp(s - m_new)
            l = alpha * l + jnp.sum(p, axis=1, keepdims=True)
            acc = alpha * acc + _dot(p.astype(BF16), v_ref[pl.ds(start, bk), h * HEAD_DIM:(h + 1) * HEAD_DIM])
            new.append((m_new, l, acc))
        return tuple(new)

    init = tuple((jnp.full((bq, 1), NEG_BIG, F32), jnp.zeros((bq, 1), F32),
                  jnp.zeros((bq, HEAD_DIM), F32)) for _ in range(N_HEADS))
    heads = lax.fori_loop(0, i * ratio, step, init)
    for d in range(ratio):
        heads = step(i * ratio + d, heads, key_offset=d * bk)
    for h in range(N_HEADS):
        _, l, acc = heads[h]
        o_ref[:, h * HEAD_DIM:(h + 1) * HEAD_DIM] = acc / l


def _flash_attention(qa, ka, va, B, S):
    T = qa.shape[0]
    bq, bk = FLASH_Q_BLOCK, ATT_BLOCK
    nq = S // bq
    W = BRANCH_WIDTH
    return pl.pallas_call(
        functools.partial(_flash_kernel, bq=bq, bk=bk),
        out_shape=jax.ShapeDtypeStruct((T, W), F32),
        grid=(B, nq),
        in_specs=[pl.BlockSpec((bq, 2 * W), lambda b, i: (b * nq + i, 0)),
                  pl.BlockSpec((S, 2 * W), lambda b, i: (b, 0)),
                  pl.BlockSpec((S, W), lambda b, i: (b, 0))],
        out_specs=pl.BlockSpec((bq, W), lambda b, i: (b * nq + i, 0)),
        compiler_params=_cparams(2, 44),
        name="flash_attention",
    )(qa, ka, va)


def _fox_prep_kernel(q_ref, k_ref, v_ref, misc_ref, bias_ref, qa_ref, ka_ref, va_ref, carry_ref, *, tb, scale):
    va_ref[...] = v_ref[...].astype(BF16)
    @pl.when(pl.program_id(1) == 0)
    def _():
        carry_ref[...] = jnp.zeros_like(carry_ref)

    lane = lax.broadcasted_iota(I32, (tb, LANES), 1)
    lf = jnp.where(lane < N_HEADS, _log_sigmoid(misc_ref[...] + bias_ref[...]), 0.0)
    row = lax.broadcasted_iota(I32, (tb, tb), 0)
    col = lax.broadcasted_iota(I32, (tb, tb), 1)
    incl = (col <= row).astype(F32)
    F = _dot(incl, lf, precision=HIGHEST) + carry_ref[0:1, :]
    carry_ref[0:1, :] = F[tb - 1:tb, :]
    for h in range(N_HEADS):
        Fh = jnp.broadcast_to(F[:, h:h + 1], (tb, LANES))
        hi, mid, lo = (p.astype(F32) for p in _split3(Fh))
        ones = jnp.ones((tb, LANES), F32)
        zeros = jnp.zeros((tb, LANES), F32)
        eq = jnp.where(lane == 0, hi, jnp.where(lane == 1, mid, jnp.where(lane == 2, lo,
             jnp.where(lane < 6, ones, zeros))))
        ek = jnp.where(lane < 3, ones, jnp.where(lane == 3, -hi, jnp.where(lane == 4, -mid,
             jnp.where(lane == 5, -lo, zeros))))
        hs = slice(h * HEAD_DIM, (h + 1) * HEAD_DIM)
        qa_ref[:, 2 * h * HEAD_DIM:(2 * h + 1) * HEAD_DIM] = (q_ref[:, hs].astype(F32) * scale).astype(BF16)
        qa_ref[:, (2 * h + 1) * HEAD_DIM:(2 * h + 2) * HEAD_DIM] = eq.astype(BF16)
        ka_ref[:, 2 * h * HEAD_DIM:(2 * h + 1) * HEAD_DIM] = k_ref[:, hs].astype(BF16)
        ka_ref[:, (2 * h + 1) * HEAD_DIM:(2 * h + 2) * HEAD_DIM] = ek.astype(BF16)


def _fox_prep(proj, misc, fox_bias_pad, B, S):
    T = proj.shape[0]
    tb = 512
    nb = S // tb
    qblk = COL_FOX * LANES // BRANCH_WIDTH
    out = jax.ShapeDtypeStruct((T, 2 * BRANCH_WIDTH), BF16)
    return pl.pallas_call(
        functools.partial(_fox_prep_kernel, tb=tb, scale=HEAD_DIM ** -0.5),
        out_shape=(out, out, jax.ShapeDtypeStruct((T, BRANCH_WIDTH), BF16)),
        grid=(B, nb),
        in_specs=[pl.BlockSpec((tb, BRANCH_WIDTH), lambda b, i: (b * nb + i, qblk)),
                  pl.BlockSpec((tb, BRANCH_WIDTH), lambda b, i: (b * nb + i, qblk + 1)),
                  pl.BlockSpec((tb, BRANCH_WIDTH), lambda b, i: (b * nb + i, qblk + 2)),
                  pl.BlockSpec((tb, LANES), lambda b, i: (b * nb + i, 0)),
                  pl.BlockSpec((1, LANES), lambda b, i: (0, 0))],
        out_specs=(pl.BlockSpec((tb, 2 * BRANCH_WIDTH), lambda b, i: (b * nb + i, 0)),
                   pl.BlockSpec((tb, 2 * BRANCH_WIDTH), lambda b, i: (b * nb + i, 0)),
                   pl.BlockSpec((tb, BRANCH_WIDTH), lambda b, i: (b * nb + i, 0))),
        scratch_shapes=[pltpu.VMEM((SUBLANES, LANES), F32)],
        compiler_params=_cparams(2, 32),
        name="fox_prep",
    )(proj, proj, proj, misc, fox_bias_pad)


def _rope(x, cos, sin, lane):
    half = ROPE_DIM // 2
    up = pltpu.roll(x, half, 1)
    down = pltpu.roll(x, LANES - half, 1)
    rot = jnp.where(lane < half, -down * sin, jnp.where(lane < ROPE_DIM, up * sin, 0.0))
    return x * jnp.where(lane < ROPE_DIM, cos, 1.0) + rot


def _rope_table_kernel(pos_ref, freq_ref, cos_ref, sin_ref):
    ang = pos_ref[...].astype(F32) * freq_ref[...]
    cos_ref[...] = jnp.cos(ang)
    sin_ref[...] = jnp.sin(ang)


def _rope_tables(pos_col, freq_lanes):
    T = pos_col.shape[0]
    tb = 1024
    row = lambda i: (i, 0)
    out = jax.ShapeDtypeStruct((T, LANES), F32)
    return pl.pallas_call(
        _rope_table_kernel, out_shape=(out, out), grid=(T // tb,),
        in_specs=[pl.BlockSpec((tb, 1), row), pl.BlockSpec((1, LANES), lambda i: (0, 0))],
        out_specs=(pl.BlockSpec((tb, LANES), row), pl.BlockSpec((tb, LANES), row)),
        compiler_params=_cparams(1, 32), name="rope_tables",
    )(pos_col, freq_lanes)


def _moba_k_kernel(k_ref, v_ref, cos_ref, sin_ref, ka_ref, va_ref, kmean_ref, *, tb):
    i = pl.program_id(1)
    va_ref[...] = v_ref[...].astype(BF16)

    @pl.when(i == 0)
    def _():
        kmean_ref[...] = jnp.zeros_like(kmean_ref)

    lane = lax.broadcasted_iota(I32, (tb, LANES), 1)
    cos, sin = cos_ref[...], sin_ref[...]
    onehot = jnp.where(lane == i, 1.0, 0.0).astype(BF16)
    this_row = lax.broadcasted_iota(I32, (LANES, HEAD_DIM), 0) == i
    for h in range(N_HEADS):
        hs = slice(h * HEAD_DIM, (h + 1) * HEAD_DIM)
        kr = _rope(k_ref[:, hs].astype(F32), cos, sin, lane)
        mean = jnp.sum(kr, axis=0, keepdims=True) * (1.0 / tb)
        kmean_ref[:, hs] = jnp.where(this_row, mean, kmean_ref[:, hs])
        ka_ref[:, 2 * h * HEAD_DIM:(2 * h + 1) * HEAD_DIM] = kr.astype(BF16)
        ka_ref[:, (2 * h + 1) * HEAD_DIM:(2 * h + 2) * HEAD_DIM] = onehot


def _moba_q_kernel(q_ref, cos_ref, sin_ref, kmean_ref, qa_ref, *, tb, scale):
    i = pl.program_id(1)
    lane = lax.broadcasted_iota(I32, (tb, LANES), 1)
    cos, sin = cos_ref[...], sin_ref[...]
    neg_inf = jnp.float32(-jnp.inf)
    for h in range(N_HEADS):
        hs = slice(h * HEAD_DIM, (h + 1) * HEAD_DIM)
        qr = _rope(q_ref[:, hs].astype(F32), cos, sin, lane)
        gate = _dot_nt(qr, kmean_ref[:, hs], precision=HIGHEST)
        cur = jnp.where(lane < i, gate, neg_inf)
        chosen = lane == i
        for _ in range(MOBA_TOPK):
            m = jnp.max(cur, axis=1, keepdims=True)
            first = jnp.min(jnp.where(jnp.logical_and(cur == m, m > neg_inf), lane, LANES),
                            axis=1, keepdims=True)
            pick = lane == first
            chosen = jnp.logical_or(chosen, pick)
            cur = jnp.where(pick, neg_inf, cur)
        bias = jnp.where(jnp.logical_or(chosen, lane >= LANES // 2), 0.0, NEG_BIG)
        qa_ref[:, 2 * h * HEAD_DIM:(2 * h + 1) * HEAD_DIM] = (qr * scale).astype(BF16)
        qa_ref[:, (2 * h + 1) * HEAD_DIM:(2 * h + 2) * HEAD_DIM] = bias.astype(BF16)


def _moba_prep(proj, cos_tab, sin_tab, B, S):
    T = proj.shape[0]
    tb = MOBA_BLOCK
    nb = S // tb
    qblk = COL_MOBA * LANES // BRANCH_WIDTH
    aug = jax.ShapeDtypeStruct((T, 2 * BRANCH_WIDTH), BF16)
    row_spec = lambda c: pl.BlockSpec((tb, BRANCH_WIDTH), lambda b, i: (b * nb + i, c))
    trig_spec = pl.BlockSpec((tb, LANES), lambda b, i: (b * nb + i, 0))
    aug_spec = pl.BlockSpec((tb, 2 * BRANCH_WIDTH), lambda b, i: (b * nb + i, 0))
    kmean_spec = pl.BlockSpec((None, LANES, BRANCH_WIDTH), lambda b, i: (b, 0, 0))
    ka, va, kmean = pl.pallas_call(
        functools.partial(_moba_k_kernel, tb=tb),
        out_shape=(aug, jax.ShapeDtypeStruct((T, BRANCH_WIDTH), BF16),
                   jax.ShapeDtypeStruct((B, LANES, BRANCH_WIDTH), F32)),
        grid=(B, nb),
        in_specs=[row_spec(qblk + 1), row_spec(qblk + 2), trig_spec, trig_spec],
        out_specs=(aug_spec, pl.BlockSpec((tb, BRANCH_WIDTH), lambda b, i: (b * nb + i, 0)), kmean_spec),
        compiler_params=_cparams(2, 32),
        name="moba_k_prep",
    )(proj, proj, cos_tab, sin_tab)
    qa = pl.pallas_call(
        functools.partial(_moba_q_kernel, tb=tb, scale=HEAD_DIM ** -0.5),
        out_shape=aug,
        grid=(B, nb),
        in_specs=[row_spec(qblk), trig_spec, trig_spec, kmean_spec],
        out_specs=aug_spec,
        compiler_params=_cparams(2, 32),
        name="moba_q_prep",
    )(proj, cos_tab, sin_tab, kmean)
    return qa, ka, va


def _gla_kernel(q_ref, k_ref, v_ref, gr_ref, misc_ref, wg_ref, bg_ref, gn_ref, o_ref, state_ref, *, tb):
    C = GLA_CHUNK
    @pl.when(pl.program_id(1) == 0)
    def _():
        state_ref[...] = jnp.zeros_like(state_ref)

    la_all = _log_sigmoid(_dot(misc_ref[...], wg_ref[...], precision=HIGHEST) + bg_ref[...]) \
        * (1.0 / GLA_GATE_NORMALIZER)
    rowc = lax.broadcasted_iota(I32, (C, C), 0)
    colc = lax.broadcasted_iota(I32, (C, C), 1)
    incl = (colc <= rowc).astype(F32)
    causal = colc <= rowc
    lane = lax.broadcasted_iota(I32, (C, LANES), 1)
    row2 = lax.broadcasted_iota(I32, (2 * GLA_HEAD_K, LANES), 0)
    ones_cv = jnp.ones((C, LANES), F32)
    qscale = GLA_HEAD_K ** -0.5
    gn = gn_ref[...]
    for c in range(tb // C):
        rs = slice(c * C, (c + 1) * C)
        for p in range(2):
            ps = slice(p * LANES, (p + 1) * LANES)
            la = la_all[rs, ps]
            b = _dot(incl, la, precision=HIGHEST)
            b_last = b[C - 1:C, :]
            eb = jnp.exp(b)
            q_dec = q_ref[rs, ps].astype(F32) * qscale * eb
            k = k_ref[rs, ps].astype(F32)
            k_inv = (k * jnp.exp(-b)).astype(BF16)
            k_end = (k * jnp.exp(b_last - b)).astype(BF16)
            decay = jnp.exp(_dot_tn(la, ones_cv, precision=HIGHEST))
            st = state_ref[p * LANES:(p + 1) * LANES, :]
            new_rows = []
            for e in range(2):
                h = 2 * p + e
                own = jnp.logical_and(lane >= e * GLA_HEAD_K, lane < (e + 1) * GLA_HEAD_K)
                qh = jnp.where(own, q_dec, 0.0).astype(BF16)
                v = v_ref[rs, h * HEAD_DIM:(h + 1) * HEAD_DIM].astype(BF16)
                attn = jnp.where(causal, _dot_nt(qh, k_inv), 0.0)
                st_h = jnp.where(jnp.logical_and(row2 >= e * GLA_HEAD_K, row2 < (e + 1) * GLA_HEAD_K), st, 0.0)
                o = _dot(attn.astype(BF16), v) + _dot(qh, st_h.astype(BF16))
                y = o * lax.rsqrt(jnp.mean(o * o, axis=-1, keepdims=True) + RMS_EPS) * gn
                g = gr_ref[rs, h * HEAD_DIM:(h + 1) * HEAD_DIM].astype(F32)
                o_ref[rs, h * HEAD_DIM:(h + 1) * HEAD_DIM] = y * (g * _sigmoid(g))
                new_rows.append(_dot_tn(k_end, v))
            kv = jnp.where(row2 < GLA_HEAD_K, new_rows[0], new_rows[1])
            state_ref[p * LANES:(p + 1) * LANES, :] = decay * st + kv


def _gla(proj, misc, w_gate_pad, b_gate, gla_norm, B, S):
    T = proj.shape[0]
    tb = 512
    nb = S // tb
    kd = 2 * LANES
    rows = lambda w, c: pl.BlockSpec((tb, w), lambda b, i: (b * nb + i, c))
    const = lambda shape: pl.BlockSpec(shape, lambda b, i: (0, 0))
    return pl.pallas_call(
        functools.partial(_gla_kernel, tb=tb),
        out_shape=jax.ShapeDtypeStruct((T, BRANCH_WIDTH), F32),
        grid=(B, nb),
        in_specs=[rows(kd, COL_GQ * LANES // kd), rows(kd, COL_GK * LANES // kd),
                  rows(BRANCH_WIDTH, COL_GV * LANES // BRANCH_WIDTH),
                  rows(BRANCH_WIDTH, COL_GR * LANES // BRANCH_WIDTH),
                  rows(LANES, 0), const((LANES, kd)), const((1, kd)), const((1, HEAD_DIM))],
        out_specs=rows(BRANCH_WIDTH, 0),
        scratch_shapes=[pltpu.VMEM((kd, HEAD_DIM), F32)],
        compiler_params=_cparams(2, 32),
        name="gla",
    )(proj, proj, proj, proj, misc, w_gate_pad, b_gate, gla_norm)


def _merge_kernel(oa_ref, ob_ref, oc_ref, od_ref, g0_ref, g1_ref, g2_ref, g3_ref, w_ref, o_ref):
    acc = None
    for n, (b_ref, g_ref) in enumerate(((oa_ref, g0_ref), (ob_ref, g1_ref), (oc_ref, g2_ref), (od_ref, g3_ref))):
        term = _sigmoid(g_ref[...].astype(F32)) * _dot(b_ref[...].astype(BF16), w_ref[n])
        acc = term if acc is None else acc + term
    o_ref[...] = acc.astype(BF16)


def _merge(branches, proj, w_branch_bf16, layer):
    T = proj.shape[0]
    _, nbr, W, D = w_branch_bf16.shape
    tm = 256
    g0 = COL_GATES * LANES // D
    br_spec = pl.BlockSpec((tm, W), lambda i: (i, 0))
    gate_spec = lambda n: pl.BlockSpec((tm, D), lambda i: (i, g0 + n))
    return pl.pallas_call(
        _merge_kernel,
        out_shape=jax.ShapeDtypeStruct((T, D), BF16),
        grid=(T // tm,),
        in_specs=[br_spec] * 4 + [gate_spec(n) for n in range(4)]
                 + [pl.BlockSpec((None, nbr, W, D), lambda i: (layer, 0, 0, 0))],
        out_specs=pl.BlockSpec((tm, D), lambda i: (i, 0)),
        compiler_params=_cparams(1, 44),
        name="branch_merge",
    )(*branches, proj, proj, proj, proj, w_branch_bf16)


def _butterfly(x, lane, op):
    for s in (1, 2, 4):
        up = pltpu.roll(x, s, 1)
        down = pltpu.roll(x, LANES - s, 1)
        x = op(x, jnp.where((lane & s) != 0, up, down))
    return x


def _route_kernel(lg_ref, bias_ref, e8_ref, w8_ref, p8_ref, cnt_ref, carry_ref, *, tm):
    @pl.when(pl.program_id(0) == 0)
    def _():
        carry_ref[...] = jnp.zeros_like(carry_ref)

    neg_inf = jnp.float32(-jnp.inf)
    lane = lax.broadcasted_iota(I32, (tm, LANES), 1)
    valid = lane < N_EXPERTS
    scores = _sigmoid(lg_ref[...])
    biased = jnp.where(valid, scores + bias_ref[...], neg_inf)
    g1 = _butterfly(biased, lane, jnp.maximum)
    first = _butterfly(jnp.where(biased == g1, lane, LANES), lane, jnp.minimum)
    g2 = _butterfly(jnp.where(lane == first, neg_inf, biased), lane, jnp.maximum)
    gs = g1 + g2
    gs = jnp.where(valid, gs, pltpu.roll(gs, N_EXPERTS, 1))
    gidx = lane >> 3
    beaten = jnp.zeros((tm, LANES), I32)
    for r in range(1, N_EXPERTS // GROUP_SIZE):
        other = pltpu.roll(gs, GROUP_SIZE * r, 1)
        og = (gidx - r) & (N_EXPERTS // GROUP_SIZE - 1)
        wins = jnp.logical_or(other > gs, jnp.logical_and(other == gs, og < gidx))
        beaten = beaten + wins.astype(I32)
    cur = jnp.where(jnp.logical_and(beaten < TOPK_GROUPS, valid), biased, neg_inf)
    sel = jnp.zeros((tm, LANES), jnp.bool_)
    for _ in range(TOP_K):
        m = jnp.max(cur, axis=1, keepdims=True)
        pick = lane == jnp.min(jnp.where(cur == m, lane, LANES), axis=1, keepdims=True)
        sel = jnp.logical_or(sel, pick)
        cur = jnp.where(pick, neg_inf, cur)
    wsel = jnp.where(sel, scores, 0.0)
    wd = wsel / jnp.sum(wsel, axis=1, keepdims=True) * ROUTED_SCALE
    selb = jnp.where(sel, 1.0, 0.0).astype(BF16)
    row = lax.broadcasted_iota(I32, (tm, tm), 0)
    col = lax.broadcasted_iota(I32, (tm, tm), 1)
    pos = _dot((col < row).astype(BF16), selb) + carry_ref[0:1, :]
    total = carry_ref[0:1, :] + jnp.sum(selb.astype(F32), axis=0, keepdims=True)
    carry_ref[0:1, :] = total
    cnt_ref[...] = jnp.broadcast_to(total, cnt_ref.shape)
    r2 = lax.broadcasted_iota(I32, (LANES, LANES), 0)
    c2 = lax.broadcasted_iota(I32, (LANES, LANES), 1)
    slot = _dot(selb, (r2 < c2).astype(BF16))
    lane_f = lane.astype(F32)
    e8 = jnp.zeros((tm, LANES), F32)
    w8 = jnp.zeros((tm, LANES), F32)
    p8 = jnp.zeros((tm, LANES), F32)
    for k in range(TOP_K):
        mk = jnp.logical_and(sel, slot == k)
        put = lane == k
        e8 = jnp.where(put, jnp.sum(jnp.where(mk, lane_f, 0.0), axis=1, keepdims=True), e8)
        w8 = jnp.where(put, jnp.sum(jnp.where(mk, wd, 0.0), axis=1, keepdims=True), w8)
        p8 = jnp.where(put, jnp.sum(jnp.where(mk, pos, 0.0), axis=1, keepdims=True), p8)
    e8_ref[...] = e8.astype(I32)
    w8_ref[...] = w8
    p8_ref[...] = p8.astype(I32)


def _route(logits, router_bias_pad):
    T = logits.shape[0]
    tm = 512
    row = pl.BlockSpec((tm, LANES), lambda i: (i, 0))
    return pl.pallas_call(
        functools.partial(_route_kernel, tm=tm),
        out_shape=(jax.ShapeDtypeStruct((T, LANES), I32), jax.ShapeDtypeStruct((T, LANES), F32),
                   jax.ShapeDtypeStruct((T, LANES), I32), jax.ShapeDtypeStruct((SUBLANES, LANES), F32)),
        grid=(T // tm,),
        in_specs=[row, pl.BlockSpec((1, LANES), lambda i: (0, 0))],
        out_specs=(row, row, row, pl.BlockSpec((SUBLANES, LANES), lambda i: (0, 0))),
        scratch_shapes=[pltpu.VMEM((SUBLANES, LANES), F32)],
        compiler_params=_cparams(1, 32),
        name="route",
    )(logits, router_bias_pad)


def _dest_kernel(e8_ref, p8_ref, start_ref, d_ref):
    e8 = e8_ref[...]
    lane = lax.broadcasted_iota(I32, e8.shape, 1)
    starts = start_ref[...]
    dest = p8_ref[...]
    for k in range(TOP_K):
        hit = lane == e8[:, k:k + 1]
        base = jnp.sum(jnp.where(hit, starts, 0.0), axis=1, keepdims=True).astype(I32)
        dest = jnp.where(lane == k, dest + base, dest)
    d_ref[...] = dest


def _dest_rows(e8, p8, pad_start_lanes):
    T = e8.shape[0]
    tm = 1024
    row = pl.BlockSpec((tm, LANES), lambda i: (i, 0))
    return pl.pallas_call(
        _dest_kernel, out_shape=jax.ShapeDtypeStruct((T, LANES), I32), grid=(T // tm,),
        in_specs=[row, row, pl.BlockSpec((1, LANES), lambda i: (0, 0))], out_specs=row,
        compiler_params=_cparams(1, 32), name="dest_rows",
    )(e8, p8, pad_start_lanes)


def _token_copy(src, src_token, dst, dst_token, sem, sub):
    return pltpu.make_async_copy(src.at[pl.ds(pl.multiple_of(src_token * sub, sub), sub), :],
                                 dst.at[pl.ds(pl.multiple_of(dst_token * sub, sub), sub), :], sem)


def _dispatch_kernel(pad_end_ref, padded_ref, dest_ref, hp_ref, xs_ref, zero_ref, sem_ref, *, tm, rows, n_blocks, sub):
    @pl.when(pl.program_id(0) == 0)
    def _():
        zero_ref[...] = jnp.zeros_like(zero_ref)

        def fill(e, do_wait):
            @pl.when(padded_ref[e] > 0)
            def _():
                start = pl.multiple_of((pad_end_ref[e] - rows) * sub, rows * sub)
                cp = pltpu.make_async_copy(zero_ref, xs_ref.at[pl.ds(start, rows * sub), :], sem_ref)
                if do_wait:
                    cp.wait()
                else:
                    cp.start()

        lax.fori_loop(0, N_EXPERTS, lambda e, c: (fill(e, False), c)[1], 0)
        lax.fori_loop(0, N_EXPERTS, lambda e, c: (fill(e, True), c)[1], 0)

        def tail(b):
            return pltpu.make_async_copy(zero_ref, xs_ref.at[pl.ds(pl.multiple_of(b * rows * sub, rows * sub), rows * sub), :],
                                         sem_ref)

        n_used = pad_end_ref[N_EXPERTS - 1] // rows
        lax.fori_loop(n_used, n_blocks, lambda b, c: (tail(b).start(), c)[1], 0)
        lax.fori_loop(n_used, n_blocks, lambda b, c: (tail(b).wait(), c)[1], 0)

    def issue(t, c):
        for k in range(TOP_K):
            _token_copy(hp_ref, t, xs_ref, dest_ref[t * TOP_K + k], sem_ref, sub).start(priority=k % 2)
        return c

    def drain(t, c):
        for k in range(TOP_K):
            _token_copy(hp_ref, t, xs_ref, dest_ref[t * TOP_K + k], sem_ref, sub).wait()
        return c

    lax.fori_loop(0, tm, issue, 0)
    lax.fori_loop(0, tm, drain, 0)


def _dispatch(hp, dest_flat, pad_end, padded, n_rows, sub):
    tm = TOKEN_TILE
    T = hp.shape[0] // sub
    grid_spec = pltpu.PrefetchScalarGridSpec(
        num_scalar_prefetch=2,
        grid=(T // tm,),
        in_specs=[pl.BlockSpec((tm * TOP_K,), lambda i, pe, pd: (i,), memory_space=pltpu.SMEM),
                  pl.BlockSpec((tm * sub, LANES), lambda i, pe, pd: (i, 0))],
        out_specs=pl.BlockSpec(memory_space=pl.ANY),
        scratch_shapes=[pltpu.VMEM((EXPERT_ROWS * sub, LANES), U32), pltpu.SemaphoreType.DMA(())],
    )
    return pl.pallas_call(
        functools.partial(_dispatch_kernel, tm=tm, rows=EXPERT_ROWS, n_blocks=n_rows // EXPERT_ROWS, sub=sub),
        out_shape=jax.ShapeDtypeStruct((n_rows * sub, LANES), U32),
        grid_spec=grid_spec,
        compiler_params=_cparams(1, 32),
        name="dispatch",
    )(pad_end, padded, dest_flat, hp)


def _expert_kernel(blk_e_ref, next_e_ref, n_used_ref, x_ref, wg_hbm, wu_hbm, wd_hbm, y_ref,
                   wg_f32, wu_f32, wd_f32, wg_bf, wu_bf, wd_bf, sem_ref, slot_ref, *, layer, rows, width):
    i = pl.program_id(0)
    e = blk_e_ref[i]
    active = i < n_used_ref[0]
    first = jnp.logical_or(i == 0, e != blk_e_ref[jnp.maximum(i - 1, 0)])

    def fetch(expert, slot):
        return [pltpu.make_async_copy(w.at[layer, expert], buf.at[slot], sem_ref.at[slot])
                for w, buf in ((wg_hbm, wg_f32), (wu_hbm, wu_f32), (wd_hbm, wd_f32))]

    @pl.when(i == 0)
    def _():
        slot_ref[0] = 0
        for cp in fetch(e, 0):
            cp.start()

    @pl.when(jnp.logical_and(active, first))
    def _():
        slot = slot_ref[0]
        for cp in fetch(e, slot):
            cp.wait()
        nxt = next_e_ref[i]

        @pl.when(nxt >= 0)
        def _():
            for cp in fetch(nxt, 1 - slot):
                cp.start()

        wg_bf[...] = wg_f32[slot].astype(BF16)
        wu_bf[...] = wu_f32[slot].astype(BF16)
        wd_bf[...] = wd_f32[slot].astype(BF16)
        slot_ref[0] = 1 - slot

    @pl.when(active)
    def _():
        x = _unpack_pairs(_load_token_rows(x_ref, 0, rows, width))
        g = _dot(x, wg_bf[...])
        u = _dot(x, wu_bf[...])
        hb = (g * _sigmoid(g)) * u
        _store_token_rows(y_ref, _pack_pairs(_dot(hb.astype(BF16), wd_bf[...])))

    @pl.when(jnp.logical_not(active))
    def _():
        y_ref[...] = jnp.zeros_like(y_ref)


def _experts(xs, blk_e, next_e, n_used, w_gate, w_up, w_down, layer):
    _, E, D, Hx = w_gate.shape
    W = D // 2
    sub = _rows_per_token(W)
    M = EXPERT_ROWS
    nblk = xs.shape[0] // (M * sub)
    xrow = lambda i, be, ne, nu: (jnp.minimum(i, nu[0] - 1), 0)
    hbm = pl.BlockSpec(memory_space=pl.ANY)
    grid_spec = pltpu.PrefetchScalarGridSpec(
        num_scalar_prefetch=3,
        grid=(nblk,),
        in_specs=[pl.BlockSpec((M * sub, LANES), xrow), hbm, hbm, hbm],
        out_specs=pl.BlockSpec((M * sub, LANES), lambda i, be, ne, nu: (i, 0)),
        scratch_shapes=[pltpu.VMEM((2, D, Hx), F32), pltpu.VMEM((2, D, Hx), F32), pltpu.VMEM((2, Hx, D), F32),
                        pltpu.VMEM((D, Hx), BF16), pltpu.VMEM((D, Hx), BF16), pltpu.VMEM((Hx, D), BF16),
                        pltpu.SemaphoreType.DMA((2,)), pltpu.SMEM((1,), I32)],
    )
    return pl.pallas_call(
        functools.partial(_expert_kernel, layer=layer, rows=M, width=W),
        out_shape=jax.ShapeDtypeStruct(xs.shape, U32),
        grid_spec=grid_spec,
        compiler_params=_cparams(1, 52),
        name="experts",
    )(blk_e, next_e, n_used, xs, w_gate, w_up, w_down)


def _next_expert(blk_e, cnt):
    E = cnt.shape[0]
    idx = jnp.where(cnt > 0, jnp.arange(E, dtype=I32), E)
    later = jnp.concatenate([lax.cummin(idx, axis=0, reverse=True)[1:], jnp.full((1,), E, I32)])
    return jnp.where(later < E, later, -1)[blk_e]


def _unpack_pairs_f32(w):
    a = lax.bitcast_convert_type(w & jnp.uint32(0xFFFF0000), F32)
    b = lax.bitcast_convert_type(w << 16, F32)
    return jnp.concatenate([a, b], axis=1)


def _combine_kernel(dcur_ref, dnext_ref, x_ref, ysh_ref, w8_ref, g2_ref, fn_ref, ys_ref, o_ref, buf_ref, sem_ref,
                    *, tm, n_steps, width, final_norm):
    i = pl.program_id(0)
    slot = lax.rem(i, 2)
    sub = _rows_per_token(width)

    def issue(dref, s):
        def body(t, c):
            for k in range(TOP_K):
                _token_copy(ys_ref, dref[t * TOP_K + k], buf_ref.at[s], k * tm + t, sem_ref.at[s], sub).start(priority=k % 2)
            return c
        lax.fori_loop(0, tm, body, 0)

    @pl.when(i == 0)
    def _():
        issue(dcur_ref, 0)

    @pl.when(i + 1 < n_steps)
    def _():
        issue(dnext_ref, 1 - slot)

    def drain(t, c):
        for k in range(TOP_K):
            _token_copy(ys_ref, 0, buf_ref.at[slot], 0, sem_ref.at[slot], sub).wait()
        return c

    lax.fori_loop(0, tm, drain, 0)
    w8 = w8_ref[...]
    y = _unpack_pairs_f32(_load_token_rows(ysh_ref, 0, tm, width))
    gathered = buf_ref.at[slot]
    for k in range(TOP_K):
        y = y + w8[:, k:k + 1] * _unpack_pairs_f32(_load_token_rows(gathered, k * tm, tm, width))
    out = x_ref[...] + g2_ref[...] * y
    if final_norm:
        out = out * lax.rsqrt(jnp.mean(out * out, axis=-1, keepdims=True) + RMS_EPS) * fn_ref[...]
    o_ref[...] = out


def _combine(x, ysh, ys, dest_flat, w8, gate2, final_g, S, final_norm):
    T, D = x.shape
    W = D // 2
    sub = _rows_per_token(W)
    tm = TOKEN_TILE
    nb = S // tm
    n_steps = T // tm
    row = lambda i: (i, 0)
    return pl.pallas_call(
        functools.partial(_combine_kernel, tm=tm, n_steps=n_steps, width=W, final_norm=final_norm),
        out_shape=jax.ShapeDtypeStruct((T, D), F32),
        grid=(n_steps,),
        in_specs=[pl.BlockSpec((tm * TOP_K,), lambda i: (i,), memory_space=pltpu.SMEM),
                  pl.BlockSpec((tm * TOP_K,), lambda i: (jnp.minimum(i + 1, n_steps - 1),), memory_space=pltpu.SMEM),
                  pl.BlockSpec((tm, D), row), pl.BlockSpec((tm * sub, LANES), row),
                  pl.BlockSpec((tm, LANES), row),
                  pl.BlockSpec((None, 1, D), lambda i: (i // nb, 0, 0)),
                  pl.BlockSpec((1, D), lambda i: (0, 0)),
                  pl.BlockSpec(memory_space=pl.ANY)],
        out_specs=pl.BlockSpec((tm, D), row),
        scratch_shapes=[pltpu.VMEM((2, TOP_K * tm * sub, LANES), U32), pltpu.SemaphoreType.DMA((2,))],
        compiler_params=_cparams(1, 32),
        name="combine",
    )(dest_flat, dest_flat, x, ysh, w8, gate2, final_g, ys)


def _pad_lanes(v, offset=0):
    out = jnp.zeros((1, LANES), F32)
    return out.at[0, offset:offset + v.shape[0]].set(v.astype(F32))


def kernel(x, c, positions, attn_norm, w_ada, b_ada, w_in, fox_bias, gla_w_gate, gla_b_gate, gla_norm,
           w_branch, w_out, ffn_norm, w_router, router_bias, w_exp_gate, w_exp_up, w_exp_down,
           w_sh_gate, w_sh_up, w_sh_down, final_norm):
    B, S, D = x.shape
    L = w_ada.shape[0]
    T = B * S
    E = N_EXPERTS
    M = EXPERT_ROWS
    xf = x.reshape(T, D)

    c_pad = jnp.zeros((SUBLANES, D), F32).at[:B].set(c)
    mod = _ada_mod(c_pad, w_ada, b_ada)

    half = ROPE_DIM // 2
    inv_freq = jnp.power(ROPE_THETA, -jnp.arange(half, dtype=F32) * 2.0 / ROPE_DIM)
    freq_lanes = _pad_lanes(jnp.concatenate([inv_freq, inv_freq]))
    cos_tab, sin_tab = _rope_tables(positions.reshape(T, 1), freq_lanes)

    w_main_t, w_misc_t = _input_weights(w_in)
    w_branch_bf16 = w_branch.astype(BF16)
    n_blocks = (T * TOP_K + E * (M - 1) + M - 1) // M
    P = n_blocks * M

    for l in range(L):
        m6 = mod[l, :B].reshape(B, 6, 1, D)
        shift1, scale1, gate1, shift2, scale2, gate2 = (m6[:, n] for n in range(6))

        h1 = _norm_mod(xf, attn_norm[l].reshape(1, D), scale1, shift1, S)
        proj = _nt_matmul(h1, w_main_t, l, tm=1024, tn=2048, out_dtype=BF16)
        misc = _nt_matmul(h1, w_misc_t, l, tm=1024, tn=LANES, out_dtype=F32)

        o_a = _sb_attention(proj, B, S, COL_SB)
        qa, ka, va = _fox_prep(proj, misc, _pad_lanes(fox_bias[l], MISC_FOX_F), B, S)
        o_b = _flash_attention(qa, ka, va, B, S)
        qa, ka, va = _moba_prep(proj, cos_tab, sin_tab, B, S)
        o_c = _flash_attention(qa, ka, va, B, S)
        wg_pad = jnp.zeros((LANES, gla_w_gate.shape[2]), F32).at[MISC_GLR:MISC_GLR + GLA_GATE_RANK].set(gla_w_gate[l])
        o_d = _gla(proj, misc, wg_pad, gla_b_gate[l].reshape(1, -1), gla_norm[l].reshape(1, -1), B, S)

        merged = _merge((o_a, o_b, o_c, o_d), proj, w_branch_bf16, l)
        xf = _matmul(merged, w_out, tm=512, tn=1024, residual=(xf, gate1), S=S, layer=l)

        wr_pad = jnp.zeros((D, LANES), F32).at[:, :E].set(w_router[l])
        hp, logits = _norm_mod(xf, ffn_norm[l].reshape(1, D), scale2, shift2, S, w_router_pad=wr_pad)
        e8, w8, p8, counts = _route(logits, _pad_lanes(router_bias[l]))
        cnt = counts[0, :E].astype(I32)
        padded = (cnt + M - 1) // M * M
        pad_end = jnp.cumsum(padded)
        pad_start = pad_end - padded
        dest = _dest_rows(e8, p8, _pad_lanes(pad_start))[:, :TOP_K].reshape(T * TOP_K)
        blk_start = jnp.arange(n_blocks, dtype=I32) * M
        blk_e = jnp.minimum(jnp.sum((pad_end[None, :] <= blk_start[:, None]).astype(I32), axis=1), E - 1)
        n_used = (pad_end[E - 1:] // M).astype(I32)

        xs = _dispatch(hp, dest, pad_end.astype(I32), padded.astype(I32), P, _rows_per_token(D // 2))
        ys = _experts(xs, blk_e, _next_expert(blk_e, cnt), n_used, w_exp_gate, w_exp_up, w_exp_down, l)
        ysh = _experts(hp, jnp.zeros((T // M,), I32), jnp.full((T // M,), -1, I32), jnp.full((1,), T // M, I32),
                       w_sh_gate[:, None], w_sh_up[:, None], w_sh_down[:, None], l)
        last = l == L - 1
        xf = _combine(xf, ysh, ys, dest, w8, gate2, final_norm.reshape(1, D), S, final_norm=last)

    return xf.reshape(B, S, D)
```

```python
import functools

import jax
import jax.numpy as jnp
from jax import lax
from jax.experimental import pallas as pl
from jax.experimental.pallas import tpu as pltpu

F32 = jnp.float32
BF16 = jnp.bfloat16
I32 = jnp.int32
U32 = jnp.uint32

HEAD_DIM = 128
N_HEADS = 4
BRANCH_WIDTH = N_HEADS * HEAD_DIM
GLA_HEAD_K = 64
GLA_CHUNK = 64
GLA_GATE_RANK = 16
GLA_GATE_NORMALIZER = 16.0
MOBA_BLOCK = 256
MOBA_TOPK = 3
ROPE_THETA = 500000.0
ROPE_DIM = HEAD_DIM // 4
N_EXPERTS = 64
GROUP_SIZE = 8
TOPK_GROUPS = 4
TOP_K = 8
ROUTED_SCALE = 2.5
RMS_EPS = 1e-6

LANES = 128
SUBLANES = 8
MIB = 1024 * 1024
NEG_BIG = -1e30

ATT_BLOCK = 256
FLASH_Q_BLOCK = 512
EXPERT_ROWS = 256
TOKEN_TILE = 128

COL_SB, COL_FOX, COL_MOBA = 0, 12, 24
COL_GQ, COL_GK, COL_GV, COL_GR, COL_GATES = 36, 38, 40, 44, 48
SEG_B, SEG_C = COL_MOBA, COL_GATES
MISC_FOX_F, MISC_GLR = 0, 4


def _cparams(n_axes, vmem_mib):
    return pltpu.CompilerParams(dimension_semantics=("arbitrary",) * n_axes,
                                vmem_limit_bytes=vmem_mib * MIB)


def _softplus(z):
    return jnp.maximum(z, 0.0) + jnp.log(1.0 + jnp.exp(-jnp.abs(z)))


def _log_sigmoid(z):
    return -_softplus(-z)


def _sigmoid(z):
    return 1.0 / (1.0 + jnp.exp(-z))


def _dot(a, b, precision=None):
    return jnp.dot(a, b, preferred_element_type=F32, precision=precision)


def _dot_nt(a, b, precision=None):
    return lax.dot_general(a, b, (((1,), (1,)), ((), ())), preferred_element_type=F32,
                           precision=precision)


def _dot_tn(a, b, precision=None):
    return lax.dot_general(a, b, (((0,), (0,)), ((), ())), preferred_element_type=F32,
                           precision=precision)


HIGHEST = lax.Precision.HIGHEST


def _split3(x):
    hi = x.astype(BF16)
    r1 = x - hi.astype(F32)
    mid = r1.astype(BF16)
    lo = (r1 - mid.astype(F32)).astype(BF16)
    return hi, mid, lo


def _ada_kernel(c_ref, w_ref, b_ref, o_ref):
    c = c_ref[...]
    c_act = (c * _sigmoid(c)).astype(BF16)
    o_ref[...] = _dot(c_act, w_ref[...].astype(BF16)) + b_ref[...]


def _ada_mod(c_pad, w_ada, b_ada):
    L, D, N = w_ada.shape
    tn = 1536
    return pl.pallas_call(
        _ada_kernel,
        out_shape=jax.ShapeDtypeStruct((L, SUBLANES, N), F32),
        grid=(L, N // tn),
        in_specs=[pl.BlockSpec((SUBLANES, D), lambda l, j: (0, 0)),
                  pl.BlockSpec((None, D, tn), lambda l, j: (l, 0, j)),
                  pl.BlockSpec((None, 1, tn), lambda l, j: (l, 0, j))],
        out_specs=pl.BlockSpec((None, SUBLANES, tn), lambda l, j: (l, 0, j)),
        compiler_params=_cparams(2, 40),
        name="ada_mod",
    )(c_pad, w_ada, b_ada.reshape(L, 1, N))


def _normed(x, g, scale, shift):
    y = x * lax.rsqrt(jnp.mean(x * x, axis=-1, keepdims=True) + RMS_EPS)
    return (y * g) * (1.0 + scale) + shift


def _norm_mod_kernel(x_ref, g_ref, sc_ref, sh_ref, h_ref):
    h_ref[...] = _normed(x_ref[...], g_ref[...], sc_ref[...], sh_ref[...]).astype(BF16)


def _pack_pairs(h):
    half = h.shape[1] // 2
    hi = lax.bitcast_convert_type(h[:, :half].astype(BF16).astype(F32), U32)
    lo = lax.bitcast_convert_type(h[:, half:].astype(BF16).astype(F32), U32)
    return hi | (lo >> 16)


def _unpack_pairs(w):
    a = lax.bitcast_convert_type(w & jnp.uint32(0xFFFF0000), F32).astype(BF16)
    b = lax.bitcast_convert_type(w << 16, F32).astype(BF16)
    return jnp.concatenate([a, b], axis=1)


def _rows_per_token(width):
    return width // LANES


def _store_token_rows(ref, val):
    n, width = val.shape
    sub = _rows_per_token(width)
    for s in range(sub):
        ref[pl.ds(s, n, stride=sub), :] = val[:, s * LANES:(s + 1) * LANES]


def _load_token_rows(ref, first_token, n, width):
    sub = _rows_per_token(width)
    return jnp.concatenate([ref[pl.ds(first_token * sub + s, n, stride=sub), :] for s in range(sub)], axis=1)


def _norm_route_kernel(x_ref, g_ref, sc_ref, sh_ref, wr_ref, hp_ref, lg_ref):
    h = _normed(x_ref[...], g_ref[...], sc_ref[...], sh_ref[...])
    _store_token_rows(hp_ref, _pack_pairs(h))
    lg_ref[...] = _dot(h, wr_ref[...], precision=HIGHEST)


def _norm_mod(x, g, scale, shift, S, w_router_pad=None):
    T, D = x.shape
    tm = 512
    nb = S // tm
    row = lambda i: (i, 0)
    per_batch = pl.BlockSpec((None, 1, D), lambda i: (i // nb, 0, 0))
    in_specs = [pl.BlockSpec((tm, D), row), pl.BlockSpec((1, D), lambda i: (0, 0)), per_batch, per_batch]
    if w_router_pad is None:
        return pl.pallas_call(
            _norm_mod_kernel, out_shape=jax.ShapeDtypeStruct((T, D), BF16), grid=(T // tm,),
            in_specs=in_specs, out_specs=pl.BlockSpec((tm, D), row),
            compiler_params=_cparams(1, 32), name="norm_mod",
        )(x, g, scale, shift)
    return pl.pallas_call(
        _norm_route_kernel,
        out_shape=(jax.ShapeDtypeStruct((T * (D // 2) // LANES, LANES), U32), jax.ShapeDtypeStruct((T, LANES), F32)),
        grid=(T // tm,),
        in_specs=in_specs + [pl.BlockSpec((D, LANES), lambda i: (0, 0))],
        out_specs=(pl.BlockSpec((tm * (D // 2) // LANES, LANES), row), pl.BlockSpec((tm, LANES), row)),
        compiler_params=_cparams(1, 32), name="norm_route",
    )(x, g, scale, shift, w_router_pad)


def _mm_kernel(*refs, cast_w, residual):
    if residual:
        a_ref, w_ref, x_ref, g_ref, o_ref = refs[:5]
        scratch = refs[5:]
    else:
        a_ref, w_ref, o_ref = refs[:3]
        scratch = refs[3:]
    if cast_w:
        wbf_ref, = scratch

        @pl.when(pl.program_id(1) == 0)
        def _():
            wbf_ref[...] = w_ref[...].astype(BF16)

        w = wbf_ref[...]
    else:
        w = w_ref[...]
    acc = _dot(a_ref[...], w)
    if residual:
        acc = x_ref[...] + g_ref[...] * acc
    o_ref[...] = acc.astype(o_ref.dtype)


def _matmul(a, w, tm, tn, out_dtype=F32, residual=None, S=None, vmem_mib=48, layer=None):
    M, K = a.shape
    N = w.shape[-1]
    cast_w = w.dtype != BF16
    if layer is None:
        w_spec = pl.BlockSpec((K, tn), lambda j, i: (0, j))
    else:
        w_spec = pl.BlockSpec((None, K, tn), lambda j, i: (layer, 0, j))
    in_specs = [pl.BlockSpec((tm, K), lambda j, i: (i, 0)), w_spec]
    args = [a, w]
    if residual is not None:
        x, gate = residual
        nb = S // tm
        in_specs += [pl.BlockSpec((tm, tn), lambda j, i: (i, j)),
                     pl.BlockSpec((None, 1, tn), lambda j, i: (i // nb, 0, j))]
        args += [x, gate]
    return pl.pallas_call(
        functools.partial(_mm_kernel, cast_w=cast_w, residual=residual is not None),
        out_shape=jax.ShapeDtypeStruct((M, N), out_dtype),
        grid=(N // tn, M // tm),
        in_specs=in_specs,
        out_specs=pl.BlockSpec((tm, tn), lambda j, i: (i, j)),
        scratch_shapes=[pltpu.VMEM((K, tn), BF16)] if cast_w else [],
        compiler_params=_cparams(2, vmem_mib),
        name="matmul_res" if residual is not None else "matmul",
    )(*args)


def _nt_matmul_kernel(a_ref, w_ref, o_ref):
    o_ref[...] = _dot_nt(a_ref[...], w_ref[...]).astype(o_ref.dtype)


def _nt_matmul(a, wt, layer, tm, tn, out_dtype):
    M, K = a.shape
    N = wt.shape[1]
    return pl.pallas_call(
        _nt_matmul_kernel,
        out_shape=jax.ShapeDtypeStruct((M, N), out_dtype),
        grid=(N // tn, M // tm),
        in_specs=[pl.BlockSpec((tm, K), lambda j, i: (i, 0)),
                  pl.BlockSpec((None, tn, K), lambda j, i: (layer, j, 0))],
        out_specs=pl.BlockSpec((tm, tn), lambda j, i: (i, j)),
        compiler_params=_cparams(2, 48),
        name="in_proj",
    )(a, wt)


def _input_weights(w_in):
    c_fox_f = COL_MOBA * LANES
    c_glr = COL_GATES * LANES + N_HEADS
    wt = jnp.swapaxes(w_in, 1, 2)
    seg_a = wt[:, :c_fox_f].astype(BF16)
    seg_b = wt[:, c_fox_f + N_HEADS:c_glr].astype(BF16)
    seg_c = wt[:, c_glr + GLA_GATE_RANK:].astype(BF16)
    pad = jnp.zeros((wt.shape[0], LANES - N_HEADS - GLA_GATE_RANK, wt.shape[2]), wt.dtype)
    misc = jnp.concatenate([wt[:, c_fox_f:c_fox_f + N_HEADS], wt[:, c_glr:c_glr + GLA_GATE_RANK], pad],
                           axis=1).astype(BF16)
    return seg_a, seg_b, seg_c, misc


SB_LOG_WEIGHT_FLOOR = -110.0


def _sb_kernel(q_ref, k_ref, v_ref, o_ref, *, blk, scale):
    i = pl.program_id(1)
    row = lax.broadcasted_iota(I32, (blk, blk), 0)
    col = lax.broadcasted_iota(I32, (blk, blk), 1)
    later = (row > col).astype(BF16)

    def cond(state):
        jj, alive, _ = state
        return jnp.logical_and(jj <= i, alive)

    def body(state):
        jj, _, heads = state
        j = i - jj
        start = pl.multiple_of(j * blk, blk)
        past = jnp.logical_or(j < i, col < row)
        new, top = [], None
        for h in range(N_HEADS):
            c, acc = heads[h]
            hs = slice(h * HEAD_DIM, (h + 1) * HEAD_DIM)
            q = (q_ref[:, hs].astype(F32) * scale).astype(BF16)
            k = k_ref[pl.ds(start, blk), hs]
            v = v_ref[pl.ds(start, blk), hs]
            z = _dot_nt(q, k)
            ls = jnp.where(past, -_softplus(z), 0.0)
            hi = ls.astype(BF16)
            lo = (ls - hi.astype(F32)).astype(BF16)
            between = _dot(hi, later) + _dot(lo, later)
            w = jnp.where(past, jnp.exp(z + ls + between + c), 0.0)
            acc = acc + _dot(w.astype(BF16), v)
            c = c + jnp.sum(ls, axis=1, keepdims=True)
            new.append((c, acc))
            top = c if top is None else jnp.maximum(top, c)
        return jj + 1, jnp.max(top) > SB_LOG_WEIGHT_FLOOR, tuple(new)

    init = tuple((jnp.zeros((blk, 1), F32), jnp.zeros((blk, HEAD_DIM), F32)) for _ in range(N_HEADS))
    _, _, heads = lax.while_loop(cond, body, (jnp.int32(0), jnp.bool_(True), init))
    for h in range(N_HEADS):
        o_ref[:, h * HEAD_DIM:(h + 1) * HEAD_DIM] = heads[h][1]


def _sb_attention(proj, B, S, col0):
    T = proj.shape[0]
    blk = ATT_BLOCK
    nq = S // blk
    W = BRANCH_WIDTH
    cb = col0 * LANES // W
    return pl.pallas_call(
        functools.partial(_sb_kernel, blk=blk, scale=HEAD_DIM ** -0.5),
        out_shape=jax.ShapeDtypeStruct((T, W), F32),
        grid=(B, nq),
        in_specs=[pl.BlockSpec((blk, W), lambda b, i: (b * nq + i, cb)),
                  pl.BlockSpec((S, W), lambda b, i: (b, cb + 1)),
                  pl.BlockSpec((S, W), lambda b, i: (b, cb + 2))],
        out_specs=pl.BlockSpec((blk, W), lambda b, i: (b * nq + i, 0)),
        compiler_params=_cparams(2, 44),
        name="sb_attention",
    )(proj, proj, proj)


def _flash_kernel(qa_ref, ka_ref, v_ref, o_ref, *, bq, bk):
    i = pl.program_id(1)
    A = 2 * HEAD_DIM
    ratio = bq // bk
    row = lax.broadcasted_iota(I32, (bq, bk), 0)
    col = lax.broadcasted_iota(I32, (bq, bk), 1)

    def step(j, carry, key_offset=None):
        start = pl.multiple_of(j * bk, bk)
        new = []
        for h in range(N_HEADS):
            m, l, acc = carry[h]
            s = _dot_nt(qa_ref[:, h * A:(h + 1) * A], ka_ref[pl.ds(start, bk), h * A:(h + 1) * A])
            if key_offset is not None:
                s = jnp.where(col + key_offset <= row, s, NEG_BIG)
            m_new = jnp.maximum(m, jnp.max(s, axis=1, keepdims=True))
            alpha = jnp.exp(m - m_new)
            p = jnp.exp(s - m_new)
            l = alpha * l + jnp.sum(p, axis=1, keepdims=True)
            acc = alpha * acc + _dot(p.astype(BF16), v_ref[pl.ds(start, bk), h * HEAD_DIM:(h + 1) * HEAD_DIM])
            new.append((m_new, l, acc))
        return tuple(new)

    init = tuple((jnp.full((bq, 1), NEG_BIG, F32), jnp.zeros((bq, 1), F32),
                  jnp.zeros((bq, HEAD_DIM), F32)) for _ in range(N_HEADS))
    heads = lax.fori_loop(0, i * ratio, step, init)
    for d in range(ratio):
        heads = step(i * ratio + d, heads, key_offset=d * bk)
    for h in range(N_HEADS):
        _, l, acc = heads[h]
        o_ref[:, h * HEAD_DIM:(h + 1) * HEAD_DIM] = acc / l


def _flash_attention(qa, ka, va, B, S):
    T = qa.shape[0]
    bq, bk = FLASH_Q_BLOCK, ATT_BLOCK
    nq = S // bq
    W = BRANCH_WIDTH
    return pl.pallas_call(
        functools.partial(_flash_kernel, bq=bq, bk=bk),
        out_shape=jax.ShapeDtypeStruct((T, W), F32),
        grid=(B, nq),
        in_specs=[pl.BlockSpec((bq, 2 * W), lambda b, i: (b * nq + i, 0)),
                  pl.BlockSpec((S, 2 * W), lambda b, i: (b, 0)),
                  pl.BlockSpec((S, W), lambda b, i: (b, 0))],
        out_specs=pl.BlockSpec((bq, W), lambda b, i: (b * nq + i, 0)),
        compiler_params=_cparams(2, 44),
        name="flash_attention",
    )(qa, ka, va)


def _fox_prep_kernel(q_ref, k_ref, v_ref, misc_ref, bias_ref, qa_ref, ka_ref, va_ref, carry_ref, *, tb, scale):
    va_ref[...] = v_ref[...].astype(BF16)
    @pl.when(pl.program_id(1) == 0)
    def _():
        carry_ref[...] = jnp.zeros_like(carry_ref)

    lane = lax.broadcasted_iota(I32, (tb, LANES), 1)
    lf = jnp.where(lane < N_HEADS, _log_sigmoid(misc_ref[...] + bias_ref[...]), 0.0)
    row = lax.broadcasted_iota(I32, (tb, tb), 0)
    col = lax.broadcasted_iota(I32, (tb, tb), 1)
    incl = (col <= row).astype(F32)
    F = _dot(incl, lf, precision=HIGHEST) + carry_ref[0:1, :]
    carry_ref[0:1, :] = F[tb - 1:tb, :]
    for h in range(N_HEADS):
        Fh = jnp.broadcast_to(F[:, h:h + 1], (tb, LANES))
        hi, mid, lo = (p.astype(F32) for p in _split3(Fh))
        ones = jnp.ones((tb, LANES), F32)
        zeros = jnp.zeros((tb, LANES), F32)
        eq = jnp.where(lane == 0, hi, jnp.where(lane == 1, mid, jnp.where(lane == 2, lo,
             jnp.where(lane < 6, ones, zeros))))
        ek = jnp.where(lane < 3, ones, jnp.where(lane == 3, -hi, jnp.where(lane == 4, -mid,
             jnp.where(lane == 5, -lo, zeros))))
        hs = slice(h * HEAD_DIM, (h + 1) * HEAD_DIM)
        qa_ref[:, 2 * h * HEAD_DIM:(2 * h + 1) * HEAD_DIM] = (q_ref[:, hs].astype(F32) * scale).astype(BF16)
        qa_ref[:, (2 * h + 1) * HEAD_DIM:(2 * h + 2) * HEAD_DIM] = eq.astype(BF16)
        ka_ref[:, 2 * h * HEAD_DIM:(2 * h + 1) * HEAD_DIM] = k_ref[:, hs].astype(BF16)
        ka_ref[:, (2 * h + 1) * HEAD_DIM:(2 * h + 2) * HEAD_DIM] = ek.astype(BF16)


def _fox_prep(proj, misc, fox_bias_pad, B, S):
    T = proj.shape[0]
    tb = 512
    nb = S // tb
    qblk = COL_FOX * LANES // BRANCH_WIDTH
    out = jax.ShapeDtypeStruct((T, 2 * BRANCH_WIDTH), BF16)
    return pl.pallas_call(
        functools.partial(_fox_prep_kernel, tb=tb, scale=HEAD_DIM ** -0.5),
        out_shape=(out, out, jax.ShapeDtypeStruct((T, BRANCH_WIDTH), BF16)),
        grid=(B, nb),
        in_specs=[pl.BlockSpec((tb, BRANCH_WIDTH), lambda b, i: (b * nb + i, qblk)),
                  pl.BlockSpec((tb, BRANCH_WIDTH), lambda b, i: (b * nb + i, qblk + 1)),
                  pl.BlockSpec((tb, BRANCH_WIDTH), lambda b, i: (b * nb + i, qblk + 2)),
                  pl.BlockSpec((tb, LANES), lambda b, i: (b * nb + i, 0)),
                  pl.BlockSpec((1, LANES), lambda b, i: (0, 0))],
        out_specs=(pl.BlockSpec((tb, 2 * BRANCH_WIDTH), lambda b, i: (b * nb + i, 0)),
                   pl.BlockSpec((tb, 2 * BRANCH_WIDTH), lambda b, i: (b * nb + i, 0)),
                   pl.BlockSpec((tb, BRANCH_WIDTH), lambda b, i: (b * nb + i, 0))),
        scratch_shapes=[pltpu.VMEM((SUBLANES, LANES), F32)],
        compiler_params=_cparams(2, 32),
        name="fox_prep",
    )(proj, proj, proj, misc, fox_bias_pad)


def _rope(x, cos, sin, lane):
    half = ROPE_DIM // 2
    up = pltpu.roll(x, half, 1)
    down = pltpu.roll(x, LANES - half, 1)
    rot = jnp.where(lane < half, -down * sin, jnp.where(lane < ROPE_DIM, up * sin, 0.0))
    return x * jnp.where(lane < ROPE_DIM, cos, 1.0) + rot


def _rope_table_kernel(pos_ref, freq_ref, cos_ref, sin_ref):
    ang = pos_ref[...].astype(F32) * freq_ref[...]
    cos_ref[...] = jnp.cos(ang)
    sin_ref[...] = jnp.sin(ang)


def _rope_tables(pos_col, freq_lanes):
    T = pos_col.shape[0]
    tb = 1024
    row = lambda i: (i, 0)
    out = jax.ShapeDtypeStruct((T, LANES), F32)
    return pl.pallas_call(
        _rope_table_kernel, out_shape=(out, out), grid=(T // tb,),
        in_specs=[pl.BlockSpec((tb, 1), row), pl.BlockSpec((1, LANES), lambda i: (0, 0))],
        out_specs=(pl.BlockSpec((tb, LANES), row), pl.BlockSpec((tb, LANES), row)),
        compiler_params=_cparams(1, 32), name="rope_tables",
    )(pos_col, freq_lanes)


def _moba_k_kernel(k_ref, v_ref, cos_ref, sin_ref, ka_ref, va_ref, kmean_ref, *, tb):
    i = pl.program_id(1)
    va_ref[...] = v_ref[...].astype(BF16)

    @pl.when(i == 0)
    def _():
        kmean_ref[...] = jnp.zeros_like(kmean_ref)

    lane = lax.broadcasted_iota(I32, (tb, LANES), 1)
    cos, sin = cos_ref[...], sin_ref[...]
    onehot = jnp.where(lane == i, 1.0, 0.0).astype(BF16)
    this_row = lax.broadcasted_iota(I32, (LANES, HEAD_DIM), 0) == i
    for h in range(N_HEADS):
        hs = slice(h * HEAD_DIM, (h + 1) * HEAD_DIM)
        kr = _rope(k_ref[:, hs].astype(F32), cos, sin, lane)
        mean = jnp.sum(kr, axis=0, keepdims=True) * (1.0 / tb)
        kmean_ref[:, hs] = jnp.where(this_row, mean, kmean_ref[:, hs])
        ka_ref[:, 2 * h * HEAD_DIM:(2 * h + 1) * HEAD_DIM] = kr.astype(BF16)
        ka_ref[:, (2 * h + 1) * HEAD_DIM:(2 * h + 2) * HEAD_DIM] = onehot


def _moba_q_kernel(q_ref, cos_ref, sin_ref, kmean_ref, qa_ref, *, tb, scale):
    i = pl.program_id(1)
    lane = lax.broadcasted_iota(I32, (tb, LANES), 1)
    cos, sin = cos_ref[...], sin_ref[...]
    neg_inf = jnp.float32(-jnp.inf)
    for h in range(N_HEADS):
        hs = slice(h * HEAD_DIM, (h + 1) * HEAD_DIM)
        qr = _rope(q_ref[:, hs].astype(F32), cos, sin, lane)
        gate = _dot_nt(qr, kmean_ref[:, hs], precision=HIGHEST)
        cur = jnp.where(lane < i, gate, neg_inf)
        chosen = lane == i
        for _ in range(MOBA_TOPK):
            m = jnp.max(cur, axis=1, keepdims=True)
            first = jnp.min(jnp.where(jnp.logical_and(cur == m, m > neg_inf), lane, LANES),
                            axis=1, keepdims=True)
            pick = lane == first
            chosen = jnp.logical_or(chosen, pick)
            cur = jnp.where(pick, neg_inf, cur)
        bias = jnp.where(jnp.logical_or(chosen, lane >= LANES // 2), 0.0, NEG_BIG)
        qa_ref[:, 2 * h * HEAD_DIM:(2 * h + 1) * HEAD_DIM] = (qr * scale).astype(BF16)
        qa_ref[:, (2 * h + 1) * HEAD_DIM:(2 * h + 2) * HEAD_DIM] = bias.astype(BF16)


def _moba_prep(proj, cos_tab, sin_tab, B, S):
    T = proj.shape[0]
    tb = MOBA_BLOCK
    nb = S // tb
    qblk = (COL_MOBA - SEG_B) * LANES // BRANCH_WIDTH
    aug = jax.ShapeDtypeStruct((T, 2 * BRANCH_WIDTH), BF16)
    row_spec = lambda c: pl.BlockSpec((tb, BRANCH_WIDTH), lambda b, i: (b * nb + i, c))
    trig_spec = pl.BlockSpec((tb, LANES), lambda b, i: (b * nb + i, 0))
    aug_spec = pl.BlockSpec((tb, 2 * BRANCH_WIDTH), lambda b, i: (b * nb + i, 0))
    kmean_spec = pl.BlockSpec((None, LANES, BRANCH_WIDTH), lambda b, i: (b, 0, 0))
    ka, va, kmean = pl.pallas_call(
        functools.partial(_moba_k_kernel, tb=tb),
        out_shape=(aug, jax.ShapeDtypeStruct((T, BRANCH_WIDTH), BF16),
                   jax.ShapeDtypeStruct((B, LANES, BRANCH_WIDTH), F32)),
        grid=(B, nb),
        in_specs=[row_spec(qblk + 1), row_spec(qblk + 2), trig_spec, trig_spec],
        out_specs=(aug_spec, pl.BlockSpec((tb, BRANCH_WIDTH), lambda b, i: (b * nb + i, 0)), kmean_spec),
        compiler_params=_cparams(2, 32),
        name="moba_k_prep",
    )(proj, proj, cos_tab, sin_tab)
    qa = pl.pallas_call(
        functools.partial(_moba_q_kernel, tb=tb, scale=HEAD_DIM ** -0.5),
        out_shape=aug,
        grid=(B, nb),
        in_specs=[row_spec(qblk), trig_spec, trig_spec, kmean_spec],
        out_specs=aug_spec,
        compiler_params=_cparams(2, 32),
        name="moba_q_prep",
    )(proj, cos_tab, sin_tab, kmean)
    return qa, ka, va


def _gla_kernel(q_ref, k_ref, v_ref, gr_ref, misc_ref, wg_ref, bg_ref, gn_ref, o_ref, state_ref, *, tb):
    C = GLA_CHUNK
    @pl.when(pl.program_id(1) == 0)
    def _():
        state_ref[...] = jnp.zeros_like(state_ref)

    la_all = _log_sigmoid(_dot(misc_ref[...], wg_ref[...], precision=HIGHEST) + bg_ref[...]) \
        * (1.0 / GLA_GATE_NORMALIZER)
    rowc = lax.broadcasted_iota(I32, (C, C), 0)
    colc = lax.broadcasted_iota(I32, (C, C), 1)
    incl = (colc <= rowc).astype(F32)
    causal = colc <= rowc
    lane = lax.broadcasted_iota(I32, (C, LANES), 1)
    row2 = lax.broadcasted_iota(I32, (2 * GLA_HEAD_K, LANES), 0)
    ones_cv = jnp.ones((C, LANES), F32)
    qscale = GLA_HEAD_K ** -0.5
    gn = gn_ref[...]
    for c in range(tb // C):
        rs = slice(c * C, (c + 1) * C)
        for p in range(2):
            ps = slice(p * LANES, (p + 1) * LANES)
            la = la_all[rs, ps]
            b = _dot(incl, la, precision=HIGHEST)
            b_last = b[C - 1:C, :]
            eb = jnp.exp(b)
            q_dec = q_ref[rs, ps].astype(F32) * qscale * eb
            k = k_ref[rs, ps].astype(F32)
            k_inv = (k * jnp.exp(-b)).astype(BF16)
            k_end = (k * jnp.exp(b_last - b)).astype(BF16)
            decay = jnp.exp(_dot_tn(la, ones_cv, precision=HIGHEST))
            st = state_ref[p * LANES:(p + 1) * LANES, :]
            new_rows = []
            for e in range(2):
                h = 2 * p + e
                own = jnp.logical_and(lane >= e * GLA_HEAD_K, lane < (e + 1) * GLA_HEAD_K)
                qh = jnp.where(own, q_dec, 0.0).astype(BF16)
                v = v_ref[rs, h * HEAD_DIM:(h + 1) * HEAD_DIM].astype(BF16)
                attn = jnp.where(causal, _dot_nt(qh, k_inv), 0.0)
                st_h = jnp.where(jnp.logical_and(row2 >= e * GLA_HEAD_K, row2 < (e + 1) * GLA_HEAD_K), st, 0.0)
                o = _dot(attn.astype(BF16), v) + _dot(qh, st_h.astype(BF16))
                y = o * lax.rsqrt(jnp.mean(o * o, axis=-1, keepdims=True) + RMS_EPS) * gn
                g = gr_ref[rs, h * HEAD_DIM:(h + 1) * HEAD_DIM].astype(F32)
                o_ref[rs, h * HEAD_DIM:(h + 1) * HEAD_DIM] = y * (g * _sigmoid(g))
                new_rows.append(_dot_tn(k_end, v))
            kv = jnp.where(row2 < GLA_HEAD_K, new_rows[0], new_rows[1])
            state_ref[p * LANES:(p + 1) * LANES, :] = decay * st + kv


def _gla(proj, misc, w_gate_pad, b_gate, gla_norm, B, S):
    T = proj.shape[0]
    tb = 512
    nb = S // tb
    kd = 2 * LANES
    rows = lambda w, c: pl.BlockSpec((tb, w), lambda b, i: (b * nb + i, c))
    const = lambda shape: pl.BlockSpec(shape, lambda b, i: (0, 0))
    return pl.pallas_call(
        functools.partial(_gla_kernel, tb=tb),
        out_shape=jax.ShapeDtypeStruct((T, BRANCH_WIDTH), F32),
        grid=(B, nb),
        in_specs=[rows(kd, (COL_GQ - SEG_B) * LANES // kd), rows(kd, (COL_GK - SEG_B) * LANES // kd),
                  rows(BRANCH_WIDTH, (COL_GV - SEG_B) * LANES // BRANCH_WIDTH),
                  rows(BRANCH_WIDTH, (COL_GR - SEG_B) * LANES // BRANCH_WIDTH),
                  rows(LANES, 0), const((LANES, kd)), const((1, kd)), const((1, HEAD_DIM))],
        out_specs=rows(BRANCH_WIDTH, 0),
        scratch_shapes=[pltpu.VMEM((kd, HEAD_DIM), F32)],
        compiler_params=_cparams(2, 32),
        name="gla",
    )(proj, proj, proj, proj, misc, w_gate_pad, b_gate, gla_norm)


def _merge_kernel(oa_ref, ob_ref, oc_ref, od_ref, g0_ref, g1_ref, g2_ref, g3_ref, w_ref, o_ref):
    acc = None
    for n, (b_ref, g_ref) in enumerate(((oa_ref, g0_ref), (ob_ref, g1_ref), (oc_ref, g2_ref), (od_ref, g3_ref))):
        term = _sigmoid(g_ref[...].astype(F32)) * _dot(b_ref[...].astype(BF16), w_ref[n])
        acc = term if acc is None else acc + term
    o_ref[...] = acc.astype(BF16)


def _merge(branches, proj, w_branch_bf16, layer):
    T = proj.shape[0]
    _, nbr, W, D = w_branch_bf16.shape
    tm = 256
    g0 = (COL_GATES - SEG_C) * LANES // D
    br_spec = pl.BlockSpec((tm, W), lambda i: (i, 0))
    gate_spec = lambda n: pl.BlockSpec((tm, D), lambda i: (i, g0 + n))
    return pl.pallas_call(
        _merge_kernel,
        out_shape=jax.ShapeDtypeStruct((T, D), BF16),
        grid=(T // tm,),
        in_specs=[br_spec] * 4 + [gate_spec(n) for n in range(4)]
                 + [pl.BlockSpec((None, nbr, W, D), lambda i: (layer, 0, 0, 0))],
        out_specs=pl.BlockSpec((tm, D), lambda i: (i, 0)),
        compiler_params=_cparams(1, 44),
        name="branch_merge",
    )(*branches, proj, proj, proj, proj, w_branch_bf16)


def _butterfly(x, lane, op):
    for s in (1, 2, 4):
        up = pltpu.roll(x, s, 1)
        down = pltpu.roll(x, LANES - s, 1)
        x = op(x, jnp.where((lane & s) != 0, up, down))
    return x


def _route_kernel(lg_ref, bias_ref, e8_ref, w8_ref, p8_ref, cnt_ref, carry_ref, *, tm):
    @pl.when(pl.program_id(0) == 0)
    def _():
        carry_ref[...] = jnp.zeros_like(carry_ref)

    neg_inf = jnp.float32(-jnp.inf)
    lane = lax.broadcasted_iota(I32, (tm, LANES), 1)
    valid = lane < N_EXPERTS
    scores = _sigmoid(lg_ref[...])
    biased = jnp.where(valid, scores + bias_ref[...], neg_inf)
    g1 = _butterfly(biased, lane, jnp.maximum)
    first = _butterfly(jnp.where(biased == g1, lane, LANES), lane, jnp.minimum)
    g2 = _butterfly(jnp.where(lane == first, neg_inf, biased), lane, jnp.maximum)
    gs = g1 + g2
    gs = jnp.where(valid, gs, pltpu.roll(gs, N_EXPERTS, 1))
    gidx = lane >> 3
    beaten = jnp.zeros((tm, LANES), I32)
    for r in range(1, N_EXPERTS // GROUP_SIZE):
        other = pltpu.roll(gs, GROUP_SIZE * r, 1)
        og = (gidx - r) & (N_EXPERTS // GROUP_SIZE - 1)
        wins = jnp.logical_or(other > gs, jnp.logical_and(other == gs, og < gidx))
        beaten = beaten + wins.astype(I32)
    cur = jnp.where(jnp.logical_and(beaten < TOPK_GROUPS, valid), biased, neg_inf)
    sel = jnp.zeros((tm, LANES), jnp.bool_)
    for _ in range(TOP_K):
        m = jnp.max(cur, axis=1, keepdims=True)
        pick = lane == jnp.min(jnp.where(cur == m, lane, LANES), axis=1, keepdims=True)
        sel = jnp.logical_or(sel, pick)
        cur = jnp.where(pick, neg_inf, cur)
    wsel = jnp.where(sel, scores, 0.0)
    wd = wsel / jnp.sum(wsel, axis=1, keepdims=True) * ROUTED_SCALE
    selb = jnp.where(sel, 1.0, 0.0).astype(BF16)
    row = lax.broadcasted_iota(I32, (tm, tm), 0)
    col = lax.broadcasted_iota(I32, (tm, tm), 1)
    pos = _dot((col < row).astype(BF16), selb) + carry_ref[0:1, :]
    total = carry_ref[0:1, :] + jnp.sum(selb.astype(F32), axis=0, keepdims=True)
    carry_ref[0:1, :] = total
    cnt_ref[...] = jnp.broadcast_to(total, cnt_ref.shape)
    r2 = lax.broadcasted_iota(I32, (LANES, LANES), 0)
    c2 = lax.broadcasted_iota(I32, (LANES, LANES), 1)
    slot = _dot(selb, (r2 < c2).astype(BF16))
    lane_f = lane.astype(F32)
    e8 = jnp.zeros((tm, LANES), F32)
    w8 = jnp.zeros((tm, LANES), F32)
    p8 = jnp.zeros((tm, LANES), F32)
    for k in range(TOP_K):
        mk = jnp.logical_and(sel, slot == k)
        put = lane == k
        e8 = jnp.where(put, jnp.sum(jnp.where(mk, lane_f, 0.0), axis=1, keepdims=True), e8)
        w8 = jnp.where(put, jnp.sum(jnp.where(mk, wd, 0.0), axis=1, keepdims=True), w8)
        p8 = jnp.where(put, jnp.sum(jnp.where(mk, pos, 0.0), axis=1, keepdims=True), p8)
    e8_ref[...] = e8.astype(I32)
    w8_ref[...] = w8
    p8_ref[...] = p8.astype(I32)


def _route(logits, router_bias_pad):
    T = logits.shape[0]
    tm = 512
    row = pl.BlockSpec((tm, LANES), lambda i: (i, 0))
    return pl.pallas_call(
        functools.partial(_route_kernel, tm=tm),
        out_shape=(jax.ShapeDtypeStruct((T, LANES), I32), jax.ShapeDtypeStruct((T, LANES), F32),
                   jax.ShapeDtypeStruct((T, LANES), I32), jax.ShapeDtypeStruct((SUBLANES, LANES), F32)),
        grid=(T // tm,),
        in_specs=[row, pl.BlockSpec((1, LANES), lambda i: (0, 0))],
        out_specs=(row, row, row, pl.BlockSpec((SUBLANES, LANES), lambda i: (0, 0))),
        scratch_shapes=[pltpu.VMEM((SUBLANES, LANES), F32)],
        compiler_params=_cparams(1, 32),
        name="route",
    )(logits, router_bias_pad)


def _dest_kernel(e8_ref, p8_ref, start_ref, d_ref):
    e8 = e8_ref[...]
    lane = lax.broadcasted_iota(I32, e8.shape, 1)
    starts = start_ref[...]
    dest = p8_ref[...]
    for k in range(TOP_K):
        hit = lane == e8[:, k:k + 1]
        base = jnp.sum(jnp.where(hit, starts, 0.0), axis=1, keepdims=True).astype(I32)
        dest = jnp.where(lane == k, dest + base, dest)
    d_ref[...] = dest


def _dest_rows(e8, p8, pad_start_lanes):
    T = e8.shape[0]
    tm = 1024
    row = pl.BlockSpec((tm, LANES), lambda i: (i, 0))
    return pl.pallas_call(
        _dest_kernel, out_shape=jax.ShapeDtypeStruct((T, LANES), I32), grid=(T // tm,),
        in_specs=[row, row, pl.BlockSpec((1, LANES), lambda i: (0, 0))], out_specs=row,
        compiler_params=_cparams(1, 32), name="dest_rows",
    )(e8, p8, pad_start_lanes)


def _token_copy(src, src_token, dst, dst_token, sem, sub):
    return pltpu.make_async_copy(src.at[pl.ds(pl.multiple_of(src_token * sub, sub), sub), :],
                                 dst.at[pl.ds(pl.multiple_of(dst_token * sub, sub), sub), :], sem)


def _dispatch_kernel(pad_end_ref, padded_ref, dest_ref, hp_ref, xs_ref, zero_ref, sem_ref, *, tm, rows, n_blocks, sub):
    @pl.when(pl.program_id(0) == 0)
    def _():
        zero_ref[...] = jnp.zeros_like(zero_ref)

        def fill(e, do_wait):
            @pl.when(padded_ref[e] > 0)
            def _():
                start = pl.multiple_of((pad_end_ref[e] - rows) * sub, rows * sub)
                cp = pltpu.make_async_copy(zero_ref, xs_ref.at[pl.ds(start, rows * sub), :], sem_ref)
                if do_wait:
                    cp.wait()
                else:
                    cp.start()

        lax.fori_loop(0, N_EXPERTS, lambda e, c: (fill(e, False), c)[1], 0)
        lax.fori_loop(0, N_EXPERTS, lambda e, c: (fill(e, True), c)[1], 0)

        def tail(b):
            return pltpu.make_async_copy(zero_ref, xs_ref.at[pl.ds(pl.multiple_of(b * rows * sub, rows * sub), rows * sub), :],
                                         sem_ref)

        n_used = pad_end_ref[N_EXPERTS - 1] // rows
        lax.fori_loop(n_used, n_blocks, lambda b, c: (tail(b).start(), c)[1], 0)
        lax.fori_loop(n_used, n_blocks, lambda b, c: (tail(b).wait(), c)[1], 0)

    def issue(t, c):
        for k in range(TOP_K):
            _token_copy(hp_ref, t, xs_ref, dest_ref[t * TOP_K + k], sem_ref, sub).start(priority=k % 2)
        return c

    def drain(t, c):
        for k in range(TOP_K):
            _token_copy(hp_ref, t, xs_ref, dest_ref[t * TOP_K + k], sem_ref, sub).wait()
        return c

    lax.fori_loop(0, tm, issue, 0)
    lax.fori_loop(0, tm, drain, 0)


def _dispatch(hp, dest_flat, pad_end, padded, n_rows, sub):
    tm = TOKEN_TILE
    T = hp.shape[0] // sub
    grid_spec = pltpu.PrefetchScalarGridSpec(
        num_scalar_prefetch=2,
        grid=(T // tm,),
        in_specs=[pl.BlockSpec((tm * TOP_K,), lambda i, pe, pd: (i,), memory_space=pltpu.SMEM),
                  pl.BlockSpec((tm * sub, LANES), lambda i, pe, pd: (i, 0))],
        out_specs=pl.BlockSpec(memory_space=pl.ANY),
        scratch_shapes=[pltpu.VMEM((EXPERT_ROWS * sub, LANES), U32), pltpu.SemaphoreType.DMA(())],
    )
    return pl.pallas_call(
        functools.partial(_dispatch_kernel, tm=tm, rows=EXPERT_ROWS, n_blocks=n_rows // EXPERT_ROWS, sub=sub),
        out_shape=jax.ShapeDtypeStruct((n_rows * sub, LANES), U32),
        grid_spec=grid_spec,
        compiler_params=_cparams(1, 32),
        name="dispatch",
    )(pad_end, padded, dest_flat, hp)


def _expert_kernel(blk_e_ref, next_e_ref, n_used_ref, x_ref, wg_hbm, wu_hbm, wd_hbm, y_ref,
                   wg_f32, wu_f32, wd_f32, wg_bf, wu_bf, wd_bf, sem_ref, slot_ref, *, layer, rows, width):
    i = pl.program_id(0)
    e = blk_e_ref[i]
    active = i < n_used_ref[0]
    first = jnp.logical_or(i == 0, e != blk_e_ref[jnp.maximum(i - 1, 0)])

    def fetch(expert, slot):
        return [pltpu.make_async_copy(w.at[layer, expert], buf.at[slot], sem_ref.at[slot])
                for w, buf in ((wg_hbm, wg_f32), (wu_hbm, wu_f32), (wd_hbm, wd_f32))]

    @pl.when(i == 0)
    def _():
        slot_ref[0] = 0
        for cp in fetch(e, 0):
            cp.start()

    @pl.when(jnp.logical_and(active, first))
    def _():
        slot = slot_ref[0]
        for cp in fetch(e, slot):
            cp.wait()
        nxt = next_e_ref[i]

        @pl.when(nxt >= 0)
        def _():
            for cp in fetch(nxt, 1 - slot):
                cp.start()

        wg_bf[...] = wg_f32[slot].astype(BF16)
        wu_bf[...] = wu_f32[slot].astype(BF16)
        wd_bf[...] = wd_f32[slot].astype(BF16)
        slot_ref[0] = 1 - slot

    @pl.when(active)
    def _():
        x = _unpack_pairs(_load_token_rows(x_ref, 0, rows, width))
        g = _dot(x, wg_bf[...])
        u = _dot(x, wu_bf[...])
        hb = (g * _sigmoid(g)) * u
        _store_token_rows(y_ref, _pack_pairs(_dot(hb.astype(BF16), wd_bf[...])))

    @pl.when(jnp.logical_not(active))
    def _():
        y_ref[...] = jnp.zeros_like(y_ref)


def _experts(xs, blk_e, next_e, n_used, w_gate, w_up, w_down, layer):
    _, E, D, Hx = w_gate.shape
    W = D // 2
    sub = _rows_per_token(W)
    M = EXPERT_ROWS
    nblk = xs.shape[0] // (M * sub)
    xrow = lambda i, be, ne, nu: (jnp.minimum(i, nu[0] - 1), 0)
    hbm = pl.BlockSpec(memory_space=pl.ANY)
    grid_spec = pltpu.PrefetchScalarGridSpec(
        num_scalar_prefetch=3,
        grid=(nblk,),
        in_specs=[pl.BlockSpec((M * sub, LANES), xrow), hbm, hbm, hbm],
        out_specs=pl.BlockSpec((M * sub, LANES), lambda i, be, ne, nu: (i, 0)),
        scratch_shapes=[pltpu.VMEM((2, D, Hx), F32), pltpu.VMEM((2, D, Hx), F32), pltpu.VMEM((2, Hx, D), F32),
                        pltpu.VMEM((D, Hx), BF16), pltpu.VMEM((D, Hx), BF16), pltpu.VMEM((Hx, D), BF16),
                        pltpu.SemaphoreType.DMA((2,)), pltpu.SMEM((1,), I32)],
    )
    return pl.pallas_call(
        functools.partial(_expert_kernel, layer=layer, rows=M, width=W),
        out_shape=jax.ShapeDtypeStruct(xs.shape, U32),
        grid_spec=grid_spec,
        compiler_params=_cparams(1, 52),
        name="experts",
    )(blk_e, next_e, n_used, xs, w_gate, w_up, w_down)


def _next_expert(blk_e, cnt):
    E = cnt.shape[0]
    idx = jnp.where(cnt > 0, jnp.arange(E, dtype=I32), E)
    later = jnp.concatenate([lax.cummin(idx, axis=0, reverse=True)[1:], jnp.full((1,), E, I32)])
    return jnp.where(later < E, later, -1)[blk_e]


def _unpack_pairs_f32(w):
    a = lax.bitcast_convert_type(w & jnp.uint32(0xFFFF0000), F32)
    b = lax.bitcast_convert_type(w << 16, F32)
    return jnp.concatenate([a, b], axis=1)


def _combine_kernel(dcur_ref, dnext_ref, x_ref, ysh_ref, w8_ref, g2_ref, fn_ref, ys_ref, o_ref, buf_ref, sem_ref,
                    *, tm, n_steps, width, final_norm):
    i = pl.program_id(0)
    slot = lax.rem(i, 2)
    sub = _rows_per_token(width)

    def issue(dref, s):
        def body(t, c):
            for k in range(TOP_K):
                _token_copy(ys_ref, dref[t * TOP_K + k], buf_ref.at[s], k * tm + t, sem_ref.at[s], sub).start(priority=k % 2)
            return c
        lax.fori_loop(0, tm, body, 0)

    @pl.when(i == 0)
    def _():
        issue(dcur_ref, 0)

    @pl.when(i + 1 < n_steps)
    def _():
        issue(dnext_ref, 1 - slot)

    def drain(t, c):
        for k in range(TOP_K):
            _token_copy(ys_ref, 0, buf_ref.at[slot], 0, sem_ref.at[slot], sub).wait()
        return c

    lax.fori_loop(0, tm, drain, 0)
    w8 = w8_ref[...]
    y = _unpack_pairs_f32(_load_token_rows(ysh_ref, 0, tm, width))
    gathered = buf_ref.at[slot]
    for k in range(TOP_K):
        y = y + w8[:, k:k + 1] * _unpack_pairs_f32(_load_token_rows(gathered, k * tm, tm, width))
    out = x_ref[...] + g2_ref[...] * y
    if final_norm:
        out = out * lax.rsqrt(jnp.mean(out * out, axis=-1, keepdims=True) + RMS_EPS) * fn_ref[...]
    o_ref[...] = out


def _combine(x, ysh, ys, dest_flat, w8, gate2, final_g, S, final_norm):
    T, D = x.shape
    W = D // 2
    sub = _rows_per_token(W)
    tm = TOKEN_TILE
    nb = S // tm
    n_steps = T // tm
    row = lambda i: (i, 0)
    return pl.pallas_call(
        functools.partial(_combine_kernel, tm=tm, n_steps=n_steps, width=W, final_norm=final_norm),
        out_shape=jax.ShapeDtypeStruct((T, D), F32),
        grid=(n_steps,),
        in_specs=[pl.BlockSpec((tm * TOP_K,), lambda i: (i,), memory_space=pltpu.SMEM),
                  pl.BlockSpec((tm * TOP_K,), lambda i: (jnp.minimum(i + 1, n_steps - 1),), memory_space=pltpu.SMEM),
                  pl.BlockSpec((tm, D), row), pl.BlockSpec((tm * sub, LANES), row),
                  pl.BlockSpec((tm, LANES), row),
                  pl.BlockSpec((None, 1, D), lambda i: (i // nb, 0, 0)),
                  pl.BlockSpec((1, D), lambda i: (0, 0)),
                  pl.BlockSpec(memory_space=pl.ANY)],
        out_specs=pl.BlockSpec((tm, D), row),
        scratch_shapes=[pltpu.VMEM((2, TOP_K * tm * sub, LANES), U32), pltpu.SemaphoreType.DMA((2,))],
        compiler_params=_cparams(1, 32),
        name="combine",
    )(dest_flat, dest_flat, x, ysh, w8, gate2, final_g, ys)


def _pad_lanes(v, offset=0):
    out = jnp.zeros((1, LANES), F32)
    return out.at[0, offset:offset + v.shape[0]].set(v.astype(F32))


def kernel(x, c, positions, attn_norm, w_ada, b_ada, w_in, fox_bias, gla_w_gate, gla_b_gate, gla_norm,
           w_branch, w_out, ffn_norm, w_router, router_bias, w_exp_gate, w_exp_up, w_exp_down,
           w_sh_gate, w_sh_up, w_sh_down, final_norm):
    B, S, D = x.shape
    L = w_ada.shape[0]
    T = B * S
    E = N_EXPERTS
    M = EXPERT_ROWS
    xf = x.reshape(T, D)

    c_pad = jnp.zeros((SUBLANES, D), F32).at[:B].set(c)
    mod = _ada_mod(c_pad, w_ada, b_ada)

    half = ROPE_DIM // 2
    inv_freq = jnp.power(ROPE_THETA, -jnp.arange(half, dtype=F32) * 2.0 / ROPE_DIM)
    freq_lanes = _pad_lanes(jnp.concatenate([inv_freq, inv_freq]))
    cos_tab, sin_tab = _rope_tables(positions.reshape(T, 1), freq_lanes)

    w_seg_a, w_seg_b, w_seg_c, w_misc_t = _input_weights(w_in)
    w_branch_bf16 = w_branch.astype(BF16)
    n_blocks = (T * TOP_K + E * (M - 1) + M - 1) // M
    P = n_blocks * M

    for l in range(L):
        m6 = mod[l, :B].reshape(B, 6, 1, D)
        shift1, scale1, gate1, shift2, scale2, gate2 = (m6[:, n] for n in range(6))

        h1 = _norm_mod(xf, attn_norm[l].reshape(1, D), scale1, shift1, S)
        proj_a = _nt_matmul(h1, w_seg_a, l, tm=1024, tn=1536, out_dtype=BF16)
        proj_b = _nt_matmul(h1, w_seg_b, l, tm=1024, tn=1536, out_dtype=BF16)
        proj_c = _nt_matmul(h1, w_seg_c, l, tm=1024, tn=2048, out_dtype=BF16)
        misc = _nt_matmul(h1, w_misc_t, l, tm=1024, tn=LANES, out_dtype=F32)

        o_a = _sb_attention(proj_a, B, S, COL_SB)
        qa, ka, va = _fox_prep(proj_a, misc, _pad_lanes(fox_bias[l], MISC_FOX_F), B, S)
        o_b = _flash_attention(qa, ka, va, B, S)
        qa, ka, va = _moba_prep(proj_b, cos_tab, sin_tab, B, S)
        o_c = _flash_attention(qa, ka, va, B, S)
        wg_pad = jnp.zeros((LANES, gla_w_gate.shape[2]), F32).at[MISC_GLR:MISC_GLR + GLA_GATE_RANK].set(gla_w_gate[l])
        o_d = _gla(proj_b, misc, wg_pad, gla_b_gate[l].reshape(1, -1), gla_norm[l].reshape(1, -1), B, S)

        merged = _merge((o_a, o_b, o_c, o_d), proj_c, w_branch_bf16, l)
        xf = _matmul(merged, w_out, tm=512, tn=1024, residual=(xf, gate1), S=S, layer=l)

        wr_pad = jnp.zeros((D, LANES), F32).at[:, :E].set(w_router[l])
        hp, logits = _norm_mod(xf, ffn_norm[l].reshape(1, D), scale2, shift2, S, w_router_pad=wr_pad)
        e8, w8, p8, counts = _route(logits, _pad_lanes(router_bias[l]))
        cnt = counts[0, :E].astype(I32)
        padded = (cnt + M - 1) // M * M
        pad_end = jnp.cumsum(padded)
        pad_start = pad_end - padded
        dest = _dest_rows(e8, p8, _pad_lanes(pad_start))[:, :TOP_K].reshape(T * TOP_K)
        blk_start = jnp.arange(n_blocks, dtype=I32) * M
        blk_e = jnp.minimum(jnp.sum((pad_end[None, :] <= blk_start[:, None]).astype(I32), axis=1), E - 1)
        n_used = (pad_end[E - 1:] // M).astype(I32)

        xs = _dispatch(hp, dest, pad_end.astype(I32), padded.astype(I32), P, _rows_per_token(D // 2))
        ys = _experts(xs, blk_e, _next_expert(blk_e, cnt), n_used, w_exp_gate, w_exp_up, w_exp_down, l)
        ysh = _experts(hp, jnp.zeros((T // M,), I32), jnp.full((T // M,), -1, I32), jnp.full((1,), T // M, I32),
                       w_sh_gate[:, None], w_sh_up[:, None], w_sh_down[:, None], l)
        last = l == L - 1
        xf = _combine(xf, ysh, ys, dest, w8, gate2, final_norm.reshape(1, D), S, final_norm=last)

    return xf.reshape(B, S, D)
```

```python
import functools

import jax
import jax.numpy as jnp
from jax import lax
from jax.experimental import pallas as pl
from jax.experimental.pallas import tpu as pltpu

F32 = jnp.float32
BF16 = jnp.bfloat16
I32 = jnp.int32
U32 = jnp.uint32

HEAD_DIM = 128
N_HEADS = 4
BRANCH_WIDTH = N_HEADS * HEAD_DIM
GLA_HEAD_K = 64
GLA_CHUNK = 64
GLA_GATE_RANK = 16
GLA_GATE_NORMALIZER = 16.0
MOBA_BLOCK = 256
MOBA_TOPK = 3
ROPE_THETA = 500000.0
ROPE_DIM = HEAD_DIM // 4
N_EXPERTS = 64
GROUP_SIZE = 8
TOPK_GROUPS = 4
TOP_K = 8
ROUTED_SCALE = 2.5
RMS_EPS = 1e-6

LANES = 128
SUBLANES = 8
MIB = 1024 * 1024
NEG_BIG = -1e30

ATT_BLOCK = 256
FLASH_Q_BLOCK = 1024
EXPERT_ROWS = 256
TOKEN_TILE = 128

COL_SB, COL_FOX, COL_MOBA = 0, 12, 24
COL_GQ, COL_GK, COL_GV, COL_GR, COL_GATES = 36, 38, 40, 44, 48
SEG_B, SEG_C = COL_MOBA, COL_GATES
MISC_FOX_F, MISC_GLR = 0, 4


def _cparams(n_axes, vmem_mib):
    return pltpu.CompilerParams(dimension_semantics=("arbitrary",) * n_axes,
                                vmem_limit_bytes=vmem_mib * MIB)


def _softplus(z):
    return jnp.maximum(z, 0.0) + jnp.log(1.0 + jnp.exp(-jnp.abs(z)))


def _log_sigmoid(z):
    return -_softplus(-z)


def _sigmoid(z):
    return 1.0 / (1.0 + jnp.exp(-z))


def _dot(a, b, precision=None):
    return jnp.dot(a, b, preferred_element_type=F32, precision=precision)


def _dot_nt(a, b, precision=None):
    return lax.dot_general(a, b, (((1,), (1,)), ((), ())), preferred_element_type=F32,
                           precision=precision)


def _dot_tn(a, b, precision=None):
    return lax.dot_general(a, b, (((0,), (0,)), ((), ())), preferred_element_type=F32,
                           precision=precision)


HIGHEST = lax.Precision.HIGHEST


def _split3(x):
    hi = x.astype(BF16)
    r1 = x - hi.astype(F32)
    mid = r1.astype(BF16)
    lo = (r1 - mid.astype(F32)).astype(BF16)
    return hi, mid, lo


def _ada_kernel(c_ref, w_ref, b_ref, o_ref):
    c = c_ref[...]
    c_act = (c * _sigmoid(c)).astype(BF16)
    o_ref[...] = _dot(c_act, w_ref[...].astype(BF16)) + b_ref[...]


def _ada_mod(c_pad, w_ada, b_ada):
    L, D, N = w_ada.shape
    tn = 1536
    return pl.pallas_call(
        _ada_kernel,
        out_shape=jax.ShapeDtypeStruct((L, SUBLANES, N), F32),
        grid=(L, N // tn),
        in_specs=[pl.BlockSpec((SUBLANES, D), lambda l, j: (0, 0)),
                  pl.BlockSpec((None, D, tn), lambda l, j: (l, 0, j)),
                  pl.BlockSpec((None, 1, tn), lambda l, j: (l, 0, j))],
        out_specs=pl.BlockSpec((None, SUBLANES, tn), lambda l, j: (l, 0, j)),
        compiler_params=_cparams(2, 40),
        name="ada_mod",
    )(c_pad, w_ada, b_ada.reshape(L, 1, N))


def _normed(x, g, scale, shift):
    y = x * lax.rsqrt(jnp.mean(x * x, axis=-1, keepdims=True) + RMS_EPS)
    return (y * g) * (1.0 + scale) + shift


def _norm_mod_kernel(x_ref, g_ref, sc_ref, sh_ref, h_ref):
    h_ref[...] = _normed(x_ref[...], g_ref[...], sc_ref[...], sh_ref[...]).astype(BF16)


def _pack_pairs(h):
    half = h.shape[1] // 2
    hi = lax.bitcast_convert_type(h[:, :half].astype(BF16).astype(F32), U32)
    lo = lax.bitcast_convert_type(h[:, half:].astype(BF16).astype(F32), U32)
    return hi | (lo >> 16)


def _unpack_pairs(w):
    a = lax.bitcast_convert_type(w & jnp.uint32(0xFFFF0000), F32).astype(BF16)
    b = lax.bitcast_convert_type(w << 16, F32).astype(BF16)
    return jnp.concatenate([a, b], axis=1)


def _rows_per_token(width):
    return width // LANES


def _store_token_rows(ref, val):
    n, width = val.shape
    sub = _rows_per_token(width)
    for s in range(sub):
        ref[pl.ds(s, n, stride=sub), :] = val[:, s * LANES:(s + 1) * LANES]


def _load_token_rows(ref, first_token, n, width):
    sub = _rows_per_token(width)
    return jnp.concatenate([ref[pl.ds(first_token * sub + s, n, stride=sub), :] for s in range(sub)], axis=1)


def _norm_route_kernel(x_ref, g_ref, sc_ref, sh_ref, wr_ref, hp_ref, lg_ref):
    h = _normed(x_ref[...], g_ref[...], sc_ref[...], sh_ref[...])
    _store_token_rows(hp_ref, _pack_pairs(h))
    lg_ref[...] = _dot(h, wr_ref[...], precision=HIGHEST)


def _norm_mod(x, g, scale, shift, S, w_router_pad=None):
    T, D = x.shape
    tm = 512
    nb = S // tm
    row = lambda i: (i, 0)
    per_batch = pl.BlockSpec((None, 1, D), lambda i: (i // nb, 0, 0))
    in_specs = [pl.BlockSpec((tm, D), row), pl.BlockSpec((1, D), lambda i: (0, 0)), per_batch, per_batch]
    if w_router_pad is None:
        return pl.pallas_call(
            _norm_mod_kernel, out_shape=jax.ShapeDtypeStruct((T, D), BF16), grid=(T // tm,),
            in_specs=in_specs, out_specs=pl.BlockSpec((tm, D), row),
            compiler_params=_cparams(1, 32), name="norm_mod",
        )(x, g, scale, shift)
    return pl.pallas_call(
        _norm_route_kernel,
        out_shape=(jax.ShapeDtypeStruct((T * (D // 2) // LANES, LANES), U32), jax.ShapeDtypeStruct((T, LANES), F32)),
        grid=(T // tm,),
        in_specs=in_specs + [pl.BlockSpec((D, LANES), lambda i: (0, 0))],
        out_specs=(pl.BlockSpec((tm * (D // 2) // LANES, LANES), row), pl.BlockSpec((tm, LANES), row)),
        compiler_params=_cparams(1, 32), name="norm_route",
    )(x, g, scale, shift, w_router_pad)


def _mm_kernel(*refs, cast_w, residual):
    if residual:
        a_ref, w_ref, x_ref, g_ref, o_ref = refs[:5]
        scratch = refs[5:]
    else:
        a_ref, w_ref, o_ref = refs[:3]
        scratch = refs[3:]
    if cast_w:
        wbf_ref, = scratch

        @pl.when(pl.program_id(1) == 0)
        def _():
            wbf_ref[...] = w_ref[...].astype(BF16)

        w = wbf_ref[...]
    else:
        w = w_ref[...]
    acc = _dot(a_ref[...], w)
    if residual:
        acc = x_ref[...] + g_ref[...] * acc
    o_ref[...] = acc.astype(o_ref.dtype)


def _matmul(a, w, tm, tn, out_dtype=F32, residual=None, S=None, vmem_mib=48, layer=None):
    M, K = a.shape
    N = w.shape[-1]
    cast_w = w.dtype != BF16
    if layer is None:
        w_spec = pl.BlockSpec((K, tn), lambda j, i: (0, j))
    else:
        w_spec = pl.BlockSpec((None, K, tn), lambda j, i: (layer, 0, j))
    in_specs = [pl.BlockSpec((tm, K), lambda j, i: (i, 0)), w_spec]
    args = [a, w]
    if residual is not None:
        x, gate = residual
        nb = S // tm
        in_specs += [pl.BlockSpec((tm, tn), lambda j, i: (i, j)),
                     pl.BlockSpec((None, 1, tn), lambda j, i: (i // nb, 0, j))]
        args += [x, gate]
    return pl.pallas_call(
        functools.partial(_mm_kernel, cast_w=cast_w, residual=residual is not None),
        out_shape=jax.ShapeDtypeStruct((M, N), out_dtype),
        grid=(N // tn, M // tm),
        in_specs=in_specs,
        out_specs=pl.BlockSpec((tm, tn), lambda j, i: (i, j)),
        scratch_shapes=[pltpu.VMEM((K, tn), BF16)] if cast_w else [],
        compiler_params=_cparams(2, vmem_mib),
        name="matmul_res" if residual is not None else "matmul",
    )(*args)


def _nt_matmul_kernel(a_ref, w_ref, o_ref):
    o_ref[...] = _dot_nt(a_ref[...], w_ref[...]).astype(o_ref.dtype)


def _nt_matmul(a, wt, layer, tm, tn, out_dtype):
    M, K = a.shape
    N = wt.shape[1]
    return pl.pallas_call(
        _nt_matmul_kernel,
        out_shape=jax.ShapeDtypeStruct((M, N), out_dtype),
        grid=(N // tn, M // tm),
        in_specs=[pl.BlockSpec((tm, K), lambda j, i: (i, 0)),
                  pl.BlockSpec((None, tn, K), lambda j, i: (layer, j, 0))],
        out_specs=pl.BlockSpec((tm, tn), lambda j, i: (i, j)),
        compiler_params=_cparams(2, 48),
        name="in_proj",
    )(a, wt)


def _input_weights(w_in):
    c_fox_f = COL_MOBA * LANES
    c_glr = COL_GATES * LANES + N_HEADS
    wt = jnp.swapaxes(w_in, 1, 2)
    seg_a = wt[:, :c_fox_f].astype(BF16)
    seg_b = wt[:, c_fox_f + N_HEADS:c_glr].astype(BF16)
    seg_c = wt[:, c_glr + GLA_GATE_RANK:].astype(BF16)
    pad = jnp.zeros((wt.shape[0], LANES - N_HEADS - GLA_GATE_RANK, wt.shape[2]), wt.dtype)
    misc = jnp.concatenate([wt[:, c_fox_f:c_fox_f + N_HEADS], wt[:, c_glr:c_glr + GLA_GATE_RANK], pad],
                           axis=1).astype(BF16)
    return seg_a, seg_b, seg_c, misc


SB_LOG_WEIGHT_FLOOR = -110.0


def _sb_kernel(q_ref, k_ref, v_ref, o_ref, *, blk, scale):
    i = pl.program_id(1)
    row = lax.broadcasted_iota(I32, (blk, blk), 0)
    col = lax.broadcasted_iota(I32, (blk, blk), 1)
    later = (row > col).astype(BF16)

    def cond(state):
        jj, alive, _ = state
        return jnp.logical_and(jj <= i, alive)

    def body(state):
        jj, _, heads = state
        j = i - jj
        start = pl.multiple_of(j * blk, blk)
        past = jnp.logical_or(j < i, col < row)
        new, top = [], None
        for h in range(N_HEADS):
            c, acc = heads[h]
            hs = slice(h * HEAD_DIM, (h + 1) * HEAD_DIM)
            q = (q_ref[:, hs].astype(F32) * scale).astype(BF16)
            k = k_ref[pl.ds(start, blk), hs]
            v = v_ref[pl.ds(start, blk), hs]
            z = _dot_nt(q, k)
            ls = jnp.where(past, -_softplus(z), 0.0)
            hi = ls.astype(BF16)
            lo = (ls - hi.astype(F32)).astype(BF16)
            between = _dot(hi, later) + _dot(lo, later)
            w = jnp.where(past, jnp.exp(z + ls + between + c), 0.0)
            acc = acc + _dot(w.astype(BF16), v)
            c = c + jnp.sum(ls, axis=1, keepdims=True)
            new.append((c, acc))
            top = c if top is None else jnp.maximum(top, c)
        return jj + 1, jnp.max(top) > SB_LOG_WEIGHT_FLOOR, tuple(new)

    init = tuple((jnp.zeros((blk, 1), F32), jnp.zeros((blk, HEAD_DIM), F32)) for _ in range(N_HEADS))
    _, _, heads = lax.while_loop(cond, body, (jnp.int32(0), jnp.bool_(True), init))
    for h in range(N_HEADS):
        o_ref[:, h * HEAD_DIM:(h + 1) * HEAD_DIM] = heads[h][1]


def _sb_attention(proj, B, S, col0):
    T = proj.shape[0]
    blk = ATT_BLOCK
    nq = S // blk
    W = BRANCH_WIDTH
    cb = col0 * LANES // W
    return pl.pallas_call(
        functools.partial(_sb_kernel, blk=blk, scale=HEAD_DIM ** -0.5),
        out_shape=jax.ShapeDtypeStruct((T, W), F32),
        grid=(B, nq),
        in_specs=[pl.BlockSpec((blk, W), lambda b, i: (b * nq + i, cb)),
                  pl.BlockSpec((S, W), lambda b, i: (b, cb + 1)),
                  pl.BlockSpec((S, W), lambda b, i: (b, cb + 2))],
        out_specs=pl.BlockSpec((blk, W), lambda b, i: (b * nq + i, 0)),
        compiler_params=_cparams(2, 44),
        name="sb_attention",
    )(proj, proj, proj)


def _flash_kernel(qa_ref, ka_ref, v_ref, o_ref, *, bq, bk):
    i = pl.program_id(1)
    A = 2 * HEAD_DIM
    ratio = bq // bk
    row = lax.broadcasted_iota(I32, (bq, bk), 0)
    col = lax.broadcasted_iota(I32, (bq, bk), 1)

    def step(j, carry, key_offset=None):
        start = pl.multiple_of(j * bk, bk)
        new = []
        for h in range(N_HEADS):
            m, l, acc = carry[h]
            s = _dot_nt(qa_ref[:, h * A:(h + 1) * A], ka_ref[pl.ds(start, bk), h * A:(h + 1) * A])
            if key_offset is not None:
                s = jnp.where(col + key_offset <= row, s, NEG_BIG)
            m_new = jnp.maximum(m, jnp.max(s, axis=1, keepdims=True))
            alpha = jnp.exp(m - m_new)
            p = jnp.exp(s - m_new)
            l = alpha * l + jnp.sum(p, axis=1, keepdims=True)
            acc = alpha * acc + _dot(p.astype(BF16), v_ref[pl.ds(start, bk), h * HEAD_DIM:(h + 1) * HEAD_DIM])
            new.append((m_new, l, acc))
        return tuple(new)

    init = tuple((jnp.full((bq, 1), NEG_BIG, F32), jnp.zeros((bq, 1), F32),
                  jnp.zeros((bq, HEAD_DIM), F32)) for _ in range(N_HEADS))
    heads = lax.fori_loop(0, i * ratio, step, init)
    for d in range(ratio):
        heads = step(i * ratio + d, heads, key_offset=d * bk)
    for h in range(N_HEADS):
        _, l, acc = heads[h]
        o_ref[:, h * HEAD_DIM:(h + 1) * HEAD_DIM] = acc / l


def _flash_attention(qa, ka, va, B, S):
    T = qa.shape[0]
    bq, bk = FLASH_Q_BLOCK, ATT_BLOCK
    nq = S // bq
    W = BRANCH_WIDTH
    return pl.pallas_call(
        functools.partial(_flash_kernel, bq=bq, bk=bk),
        out_shape=jax.ShapeDtypeStruct((T, W), F32),
        grid=(B, nq),
        in_specs=[pl.BlockSpec((bq, 2 * W), lambda b, i: (b * nq + i, 0)),
                  pl.BlockSpec((S, 2 * W), lambda b, i: (b, 0)),
                  pl.BlockSpec((S, W), lambda b, i: (b, 0))],
        out_specs=pl.BlockSpec((bq, W), lambda b, i: (b * nq + i, 0)),
        compiler_params=_cparams(2, 56),
        name="flash_attention",
    )(qa, ka, va)


def _fox_prep_kernel(q_ref, k_ref, v_ref, misc_ref, bias_ref, qa_ref, ka_ref, va_ref, carry_ref, *, tb, scale):
    va_ref[...] = v_ref[...].astype(BF16)
    @pl.when(pl.program_id(1) == 0)
    def _():
        carry_ref[...] = jnp.zeros_like(carry_ref)

    lane = lax.broadcasted_iota(I32, (tb, LANES), 1)
    lf = jnp.where(lane < N_HEADS, _log_sigmoid(misc_ref[...] + bias_ref[...]), 0.0)
    row = lax.broadcasted_iota(I32, (tb, tb), 0)
    col = lax.broadcasted_iota(I32, (tb, tb), 1)
    incl = (col <= row).astype(F32)
    F = _dot(incl, lf, precision=HIGHEST) + carry_ref[0:1, :]
    carry_ref[0:1, :] = F[tb - 1:tb, :]
    for h in range(N_HEADS):
        Fh = jnp.broadcast_to(F[:, h:h + 1], (tb, LANES))
        hi, mid, lo = (p.astype(F32) for p in _split3(Fh))
        ones = jnp.ones((tb, LANES), F32)
        zeros = jnp.zeros((tb, LANES), F32)
        eq = jnp.where(lane == 0, hi, jnp.where(lane == 1, mid, jnp.where(lane == 2, lo,
             jnp.where(lane < 6, ones, zeros))))
        ek = jnp.where(lane < 3, ones, jnp.where(lane == 3, -hi, jnp.where(lane == 4, -mid,
             jnp.where(lane == 5, -lo, zeros))))
        hs = slice(h * HEAD_DIM, (h + 1) * HEAD_DIM)
        qa_ref[:, 2 * h * HEAD_DIM:(2 * h + 1) * HEAD_DIM] = (q_ref[:, hs].astype(F32) * scale).astype(BF16)
        qa_ref[:, (2 * h + 1) * HEAD_DIM:(2 * h + 2) * HEAD_DIM] = eq.astype(BF16)
        ka_ref[:, 2 * h * HEAD_DIM:(2 * h + 1) * HEAD_DIM] = k_ref[:, hs].astype(BF16)
        ka_ref[:, (2 * h + 1) * HEAD_DIM:(2 * h + 2) * HEAD_DIM] = ek.astype(BF16)


def _fox_prep(proj, misc, fox_bias_pad, B, S):
    T = proj.shape[0]
    tb = 512
    nb = S // tb
    qblk = COL_FOX * LANES // BRANCH_WIDTH
    out = jax.ShapeDtypeStruct((T, 2 * BRANCH_WIDTH), BF16)
    return pl.pallas_call(
        functools.partial(_fox_prep_kernel, tb=tb, scale=HEAD_DIM ** -0.5),
        out_shape=(out, out, jax.ShapeDtypeStruct((T, BRANCH_WIDTH), BF16)),
        grid=(B, nb),
        in_specs=[pl.BlockSpec((tb, BRANCH_WIDTH), lambda b, i: (b * nb + i, qblk)),
                  pl.BlockSpec((tb, BRANCH_WIDTH), lambda b, i: (b * nb + i, qblk + 1)),
                  pl.BlockSpec((tb, BRANCH_WIDTH), lambda b, i: (b * nb + i, qblk + 2)),
                  pl.BlockSpec((tb, LANES), lambda b, i: (b * nb + i, 0)),
                  pl.BlockSpec((1, LANES), lambda b, i: (0, 0))],
        out_specs=(pl.BlockSpec((tb, 2 * BRANCH_WIDTH), lambda b, i: (b * nb + i, 0)),
                   pl.BlockSpec((tb, 2 * BRANCH_WIDTH), lambda b, i: (b * nb + i, 0)),
                   pl.BlockSpec((tb, BRANCH_WIDTH), lambda b, i: (b * nb + i, 0))),
        scratch_shapes=[pltpu.VMEM((SUBLANES, LANES), F32)],
        compiler_params=_cparams(2, 32),
        name="fox_prep",
    )(proj, proj, proj, misc, fox_bias_pad)


def _rope(x, cos, sin, lane):
    half = ROPE_DIM // 2
    up = pltpu.roll(x, half, 1)
    down = pltpu.roll(x, LANES - half, 1)
    rot = jnp.where(lane < half, -down * sin, jnp.where(lane < ROPE_DIM, up * sin, 0.0))
    return x * jnp.where(lane < ROPE_DIM, cos, 1.0) + rot


def _rope_table_kernel(pos_ref, freq_ref, cos_ref, sin_ref):
    ang = pos_ref[...].astype(F32) * freq_ref[...]
    cos_ref[...] = jnp.cos(ang)
    sin_ref[...] = jnp.sin(ang)


def _rope_tables(pos_col, freq_lanes):
    T = pos_col.shape[0]
    tb = 1024
    row = lambda i: (i, 0)
    out = jax.ShapeDtypeStruct((T, LANES), F32)
    return pl.pallas_call(
        _rope_table_kernel, out_shape=(out, out), grid=(T // tb,),
        in_specs=[pl.BlockSpec((tb, 1), row), pl.BlockSpec((1, LANES), lambda i: (0, 0))],
        out_specs=(pl.BlockSpec((tb, LANES), row), pl.BlockSpec((tb, LANES), row)),
        compiler_params=_cparams(1, 32), name="rope_tables",
    )(pos_col, freq_lanes)


def _moba_k_kernel(k_ref, v_ref, cos_ref, sin_ref, ka_ref, va_ref, kmean_ref, *, tb):
    i = pl.program_id(1)
    va_ref[...] = v_ref[...].astype(BF16)

    @pl.when(i == 0)
    def _():
        kmean_ref[...] = jnp.zeros_like(kmean_ref)

    lane = lax.broadcasted_iota(I32, (tb, LANES), 1)
    cos, sin = cos_ref[...], sin_ref[...]
    onehot = jnp.where(lane == i, 1.0, 0.0).astype(BF16)
    this_row = lax.broadcasted_iota(I32, (LANES, HEAD_DIM), 0) == i
    for h in range(N_HEADS):
        hs = slice(h * HEAD_DIM, (h + 1) * HEAD_DIM)
        kr = _rope(k_ref[:, hs].astype(F32), cos, sin, lane)
        mean = jnp.sum(kr, axis=0, keepdims=True) * (1.0 / tb)
        kmean_ref[:, hs] = jnp.where(this_row, mean, kmean_ref[:, hs])
        ka_ref[:, 2 * h * HEAD_DIM:(2 * h + 1) * HEAD_DIM] = kr.astype(BF16)
        ka_ref[:, (2 * h + 1) * HEAD_DIM:(2 * h + 2) * HEAD_DIM] = onehot


def _moba_q_kernel(q_ref, cos_ref, sin_ref, kmean_ref, qa_ref, *, tb, scale):
    i = pl.program_id(1)
    lane = lax.broadcasted_iota(I32, (tb, LANES), 1)
    cos, sin = cos_ref[...], sin_ref[...]
    neg_inf = jnp.float32(-jnp.inf)
    for h in range(N_HEADS):
        hs = slice(h * HEAD_DIM, (h + 1) * HEAD_DIM)
        qr = _rope(q_ref[:, hs].astype(F32), cos, sin, lane)
        gate = _dot_nt(qr, kmean_ref[:, hs], precision=HIGHEST)
        cur = jnp.where(lane < i, gate, neg_inf)
        chosen = lane == i
        for _ in range(MOBA_TOPK):
            m = jnp.max(cur, axis=1, keepdims=True)
            first = jnp.min(jnp.where(jnp.logical_and(cur == m, m > neg_inf), lane, LANES),
                            axis=1, keepdims=True)
            pick = lane == first
            chosen = jnp.logical_or(chosen, pick)
            cur = jnp.where(pick, neg_inf, cur)
        bias = jnp.where(jnp.logical_or(chosen, lane >= LANES // 2), 0.0, NEG_BIG)
        qa_ref[:, 2 * h * HEAD_DIM:(2 * h + 1) * HEAD_DIM] = (qr * scale).astype(BF16)
        qa_ref[:, (2 * h + 1) * HEAD_DIM:(2 * h + 2) * HEAD_DIM] = bias.astype(BF16)


def _moba_prep(proj, cos_tab, sin_tab, B, S):
    T = proj.shape[0]
    tb = MOBA_BLOCK
    nb = S // tb
    qblk = (COL_MOBA - SEG_B) * LANES // BRANCH_WIDTH
    aug = jax.ShapeDtypeStruct((T, 2 * BRANCH_WIDTH), BF16)
    row_spec = lambda c: pl.BlockSpec((tb, BRANCH_WIDTH), lambda b, i: (b * nb + i, c))
    trig_spec = pl.BlockSpec((tb, LANES), lambda b, i: (b * nb + i, 0))
    aug_spec = pl.BlockSpec((tb, 2 * BRANCH_WIDTH), lambda b, i: (b * nb + i, 0))
    kmean_spec = pl.BlockSpec((None, LANES, BRANCH_WIDTH), lambda b, i: (b, 0, 0))
    ka, va, kmean = pl.pallas_call(
        functools.partial(_moba_k_kernel, tb=tb),
        out_shape=(aug, jax.ShapeDtypeStruct((T, BRANCH_WIDTH), BF16),
                   jax.ShapeDtypeStruct((B, LANES, BRANCH_WIDTH), F32)),
        grid=(B, nb),
        in_specs=[row_spec(qblk + 1), row_spec(qblk + 2), trig_spec, trig_spec],
        out_specs=(aug_spec, pl.BlockSpec((tb, BRANCH_WIDTH), lambda b, i: (b * nb + i, 0)), kmean_spec),
        compiler_params=_cparams(2, 32),
        name="moba_k_prep",
    )(proj, proj, cos_tab, sin_tab)
    qa = pl.pallas_call(
        functools.partial(_moba_q_kernel, tb=tb, scale=HEAD_DIM ** -0.5),
        out_shape=aug,
        grid=(B, nb),
        in_specs=[row_spec(qblk), trig_spec, trig_spec, kmean_spec],
        out_specs=aug_spec,
        compiler_params=_cparams(2, 32),
        name="moba_q_prep",
    )(proj, cos_tab, sin_tab, kmean)
    return qa, ka, va


def _gla_kernel(q_ref, k_ref, v_ref, gr_ref, misc_ref, wg_ref, bg_ref, gn_ref, o_ref, state_ref, *, tb):
    C = GLA_CHUNK
    @pl.when(pl.program_id(1) == 0)
    def _():
        state_ref[...] = jnp.zeros_like(state_ref)

    la_all = _log_sigmoid(_dot(misc_ref[...], wg_ref[...], precision=HIGHEST) + bg_ref[...]) \
        * (1.0 / GLA_GATE_NORMALIZER)
    rowc = lax.broadcasted_iota(I32, (C, C), 0)
    colc = lax.broadcasted_iota(I32, (C, C), 1)
    incl = (colc <= rowc).astype(F32)
    causal = colc <= rowc
    lane = lax.broadcasted_iota(I32, (C, LANES), 1)
    row2 = lax.broadcasted_iota(I32, (2 * GLA_HEAD_K, LANES), 0)
    ones_cv = jnp.ones((C, LANES), F32)
    qscale = GLA_HEAD_K ** -0.5
    gn = gn_ref[...]
    for c in range(tb // C):
        rs = slice(c * C, (c + 1) * C)
        for p in range(2):
            ps = slice(p * LANES, (p + 1) * LANES)
            la = la_all[rs, ps]
            b = _dot(incl, la, precision=HIGHEST)
            b_last = b[C - 1:C, :]
            eb = jnp.exp(b)
            q_dec = q_ref[rs, ps].astype(F32) * qscale * eb
            k = k_ref[rs, ps].astype(F32)
            k_inv = (k * jnp.exp(-b)).astype(BF16)
            k_end = (k * jnp.exp(b_last - b)).astype(BF16)
            decay = jnp.exp(_dot_tn(la, ones_cv, precision=HIGHEST))
            st = state_ref[p * LANES:(p + 1) * LANES, :]
            new_rows = []
            for e in range(2):
                h = 2 * p + e
                own = jnp.logical_and(lane >= e * GLA_HEAD_K, lane < (e + 1) * GLA_HEAD_K)
                qh = jnp.where(own, q_dec, 0.0).astype(BF16)
                v = v_ref[rs, h * HEAD_DIM:(h + 1) * HEAD_DIM].astype(BF16)
                attn = jnp.where(causal, _dot_nt(qh, k_inv), 0.0)
                st_h = jnp.where(jnp.logical_and(row2 >= e * GLA_HEAD_K, row2 < (e + 1) * GLA_HEAD_K), st, 0.0)
                o = _dot(attn.astype(BF16), v) + _dot(qh, st_h.astype(BF16))
                y = o * lax.rsqrt(jnp.mean(o * o, axis=-1, keepdims=True) + RMS_EPS) * gn
                g = gr_ref[rs, h * HEAD_DIM:(h + 1) * HEAD_DIM].astype(F32)
                o_ref[rs, h * HEAD_DIM:(h + 1) * HEAD_DIM] = y * (g * _sigmoid(g))
                new_rows.append(_dot_tn(k_end, v))
            kv = jnp.where(row2 < GLA_HEAD_K, new_rows[0], new_rows[1])
            state_ref[p * LANES:(p + 1) * LANES, :] = decay * st + kv


def _gla(proj, misc, w_gate_pad, b_gate, gla_norm, B, S):
    T = proj.shape[0]
    tb = 512
    nb = S // tb
    kd = 2 * LANES
    rows = lambda w, c: pl.BlockSpec((tb, w), lambda b, i: (b * nb + i, c))
    const = lambda shape: pl.BlockSpec(shape, lambda b, i: (0, 0))
    return pl.pallas_call(
        functools.partial(_gla_kernel, tb=tb),
        out_shape=jax.ShapeDtypeStruct((T, BRANCH_WIDTH), F32),
        grid=(B, nb),
        in_specs=[rows(kd, (COL_GQ - SEG_B) * LANES // kd), rows(kd, (COL_GK - SEG_B) * LANES // kd),
                  rows(BRANCH_WIDTH, (COL_GV - SEG_B) * LANES // BRANCH_WIDTH),
                  rows(BRANCH_WIDTH, (COL_GR - SEG_B) * LANES // BRANCH_WIDTH),
                  rows(LANES, 0), const((LANES, kd)), const((1, kd)), const((1, HEAD_DIM))],
        out_specs=rows(BRANCH_WIDTH, 0),
        scratch_shapes=[pltpu.VMEM((kd, HEAD_DIM), F32)],
        compiler_params=_cparams(2, 32),
        name="gla",
    )(proj, proj, proj, proj, misc, w_gate_pad, b_gate, gla_norm)


def _merge_kernel(oa_ref, ob_ref, oc_ref, od_ref, g0_ref, g1_ref, g2_ref, g3_ref, w_ref, o_ref):
    acc = None
    for n, (b_ref, g_ref) in enumerate(((oa_ref, g0_ref), (ob_ref, g1_ref), (oc_ref, g2_ref), (od_ref, g3_ref))):
        term = _sigmoid(g_ref[...].astype(F32)) * _dot(b_ref[...].astype(BF16), w_ref[n])
        acc = term if acc is None else acc + term
    o_ref[...] = acc.astype(BF16)


def _merge(branches, proj, w_branch_bf16, layer):
    T = proj.shape[0]
    _, nbr, W, D = w_branch_bf16.shape
    tm = 256
    g0 = (COL_GATES - SEG_C) * LANES // D
    br_spec = pl.BlockSpec((tm, W), lambda i: (i, 0))
    gate_spec = lambda n: pl.BlockSpec((tm, D), lambda i: (i, g0 + n))
    return pl.pallas_call(
        _merge_kernel,
        out_shape=jax.ShapeDtypeStruct((T, D), BF16),
        grid=(T // tm,),
        in_specs=[br_spec] * 4 + [gate_spec(n) for n in range(4)]
                 + [pl.BlockSpec((None, nbr, W, D), lambda i: (layer, 0, 0, 0))],
        out_specs=pl.BlockSpec((tm, D), lambda i: (i, 0)),
        compiler_params=_cparams(1, 44),
        name="branch_merge",
    )(*branches, proj, proj, proj, proj, w_branch_bf16)


def _butterfly(x, lane, op):
    for s in (1, 2, 4):
        up = pltpu.roll(x, s, 1)
        down = pltpu.roll(x, LANES - s, 1)
        x = op(x, jnp.where((lane & s) != 0, up, down))
    return x


def _route_kernel(lg_ref, bias_ref, e8_ref, w8_ref, p8_ref, cnt_ref, carry_ref, *, tm):
    @pl.when(pl.program_id(0) == 0)
    def _():
        carry_ref[...] = jnp.zeros_like(carry_ref)

    neg_inf = jnp.float32(-jnp.inf)
    lane = lax.broadcasted_iota(I32, (tm, LANES), 1)
    valid = lane < N_EXPERTS
    scores = _sigmoid(lg_ref[...])
    biased = jnp.where(valid, scores + bias_ref[...], neg_inf)
    g1 = _butterfly(biased, lane, jnp.maximum)
    first = _butterfly(jnp.where(biased == g1, lane, LANES), lane, jnp.minimum)
    g2 = _butterfly(jnp.where(lane == first, neg_inf, biased), lane, jnp.maximum)
    gs = g1 + g2
    gs = jnp.where(valid, gs, pltpu.roll(gs, N_EXPERTS, 1))
    gidx = lane >> 3
    beaten = jnp.zeros((tm, LANES), I32)
    for r in range(1, N_EXPERTS // GROUP_SIZE):
        other = pltpu.roll(gs, GROUP_SIZE * r, 1)
        og = (gidx - r) & (N_EXPERTS // GROUP_SIZE - 1)
        wins = jnp.logical_or(other > gs, jnp.logical_and(other == gs, og < gidx))
        beaten = beaten + wins.astype(I32)
    cur = jnp.where(jnp.logical_and(beaten < TOPK_GROUPS, valid), biased, neg_inf)
    sel = jnp.zeros((tm, LANES), jnp.bool_)
    for _ in range(TOP_K):
        m = jnp.max(cur, axis=1, keepdims=True)
        pick = lane == jnp.min(jnp.where(cur == m, lane, LANES), axis=1, keepdims=True)
        sel = jnp.logical_or(sel, pick)
        cur = jnp.where(pick, neg_inf, cur)
    wsel = jnp.where(sel, scores, 0.0)
    wd = wsel / jnp.sum(wsel, axis=1, keepdims=True) * ROUTED_SCALE
    selb = jnp.where(sel, 1.0, 0.0).astype(BF16)
    row = lax.broadcasted_iota(I32, (tm, tm), 0)
    col = lax.broadcasted_iota(I32, (tm, tm), 1)
    pos = _dot((col < row).astype(BF16), selb) + carry_ref[0:1, :]
    total = carry_ref[0:1, :] + jnp.sum(selb.astype(F32), axis=0, keepdims=True)
    carry_ref[0:1, :] = total
    cnt_ref[...] = jnp.broadcast_to(total, cnt_ref.shape)
    r2 = lax.broadcasted_iota(I32, (LANES, LANES), 0)
    c2 = lax.broadcasted_iota(I32, (LANES, LANES), 1)
    slot = _dot(selb, (r2 < c2).astype(BF16))
    lane_f = lane.astype(F32)
    e8 = jnp.zeros((tm, LANES), F32)
    w8 = jnp.zeros((tm, LANES), F32)
    p8 = jnp.zeros((tm, LANES), F32)
    for k in range(TOP_K):
        mk = jnp.logical_and(sel, slot == k)
        put = lane == k
        e8 = jnp.where(put, jnp.sum(jnp.where(mk, lane_f, 0.0), axis=1, keepdims=True), e8)
        w8 = jnp.where(put, jnp.sum(jnp.where(mk, wd, 0.0), axis=1, keepdims=True), w8)
        p8 = jnp.where(put, jnp.sum(jnp.where(mk, pos, 0.0), axis=1, keepdims=True), p8)
    e8_ref[...] = e8.astype(I32)
    w8_ref[...] = w8
    p8_ref[...] = p8.astype(I32)


def _route(logits, router_bias_pad):
    T = logits.shape[0]
    tm = 512
    row = pl.BlockSpec((tm, LANES), lambda i: (i, 0))
    return pl.pallas_call(
        functools.partial(_route_kernel, tm=tm),
        out_shape=(jax.ShapeDtypeStruct((T, LANES), I32), jax.ShapeDtypeStruct((T, LANES), F32),
                   jax.ShapeDtypeStruct((T, LANES), I32), jax.ShapeDtypeStruct((SUBLANES, LANES), F32)),
        grid=(T // tm,),
        in_specs=[row, pl.BlockSpec((1, LANES), lambda i: (0, 0))],
        out_specs=(row, row, row, pl.BlockSpec((SUBLANES, LANES), lambda i: (0, 0))),
        scratch_shapes=[pltpu.VMEM((SUBLANES, LANES), F32)],
        compiler_params=_cparams(1, 32),
        name="route",
    )(logits, router_bias_pad)


def _dest_kernel(e8_ref, p8_ref, start_ref, d_ref):
    e8 = e8_ref[...]
    lane = lax.broadcasted_iota(I32, e8.shape, 1)
    starts = start_ref[...]
    dest = p8_ref[...]
    for k in range(TOP_K):
        hit = lane == e8[:, k:k + 1]
        base = jnp.sum(jnp.where(hit, starts, 0.0), axis=1, keepdims=True).astype(I32)
        dest = jnp.where(lane == k, dest + base, dest)
    d_ref[...] = dest


def _dest_rows(e8, p8, pad_start_lanes):
    T = e8.shape[0]
    tm = 1024
    row = pl.BlockSpec((tm, LANES), lambda i: (i, 0))
    return pl.pallas_call(
        _dest_kernel, out_shape=jax.ShapeDtypeStruct((T, LANES), I32), grid=(T // tm,),
        in_specs=[row, row, pl.BlockSpec((1, LANES), lambda i: (0, 0))], out_specs=row,
        compiler_params=_cparams(1, 32), name="dest_rows",
    )(e8, p8, pad_start_lanes)


def _token_copy(src, src_token, dst, dst_token, sem, sub):
    return pltpu.make_async_copy(src.at[pl.ds(pl.multiple_of(src_token * sub, sub), sub), :],
                                 dst.at[pl.ds(pl.multiple_of(dst_token * sub, sub), sub), :], sem)


def _dispatch_kernel(pad_end_ref, padded_ref, dest_ref, hp_ref, xs_ref, zero_ref, sem_ref, *, tm, rows, n_blocks, sub):
    @pl.when(pl.program_id(0) == 0)
    def _():
        zero_ref[...] = jnp.zeros_like(zero_ref)

        def fill(e, do_wait):
            @pl.when(padded_ref[e] > 0)
            def _():
                start = pl.multiple_of((pad_end_ref[e] - rows) * sub, rows * sub)
                cp = pltpu.make_async_copy(zero_ref, xs_ref.at[pl.ds(start, rows * sub), :], sem_ref)
                if do_wait:
                    cp.wait()
                else:
                    cp.start()

        lax.fori_loop(0, N_EXPERTS, lambda e, c: (fill(e, False), c)[1], 0)
        lax.fori_loop(0, N_EXPERTS, lambda e, c: (fill(e, True), c)[1], 0)

        def tail(b):
            return pltpu.make_async_copy(zero_ref, xs_ref.at[pl.ds(pl.multiple_of(b * rows * sub, rows * sub), rows * sub), :],
                                         sem_ref)

        n_used = pad_end_ref[N_EXPERTS - 1] // rows
        lax.fori_loop(n_used, n_blocks, lambda b, c: (tail(b).start(), c)[1], 0)
        lax.fori_loop(n_used, n_blocks, lambda b, c: (tail(b).wait(), c)[1], 0)

    def issue(t, c):
        for k in range(TOP_K):
            _token_copy(hp_ref, t, xs_ref, dest_ref[t * TOP_K + k], sem_ref, sub).start(priority=k % 2)
        return c

    def drain(t, c):
        for k in range(TOP_K):
            _token_copy(hp_ref, t, xs_ref, dest_ref[t * TOP_K + k], sem_ref, sub).wait()
        return c

    lax.fori_loop(0, tm, issue, 0)
    lax.fori_loop(0, tm, drain, 0)


def _dispatch(hp, dest_flat, pad_end, padded, n_rows, sub):
    tm = TOKEN_TILE
    T = hp.shape[0] // sub
    grid_spec = pltpu.PrefetchScalarGridSpec(
        num_scalar_prefetch=2,
        grid=(T // tm,),
        in_specs=[pl.BlockSpec((tm * TOP_K,), lambda i, pe, pd: (i,), memory_space=pltpu.SMEM),
                  pl.BlockSpec((tm * sub, LANES), lambda i, pe, pd: (i, 0))],
        out_specs=pl.BlockSpec(memory_space=pl.ANY),
        scratch_shapes=[pltpu.VMEM((EXPERT_ROWS * sub, LANES), U32), pltpu.SemaphoreType.DMA(())],
    )
    return pl.pallas_call(
        functools.partial(_dispatch_kernel, tm=tm, rows=EXPERT_ROWS, n_blocks=n_rows // EXPERT_ROWS, sub=sub),
        out_shape=jax.ShapeDtypeStruct((n_rows * sub, LANES), U32),
        grid_spec=grid_spec,
        compiler_params=_cparams(1, 32),
        name="dispatch",
    )(pad_end, padded, dest_flat, hp)


def _expert_kernel(blk_e_ref, next_e_ref, n_used_ref, x_ref, wg_hbm, wu_hbm, wd_hbm, y_ref,
                   wg_f32, wu_f32, wd_f32, wg_bf, wu_bf, wd_bf, sem_ref, slot_ref, *, layer, rows, width):
    i = pl.program_id(0)
    e = blk_e_ref[i]
    active = i < n_used_ref[0]
    first = jnp.logical_or(i == 0, e != blk_e_ref[jnp.maximum(i - 1, 0)])

    def fetch(expert, slot):
        return [pltpu.make_async_copy(w.at[layer, expert], buf.at[slot], sem_ref.at[slot])
                for w, buf in ((wg_hbm, wg_f32), (wu_hbm, wu_f32), (wd_hbm, wd_f32))]

    @pl.when(i == 0)
    def _():
        slot_ref[0] = 0
        for cp in fetch(e, 0):
            cp.start()

    @pl.when(jnp.logical_and(active, first))
    def _():
        slot = slot_ref[0]
        for cp in fetch(e, slot):
            cp.wait()
        nxt = next_e_ref[i]

        @pl.when(nxt >= 0)
        def _():
            for cp in fetch(nxt, 1 - slot):
                cp.start()

        wg_bf[...] = wg_f32[slot].astype(BF16)
        wu_bf[...] = wu_f32[slot].astype(BF16)
        wd_bf[...] = wd_f32[slot].astype(BF16)
        slot_ref[0] = 1 - slot

    @pl.when(active)
    def _():
        x = _unpack_pairs(_load_token_rows(x_ref, 0, rows, width))
        g = _dot(x, wg_bf[...])
        u = _dot(x, wu_bf[...])
        hb = (g * _sigmoid(g)) * u
        _store_token_rows(y_ref, _pack_pairs(_dot(hb.astype(BF16), wd_bf[...])))

    @pl.when(jnp.logical_not(active))
    def _():
        y_ref[...] = jnp.zeros_like(y_ref)


def _experts(xs, blk_e, next_e, n_used, w_gate, w_up, w_down, layer):
    _, E, D, Hx = w_gate.shape
    W = D // 2
    sub = _rows_per_token(W)
    M = EXPERT_ROWS
    nblk = xs.shape[0] // (M * sub)
    xrow = lambda i, be, ne, nu: (jnp.minimum(i, nu[0] - 1), 0)
    hbm = pl.BlockSpec(memory_space=pl.ANY)
    grid_spec = pltpu.PrefetchScalarGridSpec(
        num_scalar_prefetch=3,
        grid=(nblk,),
        in_specs=[pl.BlockSpec((M * sub, LANES), xrow), hbm, hbm, hbm],
        out_specs=pl.BlockSpec((M * sub, LANES), lambda i, be, ne, nu: (i, 0)),
        scratch_shapes=[pltpu.VMEM((2, D, Hx), F32), pltpu.VMEM((2, D, Hx), F32), pltpu.VMEM((2, Hx, D), F32),
                        pltpu.VMEM((D, Hx), BF16), pltpu.VMEM((D, Hx), BF16), pltpu.VMEM((Hx, D), BF16),
                        pltpu.SemaphoreType.DMA((2,)), pltpu.SMEM((1,), I32)],
    )
    return pl.pallas_call(
        functools.partial(_expert_kernel, layer=layer, rows=M, width=W),
        out_shape=jax.ShapeDtypeStruct(xs.shape, U32),
        grid_spec=grid_spec,
        compiler_params=_cparams(1, 52),
        name="experts",
    )(blk_e, next_e, n_used, xs, w_gate, w_up, w_down)


def _next_expert(blk_e, cnt):
    E = cnt.shape[0]
    idx = jnp.where(cnt > 0, jnp.arange(E, dtype=I32), E)
    later = jnp.concatenate([lax.cummin(idx, axis=0, reverse=True)[1:], jnp.full((1,), E, I32)])
    return jnp.where(later < E, later, -1)[blk_e]


def _unpack_pairs_f32(w):
    a = lax.bitcast_convert_type(w & jnp.uint32(0xFFFF0000), F32)
    b = lax.bitcast_convert_type(w << 16, F32)
    return jnp.concatenate([a, b], axis=1)


def _combine_kernel(dcur_ref, dnext_ref, x_ref, ysh_ref, w8_ref, g2_ref, fn_ref, ys_ref, o_ref, buf_ref, sem_ref,
                    *, tm, n_steps, width, final_norm):
    i = pl.program_id(0)
    slot = lax.rem(i, 2)
    sub = _rows_per_token(width)

    def issue(dref, s):
        def body(t, c):
            for k in range(TOP_K):
                _token_copy(ys_ref, dref[t * TOP_K + k], buf_ref.at[s], k * tm + t, sem_ref.at[s], sub).start(priority=k % 2)
            return c
        lax.fori_loop(0, tm, body, 0)

    @pl.when(i == 0)
    def _():
        issue(dcur_ref, 0)

    @pl.when(i + 1 < n_steps)
    def _():
        issue(dnext_ref, 1 - slot)

    def drain(t, c):
        for k in range(TOP_K):
            _token_copy(ys_ref, 0, buf_ref.at[slot], 0, sem_ref.at[slot], sub).wait()
        return c

    lax.fori_loop(0, tm, drain, 0)
    w8 = w8_ref[...]
    y = _unpack_pairs_f32(_load_token_rows(ysh_ref, 0, tm, width))
    gathered = buf_ref.at[slot]
    for k in range(TOP_K):
        y = y + w8[:, k:k + 1] * _unpack_pairs_f32(_load_token_rows(gathered, k * tm, tm, width))
    out = x_ref[...] + g2_ref[...] * y
    if final_norm:
        out = out * lax.rsqrt(jnp.mean(out * out, axis=-1, keepdims=True) + RMS_EPS) * fn_ref[...]
    o_ref[...] = out


def _combine(x, ysh, ys, dest_flat, w8, gate2, final_g, S, final_norm):
    T, D = x.shape
    W = D // 2
    sub = _rows_per_token(W)
    tm = TOKEN_TILE
    nb = S // tm
    n_steps = T // tm
    row = lambda i: (i, 0)
    return pl.pallas_call(
        functools.partial(_combine_kernel, tm=tm, n_steps=n_steps, width=W, final_norm=final_norm),
        out_shape=jax.ShapeDtypeStruct((T, D), F32),
        grid=(n_steps,),
        in_specs=[pl.BlockSpec((tm * TOP_K,), lambda i: (i,), memory_space=pltpu.SMEM),
                  pl.BlockSpec((tm * TOP_K,), lambda i: (jnp.minimum(i + 1, n_steps - 1),), memory_space=pltpu.SMEM),
                  pl.BlockSpec((tm, D), row), pl.BlockSpec((tm * sub, LANES), row),
                  pl.BlockSpec((tm, LANES), row),
                  pl.BlockSpec((None, 1, D), lambda i: (i // nb, 0, 0)),
                  pl.BlockSpec((1, D), lambda i: (0, 0)),
                  pl.BlockSpec(memory_space=pl.ANY)],
        out_specs=pl.BlockSpec((tm, D), row),
        scratch_shapes=[pltpu.VMEM((2, TOP_K * tm * sub, LANES), U32), pltpu.SemaphoreType.DMA((2,))],
        compiler_params=_cparams(1, 32),
        name="combine",
    )(dest_flat, dest_flat, x, ysh, w8, gate2, final_g, ys)


def _pad_lanes(v, offset=0):
    out = jnp.zeros((1, LANES), F32)
    return out.at[0, offset:offset + v.shape[0]].set(v.astype(F32))


def kernel(x, c, positions, attn_norm, w_ada, b_ada, w_in, fox_bias, gla_w_gate, gla_b_gate, gla_norm,
           w_branch, w_out, ffn_norm, w_router, router_bias, w_exp_gate, w_exp_up, w_exp_down,
           w_sh_gate, w_sh_up, w_sh_down, final_norm):
    B, S, D = x.shape
    L = w_ada.shape[0]
    T = B * S
    E = N_EXPERTS
    M = EXPERT_ROWS
    xf = x.reshape(T, D)

    c_pad = jnp.zeros((SUBLANES, D), F32).at[:B].set(c)
    mod = _ada_mod(c_pad, w_ada, b_ada)

    half = ROPE_DIM // 2
    inv_freq = jnp.power(ROPE_THETA, -jnp.arange(half, dtype=F32) * 2.0 / ROPE_DIM)
    freq_lanes = _pad_lanes(jnp.concatenate([inv_freq, inv_freq]))
    cos_tab, sin_tab = _rope_tables(positions.reshape(T, 1), freq_lanes)

    w_seg_a, w_seg_b, w_seg_c, w_misc_t = _input_weights(w_in)
    w_branch_bf16 = w_branch.astype(BF16)
    n_blocks = (T * TOP_K + E * (M - 1) + M - 1) // M
    P = n_blocks * M

    for l in range(L):
        m6 = mod[l, :B].reshape(B, 6, 1, D)
        shift1, scale1, gate1, shift2, scale2, gate2 = (m6[:, n] for n in range(6))

        h1 = _norm_mod(xf, attn_norm[l].reshape(1, D), scale1, shift1, S)
        proj_a = _nt_matmul(h1, w_seg_a, l, tm=1024, tn=1536, out_dtype=BF16)
        proj_b = _nt_matmul(h1, w_seg_b, l, tm=1024, tn=1536, out_dtype=BF16)
        proj_c = _nt_matmul(h1, w_seg_c, l, tm=1024, tn=2048, out_dtype=BF16)
        misc = _nt_matmul(h1, w_misc_t, l, tm=1024, tn=LANES, out_dtype=F32)

        o_a = _sb_attention(proj_a, B, S, COL_SB)
        qa, ka, va = _fox_prep(proj_a, misc, _pad_lanes(fox_bias[l], MISC_FOX_F), B, S)
        o_b = _flash_attention(qa, ka, va, B, S)
        qa, ka, va = _moba_prep(proj_b, cos_tab, sin_tab, B, S)
        o_c = _flash_attention(qa, ka, va, B, S)
        wg_pad = jnp.zeros((LANES, gla_w_gate.shape[2]), F32).at[MISC_GLR:MISC_GLR + GLA_GATE_RANK].set(gla_w_gate[l])
        o_d = _gla(proj_b, misc, wg_pad, gla_b_gate[l].reshape(1, -1), gla_norm[l].reshape(1, -1), B, S)

        merged = _merge((o_a, o_b, o_c, o_d), proj_c, w_branch_bf16, l)
        xf = _matmul(merged, w_out, tm=512, tn=1024, residual=(xf, gate1), S=S, layer=l)

        wr_pad = jnp.zeros((D, LANES), F32).at[:, :E].set(w_router[l])
        hp, logits = _norm_mod(xf, ffn_norm[l].reshape(1, D), scale2, shift2, S, w_router_pad=wr_pad)
        e8, w8, p8, counts = _route(logits, _pad_lanes(router_bias[l]))
        cnt = counts[0, :E].astype(I32)
        padded = (cnt + M - 1) // M * M
        pad_end = jnp.cumsum(padded)
        pad_start = pad_end - padded
        dest = _dest_rows(e8, p8, _pad_lanes(pad_start))[:, :TOP_K].reshape(T * TOP_K)
        blk_start = jnp.arange(n_blocks, dtype=I32) * M
        blk_e = jnp.minimum(jnp.sum((pad_end[None, :] <= blk_start[:, None]).astype(I32), axis=1), E - 1)
        n_used = (pad_end[E - 1:] // M).astype(I32)

        xs = _dispatch(hp, dest, pad_end.astype(I32), padded.astype(I32), P, _rows_per_token(D // 2))
        ys = _experts(xs, blk_e, _next_expert(blk_e, cnt), n_used, w_exp_gate, w_exp_up, w_exp_down, l)
        ysh = _experts(hp, jnp.zeros((T // M,), I32), jnp.full((T // M,), -1, I32), jnp.full((1,), T // M, I32),
                       w_sh_gate[:, None], w_sh_up[:, None], w_sh_down[:, None], l)
        last = l == L - 1
        xf = _combine(xf, ysh, ys, dest, w8, gate2, final_norm.reshape(1, D), S, final_norm=last)

    return xf.reshape(B, S, D)
```

```python
import functools

import jax
import jax.numpy as jnp
from jax import lax
from jax.experimental import pallas as pl
from jax.experimental.pallas import tpu as pltpu

F32 = jnp.float32
BF16 = jnp.bfloat16
I32 = jnp.int32
U32 = jnp.uint32

HEAD_DIM = 128
N_HEADS = 4
BRANCH_WIDTH = N_HEADS * HEAD_DIM
GLA_HEAD_K = 64
GLA_CHUNK = 64
GLA_GATE_RANK = 16
GLA_GATE_NORMALIZER = 16.0
MOBA_BLOCK = 256
MOBA_TOPK = 3
ROPE_THETA = 500000.0
ROPE_DIM = HEAD_DIM // 4
N_EXPERTS = 64
GROUP_SIZE = 8
TOPK_GROUPS = 4
TOP_K = 8
ROUTED_SCALE = 2.5
RMS_EPS = 1e-6

LANES = 128
SUBLANES = 8
MIB = 1024 * 1024
NEG_BIG = -1e30

ATT_BLOCK = 256
FLASH_Q_BLOCK = 1024
EXPERT_ROWS = 256
TOKEN_TILE = 256

COL_SB, COL_FOX, COL_MOBA = 0, 12, 24
COL_GQ, COL_GK, COL_GV, COL_GR, COL_GATES = 36, 38, 40, 44, 48
SEG_B, SEG_C = COL_MOBA, COL_GATES
MISC_FOX_F, MISC_GLR = 0, 4


def _cparams(n_axes, vmem_mib):
    return pltpu.CompilerParams(dimension_semantics=("arbitrary",) * n_axes,
                                vmem_limit_bytes=vmem_mib * MIB)


def _softplus(z):
    return jnp.maximum(z, 0.0) + jnp.log(1.0 + jnp.exp(-jnp.abs(z)))


def _log_sigmoid(z):
    return -_softplus(-z)


def _sigmoid(z):
    return 1.0 / (1.0 + jnp.exp(-z))


def _dot(a, b, precision=None):
    return jnp.dot(a, b, preferred_element_type=F32, precision=precision)


def _dot_nt(a, b, precision=None):
    return lax.dot_general(a, b, (((1,), (1,)), ((), ())), preferred_element_type=F32,
                           precision=precision)


def _dot_tn(a, b, precision=None):
    return lax.dot_general(a, b, (((0,), (0,)), ((), ())), preferred_element_type=F32,
                           precision=precision)


HIGHEST = lax.Precision.HIGHEST


def _split3(x):
    hi = x.astype(BF16)
    r1 = x - hi.astype(F32)
    mid = r1.astype(BF16)
    lo = (r1 - mid.astype(F32)).astype(BF16)
    return hi, mid, lo


def _ada_kernel(c_ref, w_ref, b_ref, o_ref):
    c = c_ref[...]
    c_act = (c * _sigmoid(c)).astype(BF16)
    o_ref[...] = _dot(c_act, w_ref[...].astype(BF16)) + b_ref[...]


def _ada_mod(c_pad, w_ada, b_ada):
    L, D, N = w_ada.shape
    tn = 1536
    return pl.pallas_call(
        _ada_kernel,
        out_shape=jax.ShapeDtypeStruct((L, SUBLANES, N), F32),
        grid=(L, N // tn),
        in_specs=[pl.BlockSpec((SUBLANES, D), lambda l, j: (0, 0)),
                  pl.BlockSpec((None, D, tn), lambda l, j: (l, 0, j)),
                  pl.BlockSpec((None, 1, tn), lambda l, j: (l, 0, j))],
        out_specs=pl.BlockSpec((None, SUBLANES, tn), lambda l, j: (l, 0, j)),
        compiler_params=_cparams(2, 40),
        name="ada_mod",
    )(c_pad, w_ada, b_ada.reshape(L, 1, N))


def _normed(x, g, scale, shift):
    y = x * lax.rsqrt(jnp.mean(x * x, axis=-1, keepdims=True) + RMS_EPS)
    return (y * g) * (1.0 + scale) + shift


def _norm_mod_kernel(x_ref, g_ref, sc_ref, sh_ref, h_ref):
    h_ref[...] = _normed(x_ref[...], g_ref[...], sc_ref[...], sh_ref[...]).astype(BF16)


def _pack_pairs(h):
    half = h.shape[1] // 2
    hi = lax.bitcast_convert_type(h[:, :half].astype(BF16).astype(F32), U32)
    lo = lax.bitcast_convert_type(h[:, half:].astype(BF16).astype(F32), U32)
    return hi | (lo >> 16)


def _unpack_pairs(w):
    a = lax.bitcast_convert_type(w & jnp.uint32(0xFFFF0000), F32).astype(BF16)
    b = lax.bitcast_convert_type(w << 16, F32).astype(BF16)
    return jnp.concatenate([a, b], axis=1)


def _rows_per_token(width):
    return width // LANES


def _store_token_rows(ref, val):
    n, width = val.shape
    sub = _rows_per_token(width)
    for s in range(sub):
        ref[pl.ds(s, n, stride=sub), :] = val[:, s * LANES:(s + 1) * LANES]


def _load_token_rows(ref, first_token, n, width):
    sub = _rows_per_token(width)
    return jnp.concatenate([ref[pl.ds(first_token * sub + s, n, stride=sub), :] for s in range(sub)], axis=1)


def _norm_route_kernel(x_ref, g_ref, sc_ref, sh_ref, wr_ref, hp_ref, lg_ref):
    h = _normed(x_ref[...], g_ref[...], sc_ref[...], sh_ref[...])
    _store_token_rows(hp_ref, _pack_pairs(h))
    lg_ref[...] = _dot(h, wr_ref[...], precision=HIGHEST)


def _norm_mod(x, g, scale, shift, S, w_router_pad=None):
    T, D = x.shape
    tm = 512
    nb = S // tm
    row = lambda i: (i, 0)
    per_batch = pl.BlockSpec((None, 1, D), lambda i: (i // nb, 0, 0))
    in_specs = [pl.BlockSpec((tm, D), row), pl.BlockSpec((1, D), lambda i: (0, 0)), per_batch, per_batch]
    if w_router_pad is None:
        return pl.pallas_call(
            _norm_mod_kernel, out_shape=jax.ShapeDtypeStruct((T, D), BF16), grid=(T // tm,),
            in_specs=in_specs, out_specs=pl.BlockSpec((tm, D), row),
            compiler_params=_cparams(1, 32), name="norm_mod",
        )(x, g, scale, shift)
    return pl.pallas_call(
        _norm_route_kernel,
        out_shape=(jax.ShapeDtypeStruct((T * (D // 2) // LANES, LANES), U32), jax.ShapeDtypeStruct((T, LANES), F32)),
        grid=(T // tm,),
        in_specs=in_specs + [pl.BlockSpec((D, LANES), lambda i: (0, 0))],
        out_specs=(pl.BlockSpec((tm * (D // 2) // LANES, LANES), row), pl.BlockSpec((tm, LANES), row)),
        compiler_params=_cparams(1, 32), name="norm_route",
    )(x, g, scale, shift, w_router_pad)


def _mm_kernel(*refs, cast_w, residual):
    if residual:
        a_ref, w_ref, x_ref, g_ref, o_ref = refs[:5]
        scratch = refs[5:]
    else:
        a_ref, w_ref, o_ref = refs[:3]
        scratch = refs[3:]
    if cast_w:
        wbf_ref, = scratch

        @pl.when(pl.program_id(1) == 0)
        def _():
            wbf_ref[...] = w_ref[...].astype(BF16)

        w = wbf_ref[...]
    else:
        w = w_ref[...]
    acc = _dot(a_ref[...], w)
    if residual:
        acc = x_ref[...] + g_ref[...] * acc
    o_ref[...] = acc.astype(o_ref.dtype)


def _matmul(a, w, tm, tn, out_dtype=F32, residual=None, S=None, vmem_mib=48, layer=None):
    M, K = a.shape
    N = w.shape[-1]
    cast_w = w.dtype != BF16
    if layer is None:
        w_spec = pl.BlockSpec((K, tn), lambda j, i: (0, j))
    else:
        w_spec = pl.BlockSpec((None, K, tn), lambda j, i: (layer, 0, j))
    in_specs = [pl.BlockSpec((tm, K), lambda j, i: (i, 0)), w_spec]
    args = [a, w]
    if residual is not None:
        x, gate = residual
        nb = S // tm
        in_specs += [pl.BlockSpec((tm, tn), lambda j, i: (i, j)),
                     pl.BlockSpec((None, 1, tn), lambda j, i: (i // nb, 0, j))]
        args += [x, gate]
    return pl.pallas_call(
        functools.partial(_mm_kernel, cast_w=cast_w, residual=residual is not None),
        out_shape=jax.ShapeDtypeStruct((M, N), out_dtype),
        grid=(N // tn, M // tm),
        in_specs=in_specs,
        out_specs=pl.BlockSpec((tm, tn), lambda j, i: (i, j)),
        scratch_shapes=[pltpu.VMEM((K, tn), BF16)] if cast_w else [],
        compiler_params=_cparams(2, vmem_mib),
        name="matmul_res" if residual is not None else "matmul",
    )(*args)


def _nt_matmul_kernel(a_ref, w_ref, o_ref):
    o_ref[...] = _dot_nt(a_ref[...], w_ref[...]).astype(o_ref.dtype)


def _nt_matmul(a, wt, layer, tm, tn, out_dtype):
    M, K = a.shape
    N = wt.shape[1]
    return pl.pallas_call(
        _nt_matmul_kernel,
        out_shape=jax.ShapeDtypeStruct((M, N), out_dtype),
        grid=(N // tn, M // tm),
        in_specs=[pl.BlockSpec((tm, K), lambda j, i: (i, 0)),
                  pl.BlockSpec((None, tn, K), lambda j, i: (layer, j, 0))],
        out_specs=pl.BlockSpec((tm, tn), lambda j, i: (i, j)),
        compiler_params=_cparams(2, 48),
        name="in_proj",
    )(a, wt)


def _input_weights(w_in):
    c_fox_f = COL_MOBA * LANES
    c_glr = COL_GATES * LANES + N_HEADS
    wt = jnp.swapaxes(w_in, 1, 2)
    seg_a = wt[:, :c_fox_f].astype(BF16)
    seg_b = wt[:, c_fox_f + N_HEADS:c_glr].astype(BF16)
    seg_c = wt[:, c_glr + GLA_GATE_RANK:].astype(BF16)
    pad = jnp.zeros((wt.shape[0], LANES - N_HEADS - GLA_GATE_RANK, wt.shape[2]), wt.dtype)
    misc = jnp.concatenate([wt[:, c_fox_f:c_fox_f + N_HEADS], wt[:, c_glr:c_glr + GLA_GATE_RANK], pad],
                           axis=1).astype(BF16)
    return seg_a, seg_b, seg_c, misc


SB_LOG_WEIGHT_FLOOR = -110.0


def _sb_kernel(q_ref, k_ref, v_ref, o_ref, *, blk, scale):
    i = pl.program_id(1)
    row = lax.broadcasted_iota(I32, (blk, blk), 0)
    col = lax.broadcasted_iota(I32, (blk, blk), 1)
    later = (row > col).astype(BF16)

    def cond(state):
        jj, alive, _ = state
        return jnp.logical_and(jj <= i, alive)

    def body(state):
        jj, _, heads = state
        j = i - jj
        start = pl.multiple_of(j * blk, blk)
        past = jnp.logical_or(j < i, col < row)
        new, top = [], None
        for h in range(N_HEADS):
            c, acc = heads[h]
            hs = slice(h * HEAD_DIM, (h + 1) * HEAD_DIM)
            q = (q_ref[:, hs].astype(F32) * scale).astype(BF16)
            k = k_ref[pl.ds(start, blk), hs]
            v = v_ref[pl.ds(start, blk), hs]
            z = _dot_nt(q, k)
            ls = jnp.where(past, -_softplus(z), 0.0)
            hi = ls.astype(BF16)
            lo = (ls - hi.astype(F32)).astype(BF16)
            between = _dot(hi, later) + _dot(lo, later)
            w = jnp.where(past, jnp.exp(z + ls + between + c), 0.0)
            acc = acc + _dot(w.astype(BF16), v)
            c = c + jnp.sum(ls, axis=1, keepdims=True)
            new.append((c, acc))
            top = c if top is None else jnp.maximum(top, c)
        return jj + 1, jnp.max(top) > SB_LOG_WEIGHT_FLOOR, tuple(new)

    init = tuple((jnp.zeros((blk, 1), F32), jnp.zeros((blk, HEAD_DIM), F32)) for _ in range(N_HEADS))
    _, _, heads = lax.while_loop(cond, body, (jnp.int32(0), jnp.bool_(True), init))
    for h in range(N_HEADS):
        o_ref[:, h * HEAD_DIM:(h + 1) * HEAD_DIM] = heads[h][1]


def _sb_attention(proj, B, S, col0):
    T = proj.shape[0]
    blk = ATT_BLOCK
    nq = S // blk
    W = BRANCH_WIDTH
    cb = col0 * LANES // W
    return pl.pallas_call(
        functools.partial(_sb_kernel, blk=blk, scale=HEAD_DIM ** -0.5),
        out_shape=jax.ShapeDtypeStruct((T, W), F32),
        grid=(B, nq),
        in_specs=[pl.BlockSpec((blk, W), lambda b, i: (b * nq + i, cb)),
                  pl.BlockSpec((S, W), lambda b, i: (b, cb + 1)),
                  pl.BlockSpec((S, W), lambda b, i: (b, cb + 2))],
        out_specs=pl.BlockSpec((blk, W), lambda b, i: (b * nq + i, 0)),
        compiler_params=_cparams(2, 44),
        name="sb_attention",
    )(proj, proj, proj)


def _flash_kernel(qa_ref, ka_ref, v_ref, o_ref, *, bq, bk):
    i = pl.program_id(1)
    A = 2 * HEAD_DIM
    ratio = bq // bk
    row = lax.broadcasted_iota(I32, (bq, bk), 0)
    col = lax.broadcasted_iota(I32, (bq, bk), 1)

    def step(j, carry, key_offset=None):
        start = pl.multiple_of(j * bk, bk)
        new = []
        for h in range(N_HEADS):
            m, l, acc = carry[h]
            s = _dot_nt(qa_ref[:, h * A:(h + 1) * A], ka_ref[pl.ds(start, bk), h * A:(h + 1) * A])
            if key_offset is not None:
                s = jnp.where(col + key_offset <= row, s, NEG_BIG)
            m_new = jnp.maximum(m, jnp.max(s, axis=1, keepdims=True))
            alpha = jnp.exp(m - m_new)
            p = jnp.exp(s - m_new)
            l = alpha * l + jnp.sum(p, axis=1, keepdims=True)
            acc = alpha * acc + _dot(p.astype(BF16), v_ref[pl.ds(start, bk), h * HEAD_DIM:(h + 1) * HEAD_DIM])
            new.append((m_new, l, acc))
        return tuple(new)

    init = tuple((jnp.full((bq, 1), NEG_BIG, F32), jnp.zeros((bq, 1), F32),
                  jnp.zeros((bq, HEAD_DIM), F32)) for _ in range(N_HEADS))
    heads = lax.fori_loop(0, i * ratio, step, init)
    for d in range(ratio):
        heads = step(i * ratio + d, heads, key_offset=d * bk)
    for h in range(N_HEADS):
        _, l, acc = heads[h]
        o_ref[:, h * HEAD_DIM:(h + 1) * HEAD_DIM] = acc / l


def _flash_attention(qa, ka, va, B, S):
    T = qa.shape[0]
    bq, bk = FLASH_Q_BLOCK, ATT_BLOCK
    nq = S // bq
    W = BRANCH_WIDTH
    return pl.pallas_call(
        functools.partial(_flash_kernel, bq=bq, bk=bk),
        out_shape=jax.ShapeDtypeStruct((T, W), F32),
        grid=(B, nq),
        in_specs=[pl.BlockSpec((bq, 2 * W), lambda b, i: (b * nq + i, 0)),
                  pl.BlockSpec((S, 2 * W), lambda b, i: (b, 0)),
                  pl.BlockSpec((S, W), lambda b, i: (b, 0))],
        out_specs=pl.BlockSpec((bq, W), lambda b, i: (b * nq + i, 0)),
        compiler_params=_cparams(2, 56),
        name="flash_attention",
    )(qa, ka, va)


def _fox_prep_kernel(q_ref, k_ref, v_ref, misc_ref, bias_ref, qa_ref, ka_ref, va_ref, carry_ref, *, tb, scale):
    va_ref[...] = v_ref[...].astype(BF16)
    @pl.when(pl.program_id(1) == 0)
    def _():
        carry_ref[...] = jnp.zeros_like(carry_ref)

    lane = lax.broadcasted_iota(I32, (tb, LANES), 1)
    lf = jnp.where(lane < N_HEADS, _log_sigmoid(misc_ref[...] + bias_ref[...]), 0.0)
    row = lax.broadcasted_iota(I32, (tb, tb), 0)
    col = lax.broadcasted_iota(I32, (tb, tb), 1)
    incl = (col <= row).astype(F32)
    F = _dot(incl, lf, precision=HIGHEST) + carry_ref[0:1, :]
    carry_ref[0:1, :] = F[tb - 1:tb, :]
    for h in range(N_HEADS):
        Fh = jnp.broadcast_to(F[:, h:h + 1], (tb, LANES))
        hi, mid, lo = (p.astype(F32) for p in _split3(Fh))
        ones = jnp.ones((tb, LANES), F32)
        zeros = jnp.zeros((tb, LANES), F32)
        eq = jnp.where(lane == 0, hi, jnp.where(lane == 1, mid, jnp.where(lane == 2, lo,
             jnp.where(lane < 6, ones, zeros))))
        ek = jnp.where(lane < 3, ones, jnp.where(lane == 3, -hi, jnp.where(lane == 4, -mid,
             jnp.where(lane == 5, -lo, zeros))))
        hs = slice(h * HEAD_DIM, (h + 1) * HEAD_DIM)
        qa_ref[:, 2 * h * HEAD_DIM:(2 * h + 1) * HEAD_DIM] = (q_ref[:, hs].astype(F32) * scale).astype(BF16)
        qa_ref[:, (2 * h + 1) * HEAD_DIM:(2 * h + 2) * HEAD_DIM] = eq.astype(BF16)
        ka_ref[:, 2 * h * HEAD_DIM:(2 * h + 1) * HEAD_DIM] = k_ref[:, hs].astype(BF16)
        ka_ref[:, (2 * h + 1) * HEAD_DIM:(2 * h + 2) * HEAD_DIM] = ek.astype(BF16)


def _fox_prep(proj, misc, fox_bias_pad, B, S):
    T = proj.shape[0]
    tb = 512
    nb = S // tb
    qblk = COL_FOX * LANES // BRANCH_WIDTH
    out = jax.ShapeDtypeStruct((T, 2 * BRANCH_WIDTH), BF16)
    return pl.pallas_call(
        functools.partial(_fox_prep_kernel, tb=tb, scale=HEAD_DIM ** -0.5),
        out_shape=(out, out, jax.ShapeDtypeStruct((T, BRANCH_WIDTH), BF16)),
        grid=(B, nb),
        in_specs=[pl.BlockSpec((tb, BRANCH_WIDTH), lambda b, i: (b * nb + i, qblk)),
                  pl.BlockSpec((tb, BRANCH_WIDTH), lambda b, i: (b * nb + i, qblk + 1)),
                  pl.BlockSpec((tb, BRANCH_WIDTH), lambda b, i: (b * nb + i, qblk + 2)),
                  pl.BlockSpec((tb, LANES), lambda b, i: (b * nb + i, 0)),
                  pl.BlockSpec((1, LANES), lambda b, i: (0, 0))],
        out_specs=(pl.BlockSpec((tb, 2 * BRANCH_WIDTH), lambda b, i: (b * nb + i, 0)),
                   pl.BlockSpec((tb, 2 * BRANCH_WIDTH), lambda b, i: (b * nb + i, 0)),
                   pl.BlockSpec((tb, BRANCH_WIDTH), lambda b, i: (b * nb + i, 0))),
        scratch_shapes=[pltpu.VMEM((SUBLANES, LANES), F32)],
        compiler_params=_cparams(2, 32),
        name="fox_prep",
    )(proj, proj, proj, misc, fox_bias_pad)


def _rope(x, cos, sin, lane):
    half = ROPE_DIM // 2
    up = pltpu.roll(x, half, 1)
    down = pltpu.roll(x, LANES - half, 1)
    rot = jnp.where(lane < half, -down * sin, jnp.where(lane < ROPE_DIM, up * sin, 0.0))
    return x * jnp.where(lane < ROPE_DIM, cos, 1.0) + rot


def _rope_table_kernel(pos_ref, freq_ref, cos_ref, sin_ref):
    ang = pos_ref[...].astype(F32) * freq_ref[...]
    cos_ref[...] = jnp.cos(ang)
    sin_ref[...] = jnp.sin(ang)


def _rope_tables(pos_col, freq_lanes):
    T = pos_col.shape[0]
    tb = 1024
    row = lambda i: (i, 0)
    out = jax.ShapeDtypeStruct((T, LANES), F32)
    return pl.pallas_call(
        _rope_table_kernel, out_shape=(out, out), grid=(T // tb,),
        in_specs=[pl.BlockSpec((tb, 1), row), pl.BlockSpec((1, LANES), lambda i: (0, 0))],
        out_specs=(pl.BlockSpec((tb, LANES), row), pl.BlockSpec((tb, LANES), row)),
        compiler_params=_cparams(1, 32), name="rope_tables",
    )(pos_col, freq_lanes)


def _moba_k_kernel(k_ref, v_ref, cos_ref, sin_ref, ka_ref, va_ref, kmean_ref, *, tb):
    i = pl.program_id(1)
    va_ref[...] = v_ref[...].astype(BF16)

    @pl.when(i == 0)
    def _():
        kmean_ref[...] = jnp.zeros_like(kmean_ref)

    lane = lax.broadcasted_iota(I32, (tb, LANES), 1)
    cos, sin = cos_ref[...], sin_ref[...]
    onehot = jnp.where(lane == i, 1.0, 0.0).astype(BF16)
    this_row = lax.broadcasted_iota(I32, (LANES, HEAD_DIM), 0) == i
    for h in range(N_HEADS):
        hs = slice(h * HEAD_DIM, (h + 1) * HEAD_DIM)
        kr = _rope(k_ref[:, hs].astype(F32), cos, sin, lane)
        mean = jnp.sum(kr, axis=0, keepdims=True) * (1.0 / tb)
        kmean_ref[:, hs] = jnp.where(this_row, mean, kmean_ref[:, hs])
        ka_ref[:, 2 * h * HEAD_DIM:(2 * h + 1) * HEAD_DIM] = kr.astype(BF16)
        ka_ref[:, (2 * h + 1) * HEAD_DIM:(2 * h + 2) * HEAD_DIM] = onehot


def _moba_q_kernel(q_ref, cos_ref, sin_ref, kmean_ref, qa_ref, *, tb, scale):
    i = pl.program_id(1)
    lane = lax.broadcasted_iota(I32, (tb, LANES), 1)
    cos, sin = cos_ref[...], sin_ref[...]
    neg_inf = jnp.float32(-jnp.inf)
    for h in range(N_HEADS):
        hs = slice(h * HEAD_DIM, (h + 1) * HEAD_DIM)
        qr = _rope(q_ref[:, hs].astype(F32), cos, sin, lane)
        gate = _dot_nt(qr, kmean_ref[:, hs], precision=HIGHEST)
        cur = jnp.where(lane < i, gate, neg_inf)
        chosen = lane == i
        for _ in range(MOBA_TOPK):
            m = jnp.max(cur, axis=1, keepdims=True)
            first = jnp.min(jnp.where(jnp.logical_and(cur == m, m > neg_inf), lane, LANES),
                            axis=1, keepdims=True)
            pick = lane == first
            chosen = jnp.logical_or(chosen, pick)
            cur = jnp.where(pick, neg_inf, cur)
        bias = jnp.where(jnp.logical_or(chosen, lane >= LANES // 2), 0.0, NEG_BIG)
        qa_ref[:, 2 * h * HEAD_DIM:(2 * h + 1) * HEAD_DIM] = (qr * scale).astype(BF16)
        qa_ref[:, (2 * h + 1) * HEAD_DIM:(2 * h + 2) * HEAD_DIM] = bias.astype(BF16)


def _moba_prep(proj, cos_tab, sin_tab, B, S):
    T = proj.shape[0]
    tb = MOBA_BLOCK
    nb = S // tb
    qblk = (COL_MOBA - SEG_B) * LANES // BRANCH_WIDTH
    aug = jax.ShapeDtypeStruct((T, 2 * BRANCH_WIDTH), BF16)
    row_spec = lambda c: pl.BlockSpec((tb, BRANCH_WIDTH), lambda b, i: (b * nb + i, c))
    trig_spec = pl.BlockSpec((tb, LANES), lambda b, i: (b * nb + i, 0))
    aug_spec = pl.BlockSpec((tb, 2 * BRANCH_WIDTH), lambda b, i: (b * nb + i, 0))
    kmean_spec = pl.BlockSpec((None, LANES, BRANCH_WIDTH), lambda b, i: (b, 0, 0))
    ka, va, kmean = pl.pallas_call(
        functools.partial(_moba_k_kernel, tb=tb),
        out_shape=(aug, jax.ShapeDtypeStruct((T, BRANCH_WIDTH), BF16),
                   jax.ShapeDtypeStruct((B, LANES, BRANCH_WIDTH), F32)),
        grid=(B, nb),
        in_specs=[row_spec(qblk + 1), row_spec(qblk + 2), trig_spec, trig_spec],
        out_specs=(aug_spec, pl.BlockSpec((tb, BRANCH_WIDTH), lambda b, i: (b * nb + i, 0)), kmean_spec),
        compiler_params=_cparams(2, 32),
        name="moba_k_prep",
    )(proj, proj, cos_tab, sin_tab)
    qa = pl.pallas_call(
        functools.partial(_moba_q_kernel, tb=tb, scale=HEAD_DIM ** -0.5),
        out_shape=aug,
        grid=(B, nb),
        in_specs=[row_spec(qblk), trig_spec, trig_spec, kmean_spec],
        out_specs=aug_spec,
        compiler_params=_cparams(2, 32),
        name="moba_q_prep",
    )(proj, cos_tab, sin_tab, kmean)
    return qa, ka, va


def _gla_kernel(q_ref, k_ref, v_ref, gr_ref, misc_ref, wg_ref, bg_ref, gn_ref, o_ref, state_ref, *, tb):
    C = GLA_CHUNK
    @pl.when(pl.program_id(1) == 0)
    def _():
        state_ref[...] = jnp.zeros_like(state_ref)

    la_all = _log_sigmoid(_dot(misc_ref[...], wg_ref[...], precision=HIGHEST) + bg_ref[...]) \
        * (1.0 / GLA_GATE_NORMALIZER)
    rowc = lax.broadcasted_iota(I32, (C, C), 0)
    colc = lax.broadcasted_iota(I32, (C, C), 1)
    incl = (colc <= rowc).astype(F32)
    causal = colc <= rowc
    lane = lax.broadcasted_iota(I32, (C, LANES), 1)
    row2 = lax.broadcasted_iota(I32, (2 * GLA_HEAD_K, LANES), 0)
    ones_cv = jnp.ones((C, LANES), F32)
    qscale = GLA_HEAD_K ** -0.5
    gn = gn_ref[...]
    for c in range(tb // C):
        rs = slice(c * C, (c + 1) * C)
        for p in range(2):
            ps = slice(p * LANES, (p + 1) * LANES)
            la = la_all[rs, ps]
            b = _dot(incl, la, precision=HIGHEST)
            b_last = b[C - 1:C, :]
            eb = jnp.exp(b)
            q_dec = q_ref[rs, ps].astype(F32) * qscale * eb
            k = k_ref[rs, ps].astype(F32)
            k_inv = (k * jnp.exp(-b)).astype(BF16)
            k_end = (k * jnp.exp(b_last - b)).astype(BF16)
            decay = jnp.exp(_dot_tn(la, ones_cv, precision=HIGHEST))
            st = state_ref[p * LANES:(p + 1) * LANES, :]
            new_rows = []
            for e in range(2):
                h = 2 * p + e
                own = jnp.logical_and(lane >= e * GLA_HEAD_K, lane < (e + 1) * GLA_HEAD_K)
                qh = jnp.where(own, q_dec, 0.0).astype(BF16)
                v = v_ref[rs, h * HEAD_DIM:(h + 1) * HEAD_DIM].astype(BF16)
                attn = jnp.where(causal, _dot_nt(qh, k_inv), 0.0)
                st_h = jnp.where(jnp.logical_and(row2 >= e * GLA_HEAD_K, row2 < (e + 1) * GLA_HEAD_K), st, 0.0)
                o = _dot(attn.astype(BF16), v) + _dot(qh, st_h.astype(BF16))
                y = o * lax.rsqrt(jnp.mean(o * o, axis=-1, keepdims=True) + RMS_EPS) * gn
                g = gr_ref[rs, h * HEAD_DIM:(h + 1) * HEAD_DIM].astype(F32)
                o_ref[rs, h * HEAD_DIM:(h + 1) * HEAD_DIM] = y * (g * _sigmoid(g))
                new_rows.append(_dot_tn(k_end, v))
            kv = jnp.where(row2 < GLA_HEAD_K, new_rows[0], new_rows[1])
            state_ref[p * LANES:(p + 1) * LANES, :] = decay * st + kv


def _gla(proj, misc, w_gate_pad, b_gate, gla_norm, B, S):
    T = proj.shape[0]
    tb = 512
    nb = S // tb
    kd = 2 * LANES
    rows = lambda w, c: pl.BlockSpec((tb, w), lambda b, i: (b * nb + i, c))
    const = lambda shape: pl.BlockSpec(shape, lambda b, i: (0, 0))
    return pl.pallas_call(
        functools.partial(_gla_kernel, tb=tb),
        out_shape=jax.ShapeDtypeStruct((T, BRANCH_WIDTH), F32),
        grid=(B, nb),
        in_specs=[rows(kd, (COL_GQ - SEG_B) * LANES // kd), rows(kd, (COL_GK - SEG_B) * LANES // kd),
                  rows(BRANCH_WIDTH, (COL_GV - SEG_B) * LANES // BRANCH_WIDTH),
                  rows(BRANCH_WIDTH, (COL_GR - SEG_B) * LANES // BRANCH_WIDTH),
                  rows(LANES, 0), const((LANES, kd)), const((1, kd)), const((1, HEAD_DIM))],
        out_specs=rows(BRANCH_WIDTH, 0),
        scratch_shapes=[pltpu.VMEM((kd, HEAD_DIM), F32)],
        compiler_params=_cparams(2, 32),
        name="gla",
    )(proj, proj, proj, proj, misc, w_gate_pad, b_gate, gla_norm)


def _merge_kernel(oa_ref, ob_ref, oc_ref, od_ref, g0_ref, g1_ref, g2_ref, g3_ref, w_ref, o_ref):
    acc = None
    for n, (b_ref, g_ref) in enumerate(((oa_ref, g0_ref), (ob_ref, g1_ref), (oc_ref, g2_ref), (od_ref, g3_ref))):
        term = _sigmoid(g_ref[...].astype(F32)) * _dot(b_ref[...].astype(BF16), w_ref[n])
        acc = term if acc is None else acc + term
    o_ref[...] = acc.astype(BF16)


def _merge(branches, proj, w_branch_bf16, layer):
    T = proj.shape[0]
    _, nbr, W, D = w_branch_bf16.shape
    tm = 256
    g0 = (COL_GATES - SEG_C) * LANES // D
    br_spec = pl.BlockSpec((tm, W), lambda i: (i, 0))
    gate_spec = lambda n: pl.BlockSpec((tm, D), lambda i: (i, g0 + n))
    return pl.pallas_call(
        _merge_kernel,
        out_shape=jax.ShapeDtypeStruct((T, D), BF16),
        grid=(T // tm,),
        in_specs=[br_spec] * 4 + [gate_spec(n) for n in range(4)]
                 + [pl.BlockSpec((None, nbr, W, D), lambda i: (layer, 0, 0, 0))],
        out_specs=pl.BlockSpec((tm, D), lambda i: (i, 0)),
        compiler_params=_cparams(1, 44),
        name="branch_merge",
    )(*branches, proj, proj, proj, proj, w_branch_bf16)


def _butterfly(x, lane, op):
    for s in (1, 2, 4):
        up = pltpu.roll(x, s, 1)
        down = pltpu.roll(x, LANES - s, 1)
        x = op(x, jnp.where((lane & s) != 0, up, down))
    return x


def _route_kernel(lg_ref, bias_ref, e8_ref, w8_ref, p8_ref, cnt_ref, carry_ref, *, tm):
    @pl.when(pl.program_id(0) == 0)
    def _():
        carry_ref[...] = jnp.zeros_like(carry_ref)

    neg_inf = jnp.float32(-jnp.inf)
    lane = lax.broadcasted_iota(I32, (tm, LANES), 1)
    valid = lane < N_EXPERTS
    scores = _sigmoid(lg_ref[...])
    biased = jnp.where(valid, scores + bias_ref[...], neg_inf)
    g1 = _butterfly(biased, lane, jnp.maximum)
    first = _butterfly(jnp.where(biased == g1, lane, LANES), lane, jnp.minimum)
    g2 = _butterfly(jnp.where(lane == first, neg_inf, biased), lane, jnp.maximum)
    gs = g1 + g2
    gs = jnp.where(valid, gs, pltpu.roll(gs, N_EXPERTS, 1))
    gidx = lane >> 3
    beaten = jnp.zeros((tm, LANES), I32)
    for r in range(1, N_EXPERTS // GROUP_SIZE):
        other = pltpu.roll(gs, GROUP_SIZE * r, 1)
        og = (gidx - r) & (N_EXPERTS // GROUP_SIZE - 1)
        wins = jnp.logical_or(other > gs, jnp.logical_and(other == gs, og < gidx))
        beaten = beaten + wins.astype(I32)
    cur = jnp.where(jnp.logical_and(beaten < TOPK_GROUPS, valid), biased, neg_inf)
    sel = jnp.zeros((tm, LANES), jnp.bool_)
    for _ in range(TOP_K):
        m = jnp.max(cur, axis=1, keepdims=True)
        pick = lane == jnp.min(jnp.where(cur == m, lane, LANES), axis=1, keepdims=True)
        sel = jnp.logical_or(sel, pick)
        cur = jnp.where(pick, neg_inf, cur)
    wsel = jnp.where(sel, scores, 0.0)
    wd = wsel / jnp.sum(wsel, axis=1, keepdims=True) * ROUTED_SCALE
    selb = jnp.where(sel, 1.0, 0.0).astype(BF16)
    row = lax.broadcasted_iota(I32, (tm, tm), 0)
    col = lax.broadcasted_iota(I32, (tm, tm), 1)
    pos = _dot((col < row).astype(BF16), selb) + carry_ref[0:1, :]
    total = carry_ref[0:1, :] + jnp.sum(selb.astype(F32), axis=0, keepdims=True)
    carry_ref[0:1, :] = total
    cnt_ref[...] = jnp.broadcast_to(total, cnt_ref.shape)
    r2 = lax.broadcasted_iota(I32, (LANES, LANES), 0)
    c2 = lax.broadcasted_iota(I32, (LANES, LANES), 1)
    slot = _dot(selb, (r2 < c2).astype(BF16))
    lane_f = lane.astype(F32)
    e8 = jnp.zeros((tm, LANES), F32)
    w8 = jnp.zeros((tm, LANES), F32)
    p8 = jnp.zeros((tm, LANES), F32)
    for k in range(TOP_K):
        mk = jnp.logical_and(sel, slot == k)
        put = lane == k
        e8 = jnp.where(put, jnp.sum(jnp.where(mk, lane_f, 0.0), axis=1, keepdims=True), e8)
        w8 = jnp.where(put, jnp.sum(jnp.where(mk, wd, 0.0), axis=1, keepdims=True), w8)
        p8 = jnp.where(put, jnp.sum(jnp.where(mk, pos, 0.0), axis=1, keepdims=True), p8)
    e8_ref[...] = e8.astype(I32)
    w8_ref[...] = w8
    p8_ref[...] = p8.astype(I32)


def _route(logits, router_bias_pad):
    T = logits.shape[0]
    tm = 512
    row = pl.BlockSpec((tm, LANES), lambda i: (i, 0))
    return pl.pallas_call(
        functools.partial(_route_kernel, tm=tm),
        out_shape=(jax.ShapeDtypeStruct((T, LANES), I32), jax.ShapeDtypeStruct((T, LANES), F32),
                   jax.ShapeDtypeStruct((T, LANES), I32), jax.ShapeDtypeStruct((SUBLANES, LANES), F32)),
        grid=(T // tm,),
        in_specs=[row, pl.BlockSpec((1, LANES), lambda i: (0, 0))],
        out_specs=(row, row, row, pl.BlockSpec((SUBLANES, LANES), lambda i: (0, 0))),
        scratch_shapes=[pltpu.VMEM((SUBLANES, LANES), F32)],
        compiler_params=_cparams(1, 32),
        name="route",
    )(logits, router_bias_pad)


def _dest_kernel(e8_ref, p8_ref, start_ref, d_ref):
    e8 = e8_ref[...]
    lane = lax.broadcasted_iota(I32, e8.shape, 1)
    starts = start_ref[...]
    dest = p8_ref[...]
    for k in range(TOP_K):
        hit = lane == e8[:, k:k + 1]
        base = jnp.sum(jnp.where(hit, starts, 0.0), axis=1, keepdims=True).astype(I32)
        dest = jnp.where(lane == k, dest + base, dest)
    d_ref[...] = dest


def _dest_rows(e8, p8, pad_start_lanes):
    T = e8.shape[0]
    tm = 1024
    row = pl.BlockSpec((tm, LANES), lambda i: (i, 0))
    return pl.pallas_call(
        _dest_kernel, out_shape=jax.ShapeDtypeStruct((T, LANES), I32), grid=(T // tm,),
        in_specs=[row, row, pl.BlockSpec((1, LANES), lambda i: (0, 0))], out_specs=row,
        compiler_params=_cparams(1, 32), name="dest_rows",
    )(e8, p8, pad_start_lanes)


def _token_copy(src, src_token, dst, dst_token, sem, sub):
    return pltpu.make_async_copy(src.at[pl.ds(pl.multiple_of(src_token * sub, sub), sub), :],
                                 dst.at[pl.ds(pl.multiple_of(dst_token * sub, sub), sub), :], sem)


def _dispatch_kernel(pad_end_ref, padded_ref, dest_ref, hp_ref, xs_ref, zero_ref, sem_ref, *, tm, rows, n_blocks, sub):
    @pl.when(pl.program_id(0) == 0)
    def _():
        zero_ref[...] = jnp.zeros_like(zero_ref)

        def fill(e, do_wait):
            @pl.when(padded_ref[e] > 0)
            def _():
                start = pl.multiple_of((pad_end_ref[e] - rows) * sub, rows * sub)
                cp = pltpu.make_async_copy(zero_ref, xs_ref.at[pl.ds(start, rows * sub), :], sem_ref)
                if do_wait:
                    cp.wait()
                else:
                    cp.start()

        lax.fori_loop(0, N_EXPERTS, lambda e, c: (fill(e, False), c)[1], 0)
        lax.fori_loop(0, N_EXPERTS, lambda e, c: (fill(e, True), c)[1], 0)

        def tail(b):
            return pltpu.make_async_copy(zero_ref, xs_ref.at[pl.ds(pl.multiple_of(b * rows * sub, rows * sub), rows * sub), :],
                                         sem_ref)

        n_used = pad_end_ref[N_EXPERTS - 1] // rows
        lax.fori_loop(n_used, n_blocks, lambda b, c: (tail(b).start(), c)[1], 0)
        lax.fori_loop(n_used, n_blocks, lambda b, c: (tail(b).wait(), c)[1], 0)

    def issue(t, c):
        for k in range(TOP_K):
            _token_copy(hp_ref, t, xs_ref, dest_ref[t * TOP_K + k], sem_ref, sub).start(priority=k % 2)
        return c

    def drain(t, c):
        for k in range(TOP_K):
            _token_copy(hp_ref, t, xs_ref, dest_ref[t * TOP_K + k], sem_ref, sub).wait()
        return c

    lax.fori_loop(0, tm, issue, 0)
    lax.fori_loop(0, tm, drain, 0)


def _dispatch(hp, dest_flat, pad_end, padded, n_rows, sub):
    tm = TOKEN_TILE
    T = hp.shape[0] // sub
    grid_spec = pltpu.PrefetchScalarGridSpec(
        num_scalar_prefetch=2,
        grid=(T // tm,),
        in_specs=[pl.BlockSpec((tm * TOP_K,), lambda i, pe, pd: (i,), memory_space=pltpu.SMEM),
                  pl.BlockSpec((tm * sub, LANES), lambda i, pe, pd: (i, 0))],
        out_specs=pl.BlockSpec(memory_space=pl.ANY),
        scratch_shapes=[pltpu.VMEM((EXPERT_ROWS * sub, LANES), U32), pltpu.SemaphoreType.DMA(())],
    )
    return pl.pallas_call(
        functools.partial(_dispatch_kernel, tm=tm, rows=EXPERT_ROWS, n_blocks=n_rows // EXPERT_ROWS, sub=sub),
        out_shape=jax.ShapeDtypeStruct((n_rows * sub, LANES), U32),
        grid_spec=grid_spec,
        compiler_params=_cparams(1, 32),
        name="dispatch",
    )(pad_end, padded, dest_flat, hp)


def _expert_kernel(blk_e_ref, next_e_ref, n_used_ref, x_ref, wg_hbm, wu_hbm, wd_hbm, y_ref,
                   wg_f32, wu_f32, wd_f32, wg_bf, wu_bf, wd_bf, sem_ref, slot_ref, *, layer, rows, width):
    i = pl.program_id(0)
    e = blk_e_ref[i]
    active = i < n_used_ref[0]
    first = jnp.logical_or(i == 0, e != blk_e_ref[jnp.maximum(i - 1, 0)])

    def fetch(expert, slot):
        return [pltpu.make_async_copy(w.at[layer, expert], buf.at[slot], sem_ref.at[slot])
                for w, buf in ((wg_hbm, wg_f32), (wu_hbm, wu_f32), (wd_hbm, wd_f32))]

    @pl.when(i == 0)
    def _():
        slot_ref[0] = 0
        for cp in fetch(e, 0):
            cp.start()

    @pl.when(jnp.logical_and(active, first))
    def _():
        slot = slot_ref[0]
        for cp in fetch(e, slot):
            cp.wait()
        nxt = next_e_ref[i]

        @pl.when(nxt >= 0)
        def _():
            for cp in fetch(nxt, 1 - slot):
                cp.start()

        wg_bf[...] = wg_f32[slot].astype(BF16)
        wu_bf[...] = wu_f32[slot].astype(BF16)
        wd_bf[...] = wd_f32[slot].astype(BF16)
        slot_ref[0] = 1 - slot

    @pl.when(active)
    def _():
        x = _unpack_pairs(_load_token_rows(x_ref, 0, rows, width))
        g = _dot(x, wg_bf[...])
        u = _dot(x, wu_bf[...])
        hb = (g * _sigmoid(g)) * u
        _store_token_rows(y_ref, _pack_pairs(_dot(hb.astype(BF16), wd_bf[...])))

    @pl.when(jnp.logical_not(active))
    def _():
        y_ref[...] = jnp.zeros_like(y_ref)


def _experts(xs, blk_e, next_e, n_used, w_gate, w_up, w_down, layer):
    _, E, D, Hx = w_gate.shape
    W = D // 2
    sub = _rows_per_token(W)
    M = EXPERT_ROWS
    nblk = xs.shape[0] // (M * sub)
    xrow = lambda i, be, ne, nu: (jnp.minimum(i, nu[0] - 1), 0)
    hbm = pl.BlockSpec(memory_space=pl.ANY)
    grid_spec = pltpu.PrefetchScalarGridSpec(
        num_scalar_prefetch=3,
        grid=(nblk,),
        in_specs=[pl.BlockSpec((M * sub, LANES), xrow), hbm, hbm, hbm],
        out_specs=pl.BlockSpec((M * sub, LANES), lambda i, be, ne, nu: (i, 0)),
        scratch_shapes=[pltpu.VMEM((2, D, Hx), F32), pltpu.VMEM((2, D, Hx), F32), pltpu.VMEM((2, Hx, D), F32),
                        pltpu.VMEM((D, Hx), BF16), pltpu.VMEM((D, Hx), BF16), pltpu.VMEM((Hx, D), BF16),
                        pltpu.SemaphoreType.DMA((2,)), pltpu.SMEM((1,), I32)],
    )
    return pl.pallas_call(
        functools.partial(_expert_kernel, layer=layer, rows=M, width=W),
        out_shape=jax.ShapeDtypeStruct(xs.shape, U32),
        grid_spec=grid_spec,
        compiler_params=_cparams(1, 52),
        name="experts",
    )(blk_e, next_e, n_used, xs, w_gate, w_up, w_down)


def _next_expert(blk_e, cnt):
    E = cnt.shape[0]
    idx = jnp.where(cnt > 0, jnp.arange(E, dtype=I32), E)
    later = jnp.concatenate([lax.cummin(idx, axis=0, reverse=True)[1:], jnp.full((1,), E, I32)])
    return jnp.where(later < E, later, -1)[blk_e]


def _unpack_pairs_f32(w):
    a = lax.bitcast_convert_type(w & jnp.uint32(0xFFFF0000), F32)
    b = lax.bitcast_convert_type(w << 16, F32)
    return jnp.concatenate([a, b], axis=1)


def _combine_kernel(dcur_ref, dnext_ref, x_ref, ysh_ref, w8_ref, g2_ref, fn_ref, ys_ref, o_ref, buf_ref, sem_ref,
                    *, tm, n_steps, width, final_norm):
    i = pl.program_id(0)
    slot = lax.rem(i, 2)
    sub = _rows_per_token(width)

    def issue(dref, s):
        def body(t, c):
            for k in range(TOP_K):
                _token_copy(ys_ref, dref[t * TOP_K + k], buf_ref.at[s], k * tm + t, sem_ref.at[s], sub).start(priority=k % 2)
            return c
        lax.fori_loop(0, tm, body, 0)

    @pl.when(i == 0)
    def _():
        issue(dcur_ref, 0)

    @pl.when(i + 1 < n_steps)
    def _():
        issue(dnext_ref, 1 - slot)

    def drain(t, c):
        for k in range(TOP_K):
            _token_copy(ys_ref, 0, buf_ref.at[slot], 0, sem_ref.at[slot], sub).wait()
        return c

    lax.fori_loop(0, tm, drain, 0)
    w8 = w8_ref[...]
    y = _unpack_pairs_f32(_load_token_rows(ysh_ref, 0, tm, width))
    gathered = buf_ref.at[slot]
    for k in range(TOP_K):
        y = y + w8[:, k:k + 1] * _unpack_pairs_f32(_load_token_rows(gathered, k * tm, tm, width))
    out = x_ref[...] + g2_ref[...] * y
    if final_norm:
        out = out * lax.rsqrt(jnp.mean(out * out, axis=-1, keepdims=True) + RMS_EPS) * fn_ref[...]
    o_ref[...] = out


def _combine(x, ysh, ys, dest_flat, w8, gate2, final_g, S, final_norm):
    T, D = x.shape
    W = D // 2
    sub = _rows_per_token(W)
    tm = TOKEN_TILE
    nb = S // tm
    n_steps = T // tm
    row = lambda i: (i, 0)
    return pl.pallas_call(
        functools.partial(_combine_kernel, tm=tm, n_steps=n_steps, width=W, final_norm=final_norm),
        out_shape=jax.ShapeDtypeStruct((T, D), F32),
        grid=(n_steps,),
        in_specs=[pl.BlockSpec((tm * TOP_K,), lambda i: (i,), memory_space=pltpu.SMEM),
                  pl.BlockSpec((tm * TOP_K,), lambda i: (jnp.minimum(i + 1, n_steps - 1),), memory_space=pltpu.SMEM),
                  pl.BlockSpec((tm, D), row), pl.BlockSpec((tm * sub, LANES), row),
                  pl.BlockSpec((tm, LANES), row),
                  pl.BlockSpec((None, 1, D), lambda i: (i // nb, 0, 0)),
                  pl.BlockSpec((1, D), lambda i: (0, 0)),
                  pl.BlockSpec(memory_space=pl.ANY)],
        out_specs=pl.BlockSpec((tm, D), row),
        scratch_shapes=[pltpu.VMEM((2, TOP_K * tm * sub, LANES), U32), pltpu.SemaphoreType.DMA((2,))],
        compiler_params=_cparams(1, 44),
        name="combine",
    )(dest_flat, dest_flat, x, ysh, w8, gate2, final_g, ys)


def _pad_lanes(v, offset=0):
    out = jnp.zeros((1, LANES), F32)
    return out.at[0, offset:offset + v.shape[0]].set(v.astype(F32))


def kernel(x, c, positions, attn_norm, w_ada, b_ada, w_in, fox_bias, gla_w_gate, gla_b_gate, gla_norm,
           w_branch, w_out, ffn_norm, w_router, router_bias, w_exp_gate, w_exp_up, w_exp_down,
           w_sh_gate, w_sh_up, w_sh_down, final_norm):
    B, S, D = x.shape
    L = w_ada.shape[0]
    T = B * S
    E = N_EXPERTS
    M = EXPERT_ROWS
    xf = x.reshape(T, D)

    c_pad = jnp.zeros((SUBLANES, D), F32).at[:B].set(c)
    mod = _ada_mod(c_pad, w_ada, b_ada)

    half = ROPE_DIM // 2
    inv_freq = jnp.power(ROPE_THETA, -jnp.arange(half, dtype=F32) * 2.0 / ROPE_DIM)
    freq_lanes = _pad_lanes(jnp.concatenate([inv_freq, inv_freq]))
    cos_tab, sin_tab = _rope_tables(positions.reshape(T, 1), freq_lanes)

    w_seg_a, w_seg_b, w_seg_c, w_misc_t = _input_weights(w_in)
    w_branch_bf16 = w_branch.astype(BF16)
    n_blocks = (T * TOP_K + E * (M - 1) + M - 1) // M
    P = n_blocks * M

    for l in range(L):
        m6 = mod[l, :B].reshape(B, 6, 1, D)
        shift1, scale1, gate1, shift2, scale2, gate2 = (m6[:, n] for n in range(6))

        h1 = _norm_mod(xf, attn_norm[l].reshape(1, D), scale1, shift1, S)
        proj_a = _nt_matmul(h1, w_seg_a, l, tm=1024, tn=1536, out_dtype=BF16)
        proj_b = _nt_matmul(h1, w_seg_b, l, tm=1024, tn=1536, out_dtype=BF16)
        proj_c = _nt_matmul(h1, w_seg_c, l, tm=1024, tn=2048, out_dtype=BF16)
        misc = _nt_matmul(h1, w_misc_t, l, tm=1024, tn=LANES, out_dtype=F32)

        o_a = _sb_attention(proj_a, B, S, COL_SB)
        qa, ka, va = _fox_prep(proj_a, misc, _pad_lanes(fox_bias[l], MISC_FOX_F), B, S)
        o_b = _flash_attention(qa, ka, va, B, S)
        qa, ka, va = _moba_prep(proj_b, cos_tab, sin_tab, B, S)
        o_c = _flash_attention(qa, ka, va, B, S)
        wg_pad = jnp.zeros((LANES, gla_w_gate.shape[2]), F32).at[MISC_GLR:MISC_GLR + GLA_GATE_RANK].set(gla_w_gate[l])
        o_d = _gla(proj_b, misc, wg_pad, gla_b_gate[l].reshape(1, -1), gla_norm[l].reshape(1, -1), B, S)

        merged = _merge((o_a, o_b, o_c, o_d), proj_c, w_branch_bf16, l)
        xf = _matmul(merged, w_out, tm=512, tn=1024, residual=(xf, gate1), S=S, layer=l)

        wr_pad = jnp.zeros((D, LANES), F32).at[:, :E].set(w_router[l])
        hp, logits = _norm_mod(xf, ffn_norm[l].reshape(1, D), scale2, shift2, S, w_router_pad=wr_pad)
        e8, w8, p8, counts = _route(logits, _pad_lanes(router_bias[l]))
        cnt = counts[0, :E].astype(I32)
        padded = (cnt + M - 1) // M * M
        pad_end = jnp.cumsum(padded)
        pad_start = pad_end - padded
        dest = _dest_rows(e8, p8, _pad_lanes(pad_start))[:, :TOP_K].reshape(T * TOP_K)
        blk_start = jnp.arange(n_blocks, dtype=I32) * M
        blk_e = jnp.minimum(jnp.sum((pad_end[None, :] <= blk_start[:, None]).astype(I32), axis=1), E - 1)
        n_used = (pad_end[E - 1:] // M).astype(I32)

        xs = _dispatch(hp, dest, pad_end.astype(I32), padded.astype(I32), P, _rows_per_token(D // 2))
        ys = _experts(xs, blk_e, _next_expert(blk_e, cnt), n_used, w_exp_gate, w_exp_up, w_exp_down, l)
        ysh = _experts(hp, jnp.zeros((T // M,), I32), jnp.full((T // M,), -1, I32), jnp.full((1,), T // M, I32),
                       w_sh_gate[:, None], w_sh_up[:, None], w_sh_down[:, None], l)
        last = l == L - 1
        xf = _combine(xf, ysh, ys, dest, w8, gate2, final_norm.reshape(1, D), S, final_norm=last)

    return xf.reshape(B, S, D)
```
